```python
import math
import jax
import jax.numpy as jnp
from jax import lax
import numpy as np

D_MODEL = 1024
BATCH = 8
SEQ = 4096
DEPTH = 4

CHUNK = 64
N_A_LAYERS = DEPTH // 2
N_B_LAYERS = DEPTH - N_A_LAYERS
EPS = 1e-6
GDN_HEAD_DIM = 128
GDN_HEADS = D_MODEL // GDN_HEAD_DIM
GDN_INNER = GDN_HEADS * GDN_HEAD_DIM
GDN_PROJ = 4 * GDN_INNER + 2 * GDN_HEADS
CONV_WIDTH = 4
FOX_HEAD_DIM = 64
FOX_HEADS = D_MODEL // FOX_HEAD_DIM
FOX_INNER = FOX_HEADS * FOX_HEAD_DIM
Q_BLOCK = 128
N_GROUPS = 4
EXPERTS_PER_GROUP = 8
N_EXPERTS = N_GROUPS * EXPERTS_PER_GROUP
TOP_K = 2
D_EXPERT = D_MODEL // 2
EXPERT_BLOCK = 256

kernel_name = 'yoco_gdn_fox_hier_moe_adaln'


def rms_norm(x, gain):
    xf = x.astype(jnp.float32)
    y = xf * lax.rsqrt(jnp.mean(xf * xf, axis=-1, keepdims=True) + EPS)
    return (y * gain.astype(jnp.float32)).astype(x.dtype)


def l2_norm(x):
    xf = x.astype(jnp.float32)
    return xf * lax.rsqrt(jnp.sum(xf * xf, axis=-1, keepdims=True) + EPS)


def modulate(x, gain, shift, scale):
    return rms_norm(x, gain) * (1 + scale) + shift


def causal_depthwise_conv(x, w):
    width, ch = w.shape
    return lax.conv_general_dilated(
        x, w[:, None, :].astype(x.dtype), window_strides=(1,), padding=[(width - 1, 0)],
        dimension_numbers=('NWC', 'WIO', 'NWC'), feature_group_count=ch)


def chunked_gated_delta_rule(q, k, v, g, beta):
    bsz, seqlen, nh, dk = q.shape
    dv = v.shape[-1]
    n = seqlen // CHUNK

    def blocks(t):
        return t.reshape(bsz, n, CHUNK, nh, -1).transpose(1, 0, 3, 2, 4)

    qc, kc, vc = blocks(q), blocks(k), blocks(v)
    gc = g.reshape(bsz, n, CHUNK, nh).transpose(1, 0, 3, 2)
    bc = beta.reshape(bsz, n, CHUNK, nh).transpose(1, 0, 3, 2)
    gcum = jnp.cumsum(gc, axis=-1)
    incl = jnp.tril(jnp.ones((CHUNK, CHUNK), dtype=bool))
    strict = jnp.tril(jnp.ones((CHUNK, CHUNK), dtype=bool), k=-1)
    decay = jnp.exp(jnp.where(incl, gcum[..., :, None] - gcum[..., None, :], -jnp.inf))
    kbeta = kc * bc[..., None]
    a_mat = jnp.where(strict, jnp.einsum('nbhid,nbhjd->nbhij', kbeta, kc) * decay, 0.0)
    lower = a_mat + jnp.eye(CHUNK, dtype=a_mat.dtype)
    w = lax.linalg.triangular_solve(lower, kbeta * jnp.exp(gcum)[..., None],
                                    left_side=True, lower=True, unit_diagonal=True)
    u = lax.linalg.triangular_solve(lower, vc * bc[..., None],
                                    left_side=True, lower=True, unit_diagonal=True)

    def step(state, inp):
        q_i, k_i, u_i, w_i, g_i, decay_i = inp
        v_new = u_i - jnp.einsum('bhid,bhdv->bhiv', w_i, state)
        o_inter = jnp.einsum('bhid,bhdv->bhiv', q_i * jnp.exp(g_i)[..., None], state)
        attn = jnp.einsum('bhid,bhjd->bhij', q_i, k_i) * decay_i
        o_i = o_inter + jnp.einsum('bhij,bhjv->bhiv', attn, v_new)
        g_last = g_i[..., -1]
        k_dec = k_i * jnp.exp(g_last[..., None] - g_i)[..., None]
        state = state * jnp.exp(g_last)[..., None, None] + jnp.einsum('bhid,bhiv->bhdv', k_dec, v_new)
        return state, o_i

    state0 = jnp.zeros((bsz, nh, dk, dv), jnp.float32)
    _, o = lax.scan(step, state0, (qc, kc, u, w, gcum, decay))
    return o.transpose(1, 0, 3, 2, 4).reshape(bsz, seqlen, nh, dv)


def gdn_mixer(h, w_in, conv_w, a_log, dt_bias, out_norm_g, w_out):
    bsz, seqlen, _ = h.shape
    proj = h @ w_in
    qkv = jax.nn.silu(causal_depthwise_conv(proj[..., :3 * GDN_INNER], conv_w))
    z = proj[..., 3 * GDN_INNER:4 * GDN_INNER]
    a_in = proj[..., 4 * GDN_INNER:4 * GDN_INNER + GDN_HEADS].astype(jnp.float32)
    b_in = proj[..., 4 * GDN_INNER + GDN_HEADS:].astype(jnp.float32)
    q, k, v = jnp.split(qkv, 3, axis=-1)
    q = l2_norm(q.reshape(bsz, seqlen, GDN_HEADS, GDN_HEAD_DIM)) * (GDN_HEAD_DIM ** -0.5)
    k = l2_norm(k.reshape(bsz, seqlen, GDN_HEADS, GDN_HEAD_DIM))
    v = v.reshape(bsz, seqlen, GDN_HEADS, GDN_HEAD_DIM).astype(jnp.float32)
    beta = jax.nn.sigmoid(b_in)
    g = -jnp.exp(a_log.astype(jnp.float32)) * jax.nn.softplus(a_in + dt_bias.astype(jnp.float32))
    o = chunked_gated_delta_rule(q, k, v, g, beta)
    zf = z.reshape(bsz, seqlen, GDN_HEADS, GDN_HEAD_DIM).astype(jnp.float32)
    o = rms_norm(o, out_norm_g) * jax.nn.silu(zf)
    return o.reshape(bsz, seqlen, GDN_INNER).astype(h.dtype) @ w_out


def shared_kv(x_mid, c_act, kv_mod_w, kv_mod_b, kv_norm_g, kv_w, kv_forget_b, k_norm_g):
    bsz, seqlen, _ = x_mid.shape
    shift, scale = jnp.split(c_act @ kv_mod_w + kv_mod_b, 2, axis=-1)
    h = modulate(x_mid, kv_norm_g, shift[:, None, :], scale[:, None, :])
    kvf = h @ kv_w
    k = rms_norm(kvf[..., :FOX_INNER].reshape(bsz, seqlen, FOX_HEADS, FOX_HEAD_DIM), k_norm_g)
    v = kvf[..., FOX_INNER:2 * FOX_INNER].reshape(bsz, seqlen, FOX_HEADS, FOX_HEAD_DIM)
    log_f = jax.nn.log_sigmoid(kvf[..., 2 * FOX_INNER:].astype(jnp.float32) + kv_forget_b.astype(jnp.float32))
    fcum = jnp.cumsum(log_f, axis=1).transpose(0, 2, 1)
    return k.transpose(0, 2, 1, 3), v.transpose(0, 2, 1, 3), fcum


def fox_mixer(h, w_qz, q_norm_g, w_out, k, v, fcum):
    bsz, seqlen, _ = h.shape
    qz = h @ w_qz
    q = rms_norm(qz[..., :FOX_INNER].reshape(bsz, seqlen, FOX_HEADS, FOX_HEAD_DIM), q_norm_g)
    q = q.transpose(0, 2, 1, 3)
    z = qz[..., FOX_INNER:].reshape(bsz, seqlen, FOX_HEADS, FOX_HEAD_DIM)
    scale = FOX_HEAD_DIM ** -0.5
    outs = []
    for blk in range(seqlen // Q_BLOCK):
        lo, hi = blk * Q_BLOCK, (blk + 1) * Q_BLOCK
        s = jnp.einsum('bhqd,bhkd->bhqk', q[:, :, lo:hi], k[:, :, :hi]).astype(jnp.float32) * scale
        s = s + fcum[:, :, lo:hi, None] - fcum[:, :, None, :hi]
        causal = (lo + jnp.arange(Q_BLOCK))[:, None] >= jnp.arange(hi)[None, :]
        p = jax.nn.softmax(jnp.where(causal, s, -jnp.inf), axis=-1)
        outs.append(jnp.einsum('bhqk,bhkd->bhqd', p.astype(v.dtype), v[:, :, :hi]))
    o = jnp.concatenate(outs, axis=2).transpose(0, 2, 1, 3) * jax.nn.sigmoid(z)
    return o.reshape(bsz, seqlen, FOX_INNER) @ w_out


def grouped_expert_mlp(ht, expert_idx, weights, w_gate, w_up, w_down):
    n_tok, d = ht.shape
    m = n_tok * TOP_K
    flat_e = expert_idx.reshape(m)
    order = jnp.argsort(flat_e)
    sorted_e = flat_e[order]
    counts = jnp.bincount(flat_e, length=N_EXPERTS)
    padded = (counts + EXPERT_BLOCK - 1) // EXPERT_BLOCK * EXPERT_BLOCK
    pad_end = jnp.cumsum(padded)
    pad_start = pad_end - padded
    start = jnp.cumsum(counts) - counts
    dest = pad_start[sorted_e] + jnp.arange(m) - start[sorted_e]
    n_blocks = -(-m // EXPERT_BLOCK) + N_EXPERTS
    rows = n_blocks * EXPERT_BLOCK
    row_token = jnp.zeros((rows,), jnp.int32).at[dest].set((order // TOP_K).astype(jnp.int32))
    block_expert = jnp.minimum(
        jnp.searchsorted(pad_end, jnp.arange(n_blocks) * EXPERT_BLOCK, side='right'), N_EXPERTS - 1)
    xb = ht[row_token].reshape(n_blocks, EXPERT_BLOCK, d)

    def expert_block(args):
        xblk, e = args
        return (jax.nn.silu(xblk @ w_gate[e]) * (xblk @ w_up[e])) @ w_down[e]

    yb = lax.map(expert_block, (xb, block_expert)).reshape(rows, d)
    y_assign = jnp.zeros((m, d), yb.dtype).at[order].set(yb[dest])
    return jnp.einsum('tkd,tk->td', y_assign.reshape(n_tok, TOP_K, d), weights.astype(yb.dtype))


def hier_moe(h, w_group, b_group, w_expert, b_expert, w_gate, w_up, w_down):
    bsz, seqlen, d = h.shape
    n_tok = bsz * seqlen
    ht = h.reshape(n_tok, d)
    group_logits = (ht @ w_group).astype(jnp.float32) + b_group.astype(jnp.float32)
    group_gate, group_idx = lax.top_k(jax.nn.softmax(group_logits, axis=-1), 1)
    expert_logits = ((ht @ w_expert).astype(jnp.float32) + b_expert.astype(jnp.float32)
                     ).reshape(n_tok, N_GROUPS, EXPERTS_PER_GROUP)
    sel = jnp.broadcast_to(group_idx[:, :, None], (n_tok, 1, EXPERTS_PER_GROUP))
    in_group = jnp.take_along_axis(expert_logits, sel, axis=1)[:, 0]
    top_p, top_i = lax.top_k(jax.nn.softmax(in_group, axis=-1), TOP_K)
    top_p = top_p / jnp.sum(top_p, axis=-1, keepdims=True)
    weights = group_gate * top_p
    expert_idx = group_idx * EXPERTS_PER_GROUP + top_i
    y = grouped_expert_mlp(ht, expert_idx, weights, w_gate, w_up, w_down)
    return y.reshape(bsz, seqlen, d)


def setup_inputs(seed: int = 0) -> dict:
    key = jax.random.key(seed)
    ks = jax.random.split(key, 32)
    f32 = jnp.float32
    d = D_MODEL

    def nrm(k, shape, fan_in, gain=1.0):
        return jax.random.normal(k, shape, f32) * (gain * fan_in ** -0.5)

    def gain_init(k, shape):
        return 1.0 + 0.02 * jax.random.normal(k, shape, f32)

    dt = jnp.exp(jax.random.uniform(ks[9], (N_A_LAYERS, GDN_HEADS), f32, math.log(1e-3), math.log(1e-1)))
    return {
        'x': jax.random.normal(ks[0], (BATCH, SEQ, d), f32),
        'c': jax.random.normal(ks[1], (BATCH, d), f32),
        'mod_w': nrm(ks[2], (DEPTH, d, 6 * d), d, 0.5),
        'mod_b': 0.02 * jax.random.normal(ks[3], (DEPTH, 6 * d), f32),
        'norm_mix_g': gain_init(ks[4], (DEPTH, d)),
        'norm_ffn_g': gain_init(ks[5], (DEPTH, d)),
        'gdn_w_in': nrm(ks[6], (N_A_LAYERS, d, GDN_PROJ), d),
        'gdn_conv_w': nrm(ks[7], (N_A_LAYERS, CONV_WIDTH, 3 * GDN_INNER), CONV_WIDTH),
        'gdn_a_log': jnp.log(jax.random.uniform(ks[8], (N_A_LAYERS, GDN_HEADS), f32, 1.0, 16.0)),
        'gdn_dt_bias': dt + jnp.log(-jnp.expm1(-dt)),
        'gdn_out_norm_g': gain_init(ks[10], (N_A_LAYERS, GDN_HEAD_DIM)),
        'gdn_w_out': nrm(ks[11], (N_A_LAYERS, GDN_INNER, d), GDN_INNER),
        'kv_mod_w': nrm(ks[12], (d, 2 * d), d, 0.5),
        'kv_mod_b': 0.02 * jax.random.normal(ks[13], (2 * d,), f32),
        'kv_norm_g': gain_init(ks[14], (d,)),
        'kv_w': nrm(ks[15], (d, 2 * FOX_INNER + FOX_HEADS), d),
        'kv_forget_b': jax.random.uniform(ks[16], (FOX_HEADS,), f32, 1.0, 5.0),
        'k_norm_g': gain_init(ks[17], (FOX_HEAD_DIM,)),
        'fox_w_qz': nrm(ks[18], (N_B_LAYERS, d, 2 * FOX_INNER), d),
        'fox_q_norm_g': gain_init(ks[19], (N_B_LAYERS, FOX_HEAD_DIM)),
        'fox_w_out': nrm(ks[20], (N_B_LAYERS, FOX_INNER, d), FOX_INNER),
        'moe_w_group': nrm(ks[21], (DEPTH, d, N_GROUPS), d),
        'moe_b_group': 0.01 * jax.random.normal(ks[22], (DEPTH, N_GROUPS), f32),
        'moe_w_expert': nrm(ks[23], (DEPTH, d, N_EXPERTS), d),
        'moe_b_expert': 0.01 * jax.random.normal(ks[24], (DEPTH, N_EXPERTS), f32),
        'moe_w_gate': nrm(ks[25], (DEPTH, N_EXPERTS, d, D_EXPERT), d),
        'moe_w_up': nrm(ks[26], (DEPTH, N_EXPERTS, d, D_EXPERT), d),
        'moe_w_down': nrm(ks[27], (DEPTH, N_EXPERTS, D_EXPERT, d), D_EXPERT),
    }


def reference(x, c, mod_w, mod_b, norm_mix_g, norm_ffn_g, gdn_w_in, gdn_conv_w, gdn_a_log, gdn_dt_bias,
              gdn_out_norm_g, gdn_w_out, kv_mod_w, kv_mod_b, kv_norm_g, kv_w, kv_forget_b, k_norm_g,
              fox_w_qz, fox_q_norm_g, fox_w_out, moe_w_group, moe_b_group, moe_w_expert, moe_b_expert,
              moe_w_gate, moe_w_up, moe_w_down):
    c_act = jax.nn.silu(c)
    k_sh = v_sh = fcum = None
    for layer in range(DEPTH):
        mod = c_act @ mod_w[layer] + mod_b[layer]
        sh1, sc1, gt1, sh2, sc2, gt2 = [m[:, None, :] for m in jnp.split(mod, 6, axis=-1)]
        h = modulate(x, norm_mix_g[layer], sh1, sc1)
        if layer < N_A_LAYERS:
            y = gdn_mixer(h, gdn_w_in[layer], gdn_conv_w[layer], gdn_a_log[layer], gdn_dt_bias[layer],
                          gdn_out_norm_g[layer], gdn_w_out[layer])
        else:
            j = layer - N_A_LAYERS
            y = fox_mixer(h, fox_w_qz[j], fox_q_norm_g[j], fox_w_out[j], k_sh, v_sh, fcum)
        x = x + gt1 * y
        h = modulate(x, norm_ffn_g[layer], sh2, sc2)
        x = x + gt2 * hier_moe(h, moe_w_group[layer], moe_b_group[layer], moe_w_expert[layer],
                               moe_b_expert[layer], moe_w_gate[layer], moe_w_up[layer], moe_w_down[layer])
        if layer == N_A_LAYERS - 1:
            k_sh, v_sh, fcum = shared_kv(x, c_act, kv_mod_w, kv_mod_b, kv_norm_g, kv_w, kv_forget_b, k_norm_g)
    return x
```

```python
import functools

import jax
import jax.numpy as jnp
from jax import lax
from jax.experimental import pallas as pl
from jax.experimental.pallas import tpu as pltpu

F32 = jnp.float32
BF16 = jnp.bfloat16
I32 = jnp.int32

EPS = 1e-6
GDN_CHUNK = 64
GDN_HEAD_DIM = 128
FOX_HEAD_DIM = 64
TOP_K = 2
LANES = 128
VMEM_LIMIT = 56 * 1024 * 1024
HIGHEST = lax.Precision.HIGHEST

NT_DIMS = (((1,), (1,)), ((), ()))
TN_DIMS = (((0,), (0,)), ((), ()))


def _params(*sem):
    return pltpu.CompilerParams(dimension_semantics=sem, vmem_limit_bytes=VMEM_LIMIT)


def _sigmoid(x):
    return 1.0 / (1.0 + jnp.exp(-x))


def _silu(x):
    return x * _sigmoid(x)


def _softplus(x):
    return jnp.maximum(x, 0.0) + jnp.log(1.0 + jnp.exp(-jnp.abs(x)))


def _modulated(x, gain, scale, shift):
    ms = jnp.mean(x * x, axis=-1, keepdims=True)
    y = x * lax.rsqrt(ms + EPS)
    return (y * gain) * (1.0 + scale) + shift


def _bdot(a, b, dims=None):
    a = a.astype(BF16)
    b = b.astype(BF16)
    if dims is None:
        return jnp.dot(a, b, preferred_element_type=F32)
    return lax.dot_general(a, b, dims, preferred_element_type=F32)


def _hdot(a, b, dims=None):
    if dims is None:
        return jnp.dot(a, b, preferred_element_type=F32, precision=HIGHEST)
    return lax.dot_general(a, b, dims, preferred_element_type=F32, precision=HIGHEST)


def _iota2(shape, dim):
    return lax.broadcasted_iota(I32, shape, dim)


def _mod_kernel(c_ref, w_ref, b_ref, o_ref):
    c = c_ref[...]
    o_ref[...] = _hdot(_silu(c), w_ref[...]) + b_ref[...]


def _mod_vectors(c, w, b):
    n_layers, d, n = w.shape
    bsz = c.shape[0]
    tn = 1536 if n % 1536 == 0 else n
    return pl.pallas_call(
        _mod_kernel,
        grid=(n_layers, n // tn),
        in_specs=[
            pl.BlockSpec((bsz, d), lambda l, j: (0, 0)),
            pl.BlockSpec((None, d, tn), lambda l, j: (l, 0, j)),
            pl.BlockSpec((None, 1, tn), lambda l, j: (l, 0, j)),
        ],
        out_specs=pl.BlockSpec((None, bsz, tn), lambda l, j: (l, 0, j)),
        out_shape=jax.ShapeDtypeStruct((n_layers, bsz, n), F32),
        compiler_params=_params("parallel", "parallel"),
        name="mod_vectors",
    )(c, w, b.reshape(n_layers, 1, n))


def _gdn_in_kernel(x_ref, mod_ref, g_ref, w_ref, wab_ref, o_ref, oab_ref, *, d):
    h = _modulated(x_ref[...], g_ref[...], mod_ref[:, d:2 * d], mod_ref[:, 0:d]).astype(BF16)
    o_ref[...] = jnp.dot(h, w_ref[...], preferred_element_type=F32).astype(o_ref.dtype)
    oab_ref[...] = jnp.dot(h, wab_ref[...], preferred_element_type=F32)


def _gdn_in_proj(x, mod, gain, w_main, w_ab, tm):
    bsz, seq, d = x.shape
    n = w_main.shape[1]
    nab = w_ab.shape[1]
    return pl.pallas_call(
        functools.partial(_gdn_in_kernel, d=d),
        grid=(bsz, seq // tm),
        in_specs=[
            pl.BlockSpec((None, tm, d), lambda b, i: (b, i, 0)),
            pl.BlockSpec((None, 1, mod.shape[-1]), lambda b, i: (b, 0, 0)),
            pl.BlockSpec((1, d), lambda b, i: (0, 0)),
            pl.BlockSpec((d, n), lambda b, i: (0, 0)),
            pl.BlockSpec((d, nab), lambda b, i: (0, 0)),
        ],
        out_specs=[
            pl.BlockSpec((None, tm, n), lambda b, i: (b, i, 0)),
            pl.BlockSpec((None, tm, nab), lambda b, i: (b, i, 0)),
        ],
        out_shape=[
            jax.ShapeDtypeStruct((bsz, seq, n), BF16),
            jax.ShapeDtypeStruct((bsz, seq, nab), F32),
        ],
        compiler_params=_params("parallel", "parallel"),
        name="gdn_in_proj",
    )(x, mod, gain, w_main, w_ab)


def _unit_lower_inverse(a_strict, eye):
    n = a_strict.shape[0]
    p = -a_strict
    t = eye + p
    span = 2
    while span < n:
        p = _bdot(p, p)
        t = t + _bdot(t, p)
        span *= 2
    return t


def _gdn_kernel(qkv_ref, z_ref, ab_ref, cw_ref, alog_ref, dtb_ref, og_ref, o_ref,
                xpad_ref, state_ref, *, tb, heads, dh, width):
    inner = heads * dh
    halo = 8
    i = pl.program_id(1)

    @pl.when(i == 0)
    def _():
        xpad_ref[0:halo, :] = jnp.zeros((halo, 3 * inner), F32)
        state_ref[...] = jnp.zeros(state_ref.shape, F32)

    xpad_ref[halo:halo + tb, :] = qkv_ref[...].astype(F32)
    cw = cw_ref[...]
    conv = xpad_ref[halo:halo + tb, :] * cw[width - 1:width, :]
    for s in range(1, width):
        conv = conv + xpad_ref[halo - s:halo - s + tb, :] * cw[width - 1 - s:width - s, :]
    xpad_ref[0:halo, :] = xpad_ref[tb:tb + halo, :]
    qkv = _silu(conv)

    ab = ab_ref[...]
    beta_all = _sigmoid(ab[:, heads:2 * heads])
    g_all = -jnp.exp(alog_ref[...]) * _softplus(ab[:, 0:heads] + dtb_ref[...])
    og = og_ref[...]

    c = GDN_CHUNK
    row = _iota2((c, c), 0)
    col = _iota2((c, c), 1)
    incl = row >= col
    strict = row > col
    eye = jnp.where(row == col, 1.0, 0.0).astype(F32)
    tri = jnp.where(incl, 1.0, 0.0).astype(F32)
    eye_h = jnp.where(_iota2((heads, heads), 0) == _iota2((heads, heads), 1), 1.0, 0.0).astype(F32)

    for ci in range(tb // c):
        r0 = ci * c
        gcum = _hdot(tri, g_all[r0:r0 + c, :])
        gcum_t = _hdot(eye_h, gcum, NT_DIMS)
        for h in range(heads):
            q = qkv[r0:r0 + c, h * dh:(h + 1) * dh]
            k = qkv[r0:r0 + c, inner + h * dh:inner + (h + 1) * dh]
            v = qkv[r0:r0 + c, 2 * inner + h * dh:2 * inner + (h + 1) * dh]
            q = q * lax.rsqrt(jnp.sum(q * q, axis=-1, keepdims=True) + EPS) * (dh ** -0.5)
            k = k * lax.rsqrt(jnp.sum(k * k, axis=-1, keepdims=True) + EPS)
            bcol = beta_all[r0:r0 + c, h:h + 1]
            gcol = gcum[:, h:h + 1]
            grow = gcum_t[h:h + 1, :]
            glast = gcum[c - 1:c, h:h + 1]
            decay = jnp.exp(jnp.where(incl, gcol - grow, -jnp.inf))
            kb = k * bcol
            a_mat = jnp.where(strict, _bdot(kb, k, NT_DIMS) * decay, 0.0)
            t_inv = _unit_lower_inverse(a_mat, eye)
            egc = jnp.exp(gcol)
            wu = _bdot(t_inv, jnp.concatenate([kb * egc, v * bcol], axis=1))
            w = wu[:, 0:dh]
            u = wu[:, dh:2 * dh]
            st = state_ref[h]
            ws_qs = _bdot(jnp.concatenate([w, q * egc], axis=0), st)
            v_new = u - ws_qs[0:c, :]
            attn = _bdot(q, k, NT_DIMS) * decay
            o = ws_qs[c:2 * c, :] + _bdot(attn, v_new)
            k_dec = k * jnp.exp(glast - gcol)
            state_ref[h] = st * jnp.exp(glast) + _bdot(k_dec, v_new, TN_DIMS)
            zh = z_ref[r0:r0 + c, h * dh:(h + 1) * dh].astype(F32)
            on = o * lax.rsqrt(jnp.mean(o * o, axis=-1, keepdims=True) + EPS) * og
            o_ref[r0:r0 + c, h * dh:(h + 1) * dh] = (on * _silu(zh)).astype(o_ref.dtype)


def _gdn_core(proj, ab, conv_w, a_log, dt_bias, out_g, heads, tb):
    bsz, seq, n = proj.shape
    dh = GDN_HEAD_DIM
    inner = heads * dh
    width = conv_w.shape[0]
    return pl.pallas_call(
        functools.partial(_gdn_kernel, tb=tb, heads=heads, dh=dh, width=width),
        grid=(bsz, seq // tb),
        in_specs=[
            pl.BlockSpec((None, tb, 3 * inner), lambda b, i: (b, i, 0)),
            pl.BlockSpec((None, tb, inner), lambda b, i: (b, i, 3)),
            pl.BlockSpec((None, tb, 2 * heads), lambda b, i: (b, i, 0)),
            pl.BlockSpec((width, 3 * inner), lambda b, i: (0, 0)),
            pl.BlockSpec((1, heads), lambda b, i: (0, 0)),
            pl.BlockSpec((1, heads), lambda b, i: (0, 0)),
            pl.BlockSpec((1, dh), lambda b, i: (0, 0)),
        ],
        out_specs=pl.BlockSpec((None, tb, inner), lambda b, i: (b, i, 0)),
        out_shape=jax.ShapeDtypeStruct((bsz, seq, inner), BF16),
        scratch_shapes=[
            pltpu.VMEM((8 + tb, 3 * inner), F32),
            pltpu.VMEM((heads, dh, dh), F32),
        ],
        compiler_params=_params("parallel", "arbitrary"),
        name="gdn_core",
    )(proj, proj, ab, conv_w, a_log.reshape(1, heads), dt_bias.reshape(1, heads), out_g.reshape(1, dh))


def _out_proj_kernel(o_ref, x_ref, mod_ref, w_ref, y_ref, *, d, gate_off):
    y = jnp.dot(o_ref[...], w_ref[...], preferred_element_type=F32)
    y_ref[...] = x_ref[...] + mod_ref[:, gate_off:gate_off + d] * y


def _out_proj_residual(o, x, mod, w, gate_off, tm):
    bsz, seq, d = x.shape
    k = o.shape[-1]
    return pl.pallas_call(
        functools.partial(_out_proj_kernel, d=d, gate_off=gate_off),
        grid=(bsz, seq // tm),
        in_specs=[
            pl.BlockSpec((None, tm, k), lambda b, i: (b, i, 0)),
            pl.BlockSpec((None, tm, d), lambda b, i: (b, i, 0)),
            pl.BlockSpec((None, 1, mod.shape[-1]), lambda b, i: (b, 0, 0)),
            pl.BlockSpec((k, d), lambda b, i: (0, 0)),
        ],
        out_specs=pl.BlockSpec((None, tm, d), lambda b, i: (b, i, 0)),
        out_shape=jax.ShapeDtypeStruct((bsz, seq, d), F32),
        compiler_params=_params("parallel", "parallel"),
        name="out_proj_residual",
    )(o, x, mod, w)


def _head_rms(x, bd, gain, dh):
    ss = jnp.dot((x * x).astype(BF16), bd, preferred_element_type=F32) * (1.0 / dh)
    return x * lax.rsqrt(ss + EPS) * gain


def _block_diag_ones(n, blk):
    r = jnp.arange(n, dtype=I32) // blk
    return (r[:, None] == r[None, :]).astype(BF16)


def _kv_kernel(x_ref, mod_ref, g_ref, wk_ref, wv_ref, wf_ref, fb_ref, kg_ref, bd_ref,
               k_ref, v_ref, fq_ref, fk_ref, carry_ref, *, d, tm, heads, dh):
    i = pl.program_id(1)

    @pl.when(i == 0)
    def _():
        carry_ref[...] = jnp.zeros(carry_ref.shape, F32)

    h = _modulated(x_ref[...], g_ref[...], mod_ref[:, d:2 * d], mod_ref[:, 0:d]).astype(BF16)
    kraw = jnp.dot(h, wk_ref[...], preferred_element_type=F32)
    k_ref[...] = _head_rms(kraw, bd_ref[...], kg_ref[...], dh).astype(k_ref.dtype)
    v_ref[...] = jnp.dot(h, wv_ref[...], preferred_element_type=F32).astype(v_ref.dtype)
    f = jnp.dot(h, wf_ref[...], preferred_element_type=F32) + fb_ref[...]
    log_f = -_softplus(-f)
    tri = jnp.where(_iota2((tm, tm), 0) >= _iota2((tm, tm), 1), 1.0, 0.0).astype(F32)
    fcum = _hdot(tri, log_f) + carry_ref[...]
    carry_ref[...] = fcum[tm - 1:tm, :]
    eye_h = jnp.where(_iota2((heads, heads), 0) == _iota2((heads, heads), 1), 1.0, 0.0).astype(F32)
    fcum_t = _hdot(eye_h, fcum, NT_DIMS)
    for p in range(heads // 2):
        fq_ref[p] = fcum[:, 2 * p:2 * p + 2]
        fk_ref[p] = fcum_t[2 * p:2 * p + 2, :]


def _shared_kv(x, mod, gain, wk, wv, wf, fb, kgain, bd, heads, tm):
    bsz, seq, d = x.shape
    dh = FOX_HEAD_DIM
    inner = heads * dh
    pairs = heads // 2
    return pl.pallas_call(
        functools.partial(_kv_kernel, d=d, tm=tm, heads=heads, dh=dh),
        grid=(bsz, seq // tm),
        in_specs=[
            pl.BlockSpec((None, tm, d), lambda b, i: (b, i, 0)),
            pl.BlockSpec((None, 1, mod.shape[-1]), lambda b, i: (b, 0, 0)),
            pl.BlockSpec((1, d), lambda b, i: (0, 0)),
            pl.BlockSpec((d, inner), lambda b, i: (0, 0)),
            pl.BlockSpec((d, inner), lambda b, i: (0, 0)),
            pl.BlockSpec((d, heads), lambda b, i: (0, 0)),
            pl.BlockSpec((1, heads), lambda b, i: (0, 0)),
            pl.BlockSpec((1, inner), lambda b, i: (0, 0)),
            pl.BlockSpec((inner, inner), lambda b, i: (0, 0)),
        ],
        out_specs=[
            pl.BlockSpec((None, tm, inner), lambda b, i: (b, i, 0)),
            pl.BlockSpec((None, tm, inner), lambda b, i: (b, i, 0)),
            pl.BlockSpec((None, pairs, tm, 2), lambda b, i: (b, 0, i, 0)),
            pl.BlockSpec((None, pairs, 2, tm), lambda b, i: (b, 0, 0, i)),
        ],
        out_shape=[
            jax.ShapeDtypeStruct((bsz, seq, inner), BF16),
            jax.ShapeDtypeStruct((bsz, seq, inner), BF16),
            jax.ShapeDtypeStruct((bsz, pairs, seq, 2), F32),
            jax.ShapeDtypeStruct((bsz, pairs, 2, seq), F32),
        ],
        scratch_shapes=[pltpu.VMEM((1, heads), F32)],
        compiler_params=_params("parallel", "arbitrary"),
        name="shared_kv",
    )(x, mod, gain, wk, wv, wf, fb, kgain, bd)


def _fox_qz_kernel(x_ref, mod_ref, g_ref, wq_ref, wz_ref, qg_ref, bd_ref, q_ref, z_ref, *, d, dh):
    h = _modulated(x_ref[...], g_ref[...], mod_ref[:, d:2 * d], mod_ref[:, 0:d]).astype(BF16)
    qraw = jnp.dot(h, wq_ref[...], preferred_element_type=F32)
    q_ref[...] = (_head_rms(qraw, bd_ref[...], qg_ref[...], dh) * (dh ** -0.5)).astype(q_ref.dtype)
    z_ref[...] = jnp.dot(h, wz_ref[...], preferred_element_type=F32).astype(z_ref.dtype)


def _fox_qz_proj(x, mod, gain, wq, wz, qgain, bd, tm):
    bsz, seq, d = x.shape
    inner = wq.shape[1]
    return pl.pallas_call(
        functools.partial(_fox_qz_kernel, d=d, dh=FOX_HEAD_DIM),
        grid=(bsz, seq // tm),
        in_specs=[
            pl.BlockSpec((None, tm, d), lambda b, i: (b, i, 0)),
            pl.BlockSpec((None, 1, mod.shape[-1]), lambda b, i: (b, 0, 0)),
            pl.BlockSpec((1, d), lambda b, i: (0, 0)),
            pl.BlockSpec((d, inner), lambda b, i: (0, 0)),
            pl.BlockSpec((d, inner), lambda b, i: (0, 0)),
            pl.BlockSpec((1, inner), lambda b, i: (0, 0)),
            pl.BlockSpec((inner, inner), lambda b, i: (0, 0)),
        ],
        out_specs=[
            pl.BlockSpec((None, tm, inner), lambda b, i: (b, i, 0)),
            pl.BlockSpec((None, tm, inner), lambda b, i: (b, i, 0)),
        ],
        out_shape=[
            jax.ShapeDtypeStruct((bsz, seq, inner), BF16),
            jax.ShapeDtypeStruct((bsz, seq, inner), BF16),
        ],
        compiler_params=_params("parallel", "parallel"),
        name="fox_qz_proj",
    )(x, mod, gain, wq, wz, qgain, bd)


def _fox_kernel(q_ref, k_ref, v_ref, fq_ref, fk_ref, z_ref, o_ref, *, tq, dh):
    i = pl.program_id(2)
    lane = _iota2((tq, 2 * dh), 1)
    first = lane < dh
    q = q_ref[...]
    q0 = jnp.where(first, q, jnp.zeros_like(q))
    q1 = jnp.where(first, jnp.zeros_like(q), q)
    fq = fq_ref[...]
    fq0 = fq[:, 0:1]
    fq1 = fq[:, 1:2]

    def scores(kb):
        start = pl.multiple_of(kb * tq, tq)
        kblk = k_ref[pl.ds(start, tq), :]
        fk = fk_ref[:, pl.ds(start, tq)]
        s0 = lax.dot_general(q0, kblk, NT_DIMS, preferred_element_type=F32) + fq0 - fk[0:1, :]
        s1 = lax.dot_general(q1, kblk, NT_DIMS, preferred_element_type=F32) + fq1 - fk[1:2, :]
        return s0, s1, start

    def update(s, m, l, acc, vblk):
        m_new = jnp.maximum(m, jnp.max(s, axis=-1, keepdims=True))
        alpha = jnp.exp(m - m_new)
        p = jnp.exp(s - m_new)
        l_new = alpha * l + jnp.sum(p, axis=-1, keepdims=True)
        acc_new = alpha * acc + jnp.dot(p.astype(BF16), vblk, preferred_element_type=F32)
        return m_new, l_new, acc_new

    def full_block(kb, carry):
        m0, l0, a0, m1, l1, a1 = carry
        s0, s1, start = scores(kb)
        vblk = v_ref[pl.ds(start, tq), :]
        m0, l0, a0 = update(s0, m0, l0, a0, vblk)
        m1, l1, a1 = update(s1, m1, l1, a1, vblk)
        return m0, l0, a0, m1, l1, a1

    neg = jnp.full((tq, 1), -jnp.inf, F32)
    zero1 = jnp.zeros((tq, 1), F32)
    zacc = jnp.zeros((tq, 2 * dh), F32)
    carry = lax.fori_loop(0, i, full_block, (neg, zero1, zacc, neg, zero1, zacc))
    m0, l0, a0, m1, l1, a1 = carry
    s0, s1, start = scores(i)
    causal = _iota2((tq, tq), 0) >= _iota2((tq, tq), 1)
    s0 = jnp.where(causal, s0, -jnp.inf)
    s1 = jnp.where(causal, s1, -jnp.inf)
    vblk = v_ref[pl.ds(start, tq), :]
    m0, l0, a0 = update(s0, m0, l0, a0, vblk)
    m1, l1, a1 = update(s1, m1, l1, a1, vblk)
    o = jnp.where(first, a0 / l0, a1 / l1)
    o_ref[...] = (o * _sigmoid(z_ref[...].astype(F32))).astype(o_ref.dtype)


def _fox_attention(q, k, v, fq, fk, z, tq):
    bsz, seq, inner = q.shape
    dh = FOX_HEAD_DIM
    pairs = inner // (2 * dh)
    return pl.pallas_call(
        functools.partial(_fox_kernel, tq=tq, dh=dh),
        grid=(bsz, pairs, seq // tq),
        in_specs=[
            pl.BlockSpec((None, tq, 2 * dh), lambda b, p, i: (b, i, p)),
            pl.BlockSpec((None, seq, 2 * dh), lambda b, p, i: (b, 0, p)),
            pl.BlockSpec((None, seq, 2 * dh), lambda b, p, i: (b, 0, p)),
            pl.BlockSpec((None, None, tq, 2), lambda b, p, i: (b, p, i, 0)),
            pl.BlockSpec((None, None, 2, seq), lambda b, p, i: (b, p, 0, 0)),
            pl.BlockSpec((None, tq, 2 * dh), lambda b, p, i: (b, i, p)),
        ],
        out_specs=pl.BlockSpec((None, tq, 2 * dh), lambda b, p, i: (b, i, p)),
        out_shape=jax.ShapeDtypeStruct((bsz, seq, inner), BF16),
        compiler_params=_params("parallel", "parallel", "parallel"),
        name="fox_attention",
    )(q, k, v, fq, fk, z)


def _router_kernel(x_ref, mod_ref, g_ref, wr_ref, br_ref, h_ref, meta_ref, cnt_ref, carry_ref,
                   *, d, tm, groups, per_group):
    step = pl.program_id(0) * pl.num_programs(1) + pl.program_id(1)

    @pl.when(step == 0)
    def _():
        carry_ref[...] = jnp.zeros(carry_ref.shape, F32)

    h = _modulated(x_ref[...], g_ref[...], mod_ref[:, 4 * d:5 * d], mod_ref[:, 3 * d:4 * d])
    h_ref[...] = h
    logits = _hdot(h, wr_ref[...]) + br_ref[...]
    lane = _iota2((tm, LANES), 1)
    neg_inf = jnp.float32(-jnp.inf)
    big = jnp.int32(LANES)

    def first_argmax(vals, mask):
        mv = jnp.where(mask, vals, neg_inf)
        mx = jnp.max(mv, axis=-1, keepdims=True)
        idx = jnp.min(jnp.where(mask & (mv == mx), lane, big), axis=-1, keepdims=True)
        return mx, idx

    gmask = lane < groups
    gmax, gidx = first_argmax(logits, gmask)
    gsum = jnp.sum(jnp.where(gmask, jnp.exp(logits - gmax), 0.0), axis=-1, keepdims=True)
    group_gate = 1.0 / gsum
    lo = groups + gidx * per_group
    emask = (lane >= lo) & (lane < lo + per_group)
    e1max, e1lane = first_argmax(logits, emask)
    e2max, e2lane = first_argmax(logits, emask & (lane != e1lane))
    esum = jnp.sum(jnp.where(emask, jnp.exp(logits - e1max), 0.0), axis=-1, keepdims=True)
    p1 = 1.0 / esum
    p2 = jnp.exp(e2max - e1max) / esum
    psum = p1 + p2
    w1 = group_gate * (p1 / psum)
    w2 = group_gate * (p2 / psum)

    oh1 = (lane == e1lane).astype(F32)
    oh2 = (lane == e2lane).astype(F32)
    strict = jnp.where(_iota2((tm, tm), 0) > _iota2((tm, tm), 1), 1.0, 0.0).astype(BF16)
    c1 = jnp.dot(strict, oh1.astype(BF16), preferred_element_type=F32)
    c2 = jnp.dot(strict, oh2.astype(BF16), preferred_element_type=F32)
    tot1 = jnp.sum(oh1, axis=0, keepdims=True)
    tot2 = jnp.sum(oh2, axis=0, keepdims=True)
    carry = carry_ref[...]
    rank1 = jnp.sum(oh1 * (c1 + carry), axis=-1, keepdims=True)
    rank2 = jnp.sum(oh2 * (c2 + carry + tot1), axis=-1, keepdims=True)
    carry = carry + tot1 + tot2
    carry_ref[...] = carry
    cnt_ref[...] = carry

    e1 = (e1lane - groups).astype(F32)
    e2 = (e2lane - groups).astype(F32)
    mlane = _iota2((tm, 8), 1)
    meta = jnp.where(mlane == 0, e1, 0.0)
    meta = jnp.where(mlane == 1, e2, meta)
    meta = jnp.where(mlane == 2, rank1, meta)
    meta = jnp.where(mlane == 3, rank2, meta)
    meta = jnp.where(mlane == 4, w1, meta)
    meta = jnp.where(mlane == 5, w2, meta)
    meta_ref[...] = meta


def _router(x, mod, gain, w_router, b_router, groups, per_group, tm):
    bsz, seq, d = x.shape
    return pl.pallas_call(
        functools.partial(_router_kernel, d=d, tm=tm, groups=groups, per_group=per_group),
        grid=(bsz, seq // tm),
        in_specs=[
            pl.BlockSpec((None, tm, d), lambda b, i: (b, i, 0)),
            pl.BlockSpec((None, 1, mod.shape[-1]), lambda b, i: (b, 0, 0)),
            pl.BlockSpec((1, d), lambda b, i: (0, 0)),
            pl.BlockSpec((d, LANES), lambda b, i: (0, 0)),
            pl.BlockSpec((1, LANES), lambda b, i: (0, 0)),
        ],
        out_specs=[
            pl.BlockSpec((None, tm, d), lambda b, i: (b, i, 0)),
            pl.BlockSpec((None, tm, 8), lambda b, i: (b, i, 0)),
            pl.BlockSpec((1, LANES), lambda b, i: (0, 0)),
        ],
        out_shape=[
            jax.ShapeDtypeStruct((bsz, seq, d), F32),
            jax.ShapeDtypeStruct((bsz, seq, 8), F32),
            jax.ShapeDtypeStruct((1, LANES), F32),
        ],
        scratch_shapes=[pltpu.VMEM((1, LANES), F32)],
        compiler_params=_params("arbitrary", "arbitrary"),
        name="moe_router",
    )(x, mod, gain, w_router, b_router)


def _dispatch_kernel(dest_ref, h_ref, xs_in_ref, xs_ref, sem, *, tm):
    del xs_in_ref

    def row_copy(r, k):
        dst = dest_ref[r * TOP_K + k]
        return pltpu.make_async_copy(h_ref.at[pl.ds(r, 1)], xs_ref.at[pl.ds(dst, 1)], sem)

    def start(r, carry):
        for k in range(TOP_K):
            row_copy(r, k).start()
        return carry

    def wait(r, carry):
        for k in range(TOP_K):
            row_copy(r, k).wait()
        return carry

    lax.fori_loop(0, tm, start, 0)
    lax.fori_loop(0, tm, wait, 0)


def _dispatch(h_flat, dest, xs_init, tm):
    n_tok, d = h_flat.shape
    return pl.pallas_call(
        functools.partial(_dispatch_kernel, tm=tm),
        grid=(n_tok // tm,),
        in_specs=[
            pl.BlockSpec((tm * TOP_K,), lambda i: (i,), memory_space=pltpu.SMEM),
            pl.BlockSpec((tm, d), lambda i: (i, 0)),
            pl.BlockSpec(memory_space=pl.ANY),
        ],
        out_specs=pl.BlockSpec(memory_space=pl.ANY),
        out_shape=jax.ShapeDtypeStruct(xs_init.shape, xs_init.dtype),
        scratch_shapes=[pltpu.SemaphoreType.DMA(())],
        input_output_aliases={2: 0},
        compiler_params=_params("arbitrary"),
        name="moe_dispatch",
    )(dest, h_flat, xs_init)


def _expert_kernel(be_ref, nu_ref, xs_ref, wg_ref, wu_ref, wd_ref, ys_ref, wgb, wub, wdb):
    i = pl.program_id(0)
    prev = be_ref[jnp.maximum(i - 1, 0)]
    fresh = (i == 0) | (be_ref[i] != prev)

    @pl.when(fresh)
    def _():
        wgb[...] = wg_ref[...].astype(BF16)
        wub[...] = wu_ref[...].astype(BF16)
        wdb[...] = wd_ref[...].astype(BF16)

    @pl.when(i < nu_ref[0])
    def _():
        x = xs_ref[...].astype(BF16)
        g = jnp.dot(x, wgb[...], preferred_element_type=F32)
        u = jnp.dot(x, wub[...], preferred_element_type=F32)
        mid = (_silu(g) * u).astype(BF16)
        ys_ref[...] = jnp.dot(mid, wdb[...], preferred_element_type=F32)

    @pl.when(i >= nu_ref[0])
    def _():
        ys_ref[...] = jnp.zeros(ys_ref.shape, ys_ref.dtype)


def _experts(xs, block_expert, n_used, w_gate, w_up, w_down, layer, blk):
    rows, d = xs.shape
    de = w_gate.shape[-1]
    n_blocks = rows // blk
    return pl.pallas_call(
        _expert_kernel,
        grid_spec=pltpu.PrefetchScalarGridSpec(
            num_scalar_prefetch=2,
            grid=(n_blocks,),
            in_specs=[
                pl.BlockSpec((blk, d), lambda i, be, nu: (i, 0)),
                pl.BlockSpec((None, None, d, de), lambda i, be, nu: (layer, be[i], 0, 0)),
                pl.BlockSpec((None, None, d, de), lambda i, be, nu: (layer, be[i], 0, 0)),
                pl.BlockSpec((None, None, de, d), lambda i, be, nu: (layer, be[i], 0, 0)),
            ],
            out_specs=pl.BlockSpec((blk, d), lambda i, be, nu: (i, 0)),
            scratch_shapes=[
                pltpu.VMEM((d, de), BF16),
                pltpu.VMEM((d, de), BF16),
                pltpu.VMEM((de, d), BF16),
            ],
        ),
        out_shape=jax.ShapeDtypeStruct((rows, d), F32),
        compiler_params=_params("arbitrary"),
        name="moe_experts",
    )(block_expert, n_used, xs, w_gate, w_up, w_down)


def _combine_kernel(dest_ref, x_ref, mod_ref, meta_ref, ys_ref, o_ref, ybuf, sem, *, tm, d):
    def row_copy(r, k):
        src = dest_ref[r * TOP_K + k]
        return pltpu.make_async_copy(ys_ref.at[pl.ds(src, 1)], ybuf.at[k, pl.ds(r, 1)], sem)

    def start(r, carry):
        for k in range(TOP_K):
            row_copy(r, k).start()
        return carry

    def wait(r, carry):
        for k in range(TOP_K):
            row_copy(r, k).wait()
        return carry

    lax.fori_loop(0, tm, start, 0)
    lax.fori_loop(0, tm, wait, 0)
    meta = meta_ref[...]
    y = meta[:, 4:5] * ybuf[0] + meta[:, 5:6] * ybuf[1]
    o_ref[...] = x_ref[...] + mod_ref[:, 5 * d:6 * d] * y


def _combine(dest, x, mod, meta, ys, tm):
    bsz, seq, d = x.shape
    nb = seq // tm
    return pl.pallas_call(
        functools.partial(_combine_kernel, tm=tm, d=d),
        grid=(bsz, nb),
        in_specs=[
            pl.BlockSpec((tm * TOP_K,), lambda b, i: (b * nb + i,), memory_space=pltpu.SMEM),
            pl.BlockSpec((None, tm, d), lambda b, i: (b, i, 0)),
            pl.BlockSpec((None, 1, mod.shape[-1]), lambda b, i: (b, 0, 0)),
            pl.BlockSpec((None, tm, 8), lambda b, i: (b, i, 0)),
            pl.BlockSpec(memory_space=pl.ANY),
        ],
        out_specs=pl.BlockSpec((None, tm, d), lambda b, i: (b, i, 0)),
        scratch_shapes=[
            pltpu.VMEM((TOP_K, tm, d), F32),
            pltpu.SemaphoreType.DMA(()),
        ],
        out_shape=jax.ShapeDtypeStruct((bsz, seq, d), F32),
        compiler_params=_params("arbitrary", "arbitrary"),
        name="moe_combine",
    )(dest, x, mod, meta, ys)


def _hier_moe(x, mod, gain, w_group, b_group, w_expert, b_expert, w_gate, w_up, w_down, layer):
    bsz, seq, d = x.shape
    groups = w_group.shape[1]
    n_exp = w_expert.shape[1]
    per_group = n_exp // groups
    n_tok = bsz * seq
    m = n_tok * TOP_K
    blk = 512
    tm = 512

    pad = LANES - groups - n_exp
    w_router = jnp.concatenate([w_group, w_expert, jnp.zeros((d, pad), F32)], axis=1)
    b_router = jnp.concatenate([b_group, b_expert, jnp.zeros((pad,), F32)]).reshape(1, LANES)
    h, meta, cnt = _router(x, mod, gain, w_router, b_router, groups, per_group, tm)

    counts = cnt[0, groups:groups + n_exp].astype(I32)
    padded = (counts + blk - 1) // blk * blk
    pad_end = jnp.cumsum(padded)
    pad_start = pad_end - padded
    n_blocks = -(-m // blk) + n_exp
    block_expert = jnp.minimum(
        jnp.searchsorted(pad_end, jnp.arange(n_blocks, dtype=I32) * blk, side='right'), n_exp - 1).astype(I32)
    n_used = (pad_end[-1] // blk).astype(I32).reshape(1)
    meta_flat = meta.reshape(n_tok, 8)
    e_idx = meta_flat[:, 0:TOP_K].astype(I32)
    rank = meta_flat[:, TOP_K:2 * TOP_K].astype(I32)
    dest = (pad_start[e_idx] + rank).reshape(m)

    xs = _dispatch(h.reshape(n_tok, d), dest, jnp.zeros((n_blocks * blk, d), F32), tm)
    ys = _experts(xs, block_expert, n_used, w_gate, w_up, w_down, layer, blk)
    return _combine(dest, x, mod, meta, ys, tm)


def kernel(x, c, mod_w, mod_b, norm_mix_g, norm_ffn_g, gdn_w_in, gdn_conv_w, gdn_a_log, gdn_dt_bias, gdn_out_norm_g, gdn_w_out, kv_mod_w, kv_mod_b, kv_norm_g, kv_w, kv_forget_b, k_norm_g, fox_w_qz, fox_q_norm_g, fox_w_out, moe_w_group, moe_b_group, moe_w_expert, moe_b_expert, moe_w_gate, moe_w_up, moe_w_down):
    bsz, seq, d = x.shape
    depth = mod_w.shape[0]
    n_a = gdn_w_in.shape[0]
    gdn_heads = gdn_a_log.shape[1]
    gdn_inner = gdn_heads * GDN_HEAD_DIM
    fox_heads = kv_forget_b.shape[0]
    fox_inner = fox_heads * FOX_HEAD_DIM
    tm = 512

    mod_all = _mod_vectors(c, mod_w, mod_b).reshape(depth, bsz, 1, 6 * d)
    kv_mod = _mod_vectors(c, kv_mod_w[None], kv_mod_b[None]).reshape(bsz, 1, 2 * d)
    bd = _block_diag_ones(fox_inner, FOX_HEAD_DIM)

    k_sh = v_sh = fq = fk = None
    for layer in range(depth):
        mod = mod_all[layer]
        gain_mix = norm_mix_g[layer].reshape(1, d)
        if layer < n_a:
            w_in = gdn_w_in[layer]
            proj, ab = _gdn_in_proj(x, mod, gain_mix, w_in[:, :4 * gdn_inner].astype(BF16),
                                    w_in[:, 4 * gdn_inner:].astype(BF16), tm)
            o = _gdn_core(proj, ab, gdn_conv_w[layer], gdn_a_log[layer], gdn_dt_bias[layer],
                          gdn_out_norm_g[layer], gdn_heads, GDN_CHUNK)
            x = _out_proj_residual(o, x, mod, gdn_w_out[layer].astype(BF16), 2 * d, tm)
        else:
            j = layer - n_a
            wqz = fox_w_qz[j]
            qg = jnp.tile(fox_q_norm_g[j], fox_heads).reshape(1, fox_inner)
            q, z = _fox_qz_proj(x, mod, gain_mix, wqz[:, :fox_inner].astype(BF16),
                                wqz[:, fox_inner:].astype(BF16), qg, bd, tm)
            o = _fox_attention(q, k_sh, v_sh, fq, fk, z, 512)
            x = _out_proj_residual(o, x, mod, fox_w_out[j].astype(BF16), 2 * d, tm)
        x = _hier_moe(x, mod, norm_ffn_g[layer].reshape(1, d), moe_w_group[layer], moe_b_group[layer],
                      moe_w_expert[layer], moe_b_expert[layer], moe_w_gate, moe_w_up, moe_w_down, layer)
        if layer == n_a - 1:
            kg = jnp.tile(k_norm_g, fox_heads).reshape(1, fox_inner)
            k_sh, v_sh, fq, fk = _shared_kv(
                x, kv_mod, kv_norm_g.reshape(1, d), kv_w[:, :fox_inner].astype(BF16),
                kv_w[:, fox_inner:2 * fox_inner].astype(BF16), kv_w[:, 2 * fox_inner:].astype(BF16),
                kv_forget_b.reshape(1, fox_heads), kg, bd, fox_heads, tm)
    return x
```

```python
import functools

import jax
import jax.numpy as jnp
from jax import lax
from jax.experimental import pallas as pl
from jax.experimental.pallas import tpu as pltpu

F32 = jnp.float32
BF16 = jnp.bfloat16
I32 = jnp.int32

EPS = 1e-6
GDN_CHUNK = 64
GDN_HEAD_DIM = 128
FOX_HEAD_DIM = 64
TOP_K = 2
LANES = 128
VMEM_LIMIT = 56 * 1024 * 1024
HIGHEST = lax.Precision.HIGHEST

NT_DIMS = (((1,), (1,)), ((), ()))
TN_DIMS = (((0,), (0,)), ((), ()))


def _params(*sem):
    return pltpu.CompilerParams(dimension_semantics=sem, vmem_limit_bytes=VMEM_LIMIT)


def _sigmoid(x):
    return 1.0 / (1.0 + jnp.exp(-x))


def _silu(x):
    return x * _sigmoid(x)


def _softplus(x):
    return jnp.maximum(x, 0.0) + jnp.log(1.0 + jnp.exp(-jnp.abs(x)))


def _modulated(x, gain, scale, shift):
    ms = jnp.mean(x * x, axis=-1, keepdims=True)
    y = x * lax.rsqrt(ms + EPS)
    return (y * gain) * (1.0 + scale) + shift


def _bdot(a, b, dims=None):
    a = a.astype(BF16)
    b = b.astype(BF16)
    if dims is None:
        return jnp.dot(a, b, preferred_element_type=F32)
    return lax.dot_general(a, b, dims, preferred_element_type=F32)


def _hdot(a, b, dims=None):
    if dims is None:
        return jnp.dot(a, b, preferred_element_type=F32, precision=HIGHEST)
    return lax.dot_general(a, b, dims, preferred_element_type=F32, precision=HIGHEST)


def _iota2(shape, dim):
    return lax.broadcasted_iota(I32, shape, dim)


def _mod_kernel(c_ref, w_ref, b_ref, o_ref):
    c = c_ref[...]
    o_ref[...] = _hdot(_silu(c), w_ref[...]) + b_ref[...]


def _mod_vectors(c, w, b):
    n_layers, d, n = w.shape
    bsz = c.shape[0]
    tn = 1536 if n % 1536 == 0 else n
    return pl.pallas_call(
        _mod_kernel,
        grid=(n_layers, n // tn),
        in_specs=[
            pl.BlockSpec((bsz, d), lambda l, j: (0, 0)),
            pl.BlockSpec((None, d, tn), lambda l, j: (l, 0, j)),
            pl.BlockSpec((None, 1, tn), lambda l, j: (l, 0, j)),
        ],
        out_specs=pl.BlockSpec((None, bsz, tn), lambda l, j: (l, 0, j)),
        out_shape=jax.ShapeDtypeStruct((n_layers, bsz, n), F32),
        compiler_params=_params("parallel", "parallel"),
        name="mod_vectors",
    )(c, w, b.reshape(n_layers, 1, n))


def _gdn_in_kernel(x_ref, mod_ref, g_ref, w_ref, wab_ref, cw_ref, qkv_ref, sz_ref, oab_ref, xpad_ref,
                   *, d, tm, heads, dh, width):
    inner = heads * dh
    halo = 8
    i = pl.program_id(1)

    @pl.when(i == 0)
    def _():
        xpad_ref[0:halo, :] = jnp.zeros((halo, 3 * inner), F32)

    h = _modulated(x_ref[...], g_ref[...], mod_ref[:, d:2 * d], mod_ref[:, 0:d]).astype(BF16)
    oab_ref[...] = jnp.dot(h, wab_ref[...], preferred_element_type=F32)
    z = jnp.dot(h, w_ref[:, 3 * inner:4 * inner], preferred_element_type=F32)
    sz_ref[...] = _silu(z).astype(sz_ref.dtype)
    xpad_ref[halo:halo + tm, :] = jnp.dot(h, w_ref[:, 0:3 * inner], preferred_element_type=F32)
    for j in range(3 * heads):
        cols = slice(j * dh, (j + 1) * dh)
        acc = xpad_ref[halo:halo + tm, cols] * cw_ref[width - 1:width, cols]
        for s in range(1, width):
            acc = acc + xpad_ref[halo - s:halo - s + tm, cols] * cw_ref[width - 1 - s:width - s, cols]
        y = _silu(acc)
        if j < 2 * heads:
            y = y * lax.rsqrt(jnp.sum(y * y, axis=-1, keepdims=True) + EPS)
        if j < heads:
            y = y * (dh ** -0.5)
        qkv_ref[:, cols] = y.astype(qkv_ref.dtype)
    xpad_ref[0:halo, :] = xpad_ref[tm:tm + halo, :]


def _gdn_in_proj(x, mod, gain, w_main, w_ab, conv_w, heads, tm):
    bsz, seq, d = x.shape
    dh = GDN_HEAD_DIM
    inner = heads * dh
    nab = w_ab.shape[1]
    width = conv_w.shape[0]
    return pl.pallas_call(
        functools.partial(_gdn_in_kernel, d=d, tm=tm, heads=heads, dh=dh, width=width),
        grid=(bsz, seq // tm),
        in_specs=[
            pl.BlockSpec((None, tm, d), lambda b, i: (b, i, 0)),
            pl.BlockSpec((None, 1, mod.shape[-1]), lambda b, i: (b, 0, 0)),
            pl.BlockSpec((1, d), lambda b, i: (0, 0)),
            pl.BlockSpec((d, 4 * inner), lambda b, i: (0, 0)),
            pl.BlockSpec((d, nab), lambda b, i: (0, 0)),
            pl.BlockSpec((width, 3 * inner), lambda b, i: (0, 0)),
        ],
        out_specs=[
            pl.BlockSpec((None, tm, 3 * inner), lambda b, i: (b, i, 0)),
            pl.BlockSpec((None, tm, inner), lambda b, i: (b, i, 0)),
            pl.BlockSpec((None, tm, nab), lambda b, i: (b, i, 0)),
        ],
        out_shape=[
            jax.ShapeDtypeStruct((bsz, seq, 3 * inner), BF16),
            jax.ShapeDtypeStruct((bsz, seq, inner), BF16),
            jax.ShapeDtypeStruct((bsz, seq, nab), F32),
        ],
        scratch_shapes=[pltpu.VMEM((8 + tm, 3 * inner), F32)],
        compiler_params=_params("parallel", "arbitrary"),
        name="gdn_in_proj",
    )(x, mod, gain, w_main, w_ab, conv_w)


def _gdn_kernel(q_ref, k_ref, v_ref, sz_ref, ab_ref, alog_ref, dtb_ref, og_ref, o_ref, state_ref,
                *, tb, heads, dh):
    c = GDN_CHUNK
    pairs = heads // 2
    i = pl.program_id(1)

    @pl.when(i == 0)
    def _():
        state_ref[...] = jnp.zeros(state_ref.shape, F32)

    ab = ab_ref[...]
    beta_all = _sigmoid(ab[:, heads:2 * heads])
    g_all = -jnp.exp(alog_ref[...]) * _softplus(ab[:, 0:heads] + dtb_ref[...])
    og = og_ref[...]

    row = _iota2((c, 2 * c), 0)
    lane = _iota2((c, 2 * c), 1)
    first = lane < c
    col = jnp.where(first, lane, lane - c)
    incl = row >= col
    strict = row > col
    eye_p = jnp.where(row == col, 1.0, 0.0).astype(F32)
    tri = jnp.where(_iota2((c, c), 0) >= _iota2((c, c), 1), 1.0, 0.0).astype(F32)
    bd_small = (_iota2((2 * c, 2 * c), 0) < c) == (_iota2((2 * c, 2 * c), 1) < c)
    bd_wide = (_iota2((2 * c, 2 * dh), 0) < c) == (_iota2((2 * c, 2 * dh), 1) < dh)
    first_h = _iota2((pairs, 2 * c), 1) < c
    sel0 = jnp.where(_iota2((pairs, heads), 1) == 2 * _iota2((pairs, heads), 0), 1.0, 0.0).astype(F32)
    sel1 = jnp.where(_iota2((pairs, heads), 1) == 2 * _iota2((pairs, heads), 0) + 1, 1.0, 0.0).astype(F32)

    def block_diag(x, mask):
        return jnp.where(mask, jnp.concatenate([x, x], axis=0), 0.0).astype(BF16)

    def pair_cols(x, p):
        return jnp.concatenate([jnp.broadcast_to(x[:, 2 * p:2 * p + 1], (c, dh)),
                                jnp.broadcast_to(x[:, 2 * p + 1:2 * p + 2], (c, dh))], axis=1)

    units = [(ci, p) for ci in range(tb // c) for p in range(pairs)]
    gcums, glasts, pk, tt, attn, w_u, qg, kdec = {}, {}, {}, {}, {}, {}, {}, {}

    for ci in range(tb // c):
        r0 = ci * c
        gcum = _hdot(tri, g_all[r0:r0 + c, :])
        gc2 = jnp.concatenate([gcum, gcum], axis=0)
        gt = jnp.where(first_h, _hdot(sel0, gc2, NT_DIMS), _hdot(sel1, gc2, NT_DIMS))
        glast = gcum[c - 1:c, :]
        gcums[ci] = gcum
        glasts[ci] = glast
        for p in range(pairs):
            u = (ci, p)
            cols = slice(2 * p * dh, (2 * p + 2) * dh)
            kp = k_ref[r0:r0 + c, cols].astype(F32)
            qp = q_ref[r0:r0 + c, cols].astype(F32)
            vp = v_ref[r0:r0 + c, cols].astype(F32)
            gcol = jnp.where(first, gcum[:, 2 * p:2 * p + 1], gcum[:, 2 * p + 1:2 * p + 2])
            decay = jnp.exp(jnp.where(incl, gcol - gt[p:p + 1, :], -jnp.inf))
            beta2 = pair_cols(beta_all[r0:r0 + c, :], p)
            gcum2 = pair_cols(gcum, p)
            egc2 = jnp.exp(gcum2)
            kb = kp * beta2
            y = block_diag(kp, bd_wide)
            kq = _bdot(jnp.concatenate([kb, qp], axis=0), y, NT_DIMS)
            a_mat = jnp.where(strict, kq[0:c, :] * decay, 0.0)
            attn[u] = kq[c:2 * c, :] * decay
            pk[u] = -a_mat
            tt[u] = eye_p - a_mat
            kbg = kb * egc2
            vb = vp * beta2
            w_u[u] = jnp.concatenate(
                [jnp.concatenate([kbg[:, 0:dh], vb[:, 0:dh]], axis=1),
                 jnp.concatenate([kbg[:, dh:2 * dh], vb[:, dh:2 * dh]], axis=1)], axis=0).astype(BF16)
            qg[u] = qp * egc2
            kdec[u] = kp * jnp.exp(pair_cols(jnp.broadcast_to(glast, (c, heads)), p) - gcum2)

    span = 2
    while span <= c:
        last = span == c
        for u in units:
            bd = block_diag(pk[u], bd_small)
            if span == 2:
                pk[u] = _bdot(pk[u], bd)
            elif last:
                tt[u] = tt[u] + _bdot(tt[u], bd)
            else:
                both = _bdot(jnp.concatenate([pk[u], tt[u]], axis=0), bd)
                pk[u] = both[0:c, :]
                tt[u] = tt[u] + both[c:2 * c, :]
        span *= 2
    for u in units:
        t = tt[u]
        lhs = jnp.concatenate([jnp.where(first, t, 0.0), jnp.where(first, 0.0, t)], axis=0)
        w_u[u] = _bdot(lhs, w_u[u])

    for ci in range(tb // c):
        r0 = ci * c
        for p in range(pairs):
            u = (ci, p)
            wu = w_u[u]
            st, wq, vn = [], [], []
            for s in range(2):
                h = 2 * p + s
                st.append(state_ref[h])
                wq.append(_bdot(jnp.concatenate([wu[s * c:(s + 1) * c, 0:dh], qg[u][:, s * dh:(s + 1) * dh]], axis=0),
                                st[s]))
                vn.append(wu[s * c:(s + 1) * c, dh:2 * dh] - wq[s][0:c, :])
            vn2 = jnp.concatenate(vn, axis=0).astype(BF16)
            for s in range(2):
                h = 2 * p + s
                am = jnp.where(first, attn[u], 0.0) if s == 0 else jnp.where(first, 0.0, attn[u])
                o = wq[s][c:2 * c, :] + _bdot(am, vn2)
                gl = glasts[ci][:, h:h + 1]
                state_ref[h] = st[s] * jnp.exp(gl) + _bdot(kdec[u][:, s * dh:(s + 1) * dh], vn[s], TN_DIMS)
                on = o * lax.rsqrt(jnp.mean(o * o, axis=-1, keepdims=True) + EPS) * og
                szh = sz_ref[r0:r0 + c, h * dh:(h + 1) * dh].astype(F32)
                o_ref[r0:r0 + c, h * dh:(h + 1) * dh] = (on * szh).astype(o_ref.dtype)


def _gdn_core(qkv, sz, ab, a_log, dt_bias, out_g, heads, tb):
    bsz, seq, _ = qkv.shape
    dh = GDN_HEAD_DIM
    inner = heads * dh
    return pl.pallas_call(
        functools.partial(_gdn_kernel, tb=tb, heads=heads, dh=dh),
        grid=(bsz, seq // tb),
        in_specs=[
            pl.BlockSpec((None, tb, inner), lambda b, i: (b, i, 0)),
            pl.BlockSpec((None, tb, inner), lambda b, i: (b, i, 1)),
            pl.BlockSpec((None, tb, inner), lambda b, i: (b, i, 2)),
            pl.BlockSpec((None, tb, inner), lambda b, i: (b, i, 0)),
            pl.BlockSpec((None, tb, 2 * heads), lambda b, i: (b, i, 0)),
            pl.BlockSpec((1, heads), lambda b, i: (0, 0)),
            pl.BlockSpec((1, heads), lambda b, i: (0, 0)),
            pl.BlockSpec((1, dh), lambda b, i: (0, 0)),
        ],
        out_specs=pl.BlockSpec((None, tb, inner), lambda b, i: (b, i, 0)),
        out_shape=jax.ShapeDtypeStruct((bsz, seq, inner), BF16),
        scratch_shapes=[pltpu.VMEM((heads, dh, dh), F32)],
        compiler_params=_params("parallel", "arbitrary"),
        name="gdn_core",
    )(qkv, qkv, qkv, sz, ab, a_log.reshape(1, heads), dt_bias.reshape(1, heads), out_g.reshape(1, dh))


def _out_proj_kernel(o_ref, x_ref, mod_ref, w_ref, y_ref, *, d, gate_off):
    y = jnp.dot(o_ref[...], w_ref[...], preferred_element_type=F32)
    y_ref[...] = x_ref[...] + mod_ref[:, gate_off:gate_off + d] * y


def _out_proj_residual(o, x, mod, w, gate_off, tm):
    bsz, seq, d = x.shape
    k = o.shape[-1]
    return pl.pallas_call(
        functools.partial(_out_proj_kernel, d=d, gate_off=gate_off),
        grid=(bsz, seq // tm),
        in_specs=[
            pl.BlockSpec((None, tm, k), lambda b, i: (b, i, 0)),
            pl.BlockSpec((None, tm, d), lambda b, i: (b, i, 0)),
            pl.BlockSpec((None, 1, mod.shape[-1]), lambda b, i: (b, 0, 0)),
            pl.BlockSpec((k, d), lambda b, i: (0, 0)),
        ],
        out_specs=pl.BlockSpec((None, tm, d), lambda b, i: (b, i, 0)),
        out_shape=jax.ShapeDtypeStruct((bsz, seq, d), F32),
        compiler_params=_params("parallel", "parallel"),
        name="out_proj_residual",
    )(o, x, mod, w)


def _head_rms(x, bd, gain, dh):
    ss = jnp.dot((x * x).astype(BF16), bd, preferred_element_type=F32) * (1.0 / dh)
    return x * lax.rsqrt(ss + EPS) * gain


def _block_diag_ones(n, blk):
    r = jnp.arange(n, dtype=I32) // blk
    return (r[:, None] == r[None, :]).astype(BF16)


def _kv_kernel(x_ref, mod_ref, g_ref, wk_ref, wv_ref, wf_ref, fb_ref, kg_ref, bd_ref,
               k_ref, v_ref, fq_ref, fk_ref, carry_ref, *, d, tm, heads, dh):
    i = pl.program_id(1)

    @pl.when(i == 0)
    def _():
        carry_ref[...] = jnp.zeros(carry_ref.shape, F32)

    h = _modulated(x_ref[...], g_ref[...], mod_ref[:, d:2 * d], mod_ref[:, 0:d]).astype(BF16)
    kraw = jnp.dot(h, wk_ref[...], preferred_element_type=F32)
    k_ref[...] = _head_rms(kraw, bd_ref[...], kg_ref[...], dh).astype(k_ref.dtype)
    v_ref[...] = jnp.dot(h, wv_ref[...], preferred_element_type=F32).astype(v_ref.dtype)
    f = jnp.dot(h, wf_ref[...], preferred_element_type=F32) + fb_ref[...]
    log_f = -_softplus(-f)
    tri = jnp.where(_iota2((tm, tm), 0) >= _iota2((tm, tm), 1), 1.0, 0.0).astype(F32)
    fcum = _hdot(tri, log_f) + carry_ref[...]
    carry_ref[...] = fcum[tm - 1:tm, :]
    eye_h = jnp.where(_iota2((heads, heads), 0) == _iota2((heads, heads), 1), 1.0, 0.0).astype(F32)
    fcum_t = _hdot(eye_h, fcum, NT_DIMS)
    for p in range(heads // 2):
        fq_ref[p] = fcum[:, 2 * p:2 * p + 2]
        fk_ref[p] = fcum_t[2 * p:2 * p + 2, :]


def _shared_kv(x, mod, gain, wk, wv, wf, fb, kgain, bd, heads, tm):
    bsz, seq, d = x.shape
    dh = FOX_HEAD_DIM
    inner = heads * dh
    pairs = heads // 2
    return pl.pallas_call(
        functools.partial(_kv_kernel, d=d, tm=tm, heads=heads, dh=dh),
        grid=(bsz, seq // tm),
        in_specs=[
            pl.BlockSpec((None, tm, d), lambda b, i: (b, i, 0)),
            pl.BlockSpec((None, 1, mod.shape[-1]), lambda b, i: (b, 0, 0)),
            pl.BlockSpec((1, d), lambda b, i: (0, 0)),
            pl.BlockSpec((d, inner), lambda b, i: (0, 0)),
            pl.BlockSpec((d, inner), lambda b, i: (0, 0)),
            pl.BlockSpec((d, heads), lambda b, i: (0, 0)),
            pl.BlockSpec((1, heads), lambda b, i: (0, 0)),
            pl.BlockSpec((1, inner), lambda b, i: (0, 0)),
            pl.BlockSpec((inner, inner), lambda b, i: (0, 0)),
        ],
        out_specs=[
            pl.BlockSpec((None, tm, inner), lambda b, i: (b, i, 0)),
            pl.BlockSpec((None, tm, inner), lambda b, i: (b, i, 0)),
            pl.BlockSpec((None, pairs, tm, 2), lambda b, i: (b, 0, i, 0)),
            pl.BlockSpec((None, pairs, 2, tm), lambda b, i: (b, 0, 0, i)),
        ],
        out_shape=[
            jax.ShapeDtypeStruct((bsz, seq, inner), BF16),
            jax.ShapeDtypeStruct((bsz, seq, inner), BF16),
            jax.ShapeDtypeStruct((bsz, pairs, seq, 2), F32),
            jax.ShapeDtypeStruct((bsz, pairs, 2, seq), F32),
        ],
        scratch_shapes=[pltpu.VMEM((1, heads), F32)],
        compiler_params=_params("parallel", "arbitrary"),
        name="shared_kv",
    )(x, mod, gain, wk, wv, wf, fb, kgain, bd)


def _fox_qz_kernel(x_ref, mod_ref, g_ref, wq_ref, wz_ref, qg_ref, bd_ref, q_ref, z_ref, *, d, dh):
    h = _modulated(x_ref[...], g_ref[...], mod_ref[:, d:2 * d], mod_ref[:, 0:d]).astype(BF16)
    qraw = jnp.dot(h, wq_ref[...], preferred_element_type=F32)
    q_ref[...] = (_head_rms(qraw, bd_ref[...], qg_ref[...], dh) * (dh ** -0.5)).astype(q_ref.dtype)
    z_ref[...] = jnp.dot(h, wz_ref[...], preferred_element_type=F32).astype(z_ref.dtype)


def _fox_qz_proj(x, mod, gain, wq, wz, qgain, bd, tm):
    bsz, seq, d = x.shape
    inner = wq.shape[1]
    return pl.pallas_call(
        functools.partial(_fox_qz_kernel, d=d, dh=FOX_HEAD_DIM),
        grid=(bsz, seq // tm),
        in_specs=[
            pl.BlockSpec((None, tm, d), lambda b, i: (b, i, 0)),
            pl.BlockSpec((None, 1, mod.shape[-1]), lambda b, i: (b, 0, 0)),
            pl.BlockSpec((1, d), lambda b, i: (0, 0)),
            pl.BlockSpec((d, inner), lambda b, i: (0, 0)),
            pl.BlockSpec((d, inner), lambda b, i: (0, 0)),
            pl.BlockSpec((1, inner), lambda b, i: (0, 0)),
            pl.BlockSpec((inner, inner), lambda b, i: (0, 0)),
        ],
        out_specs=[
            pl.BlockSpec((None, tm, inner), lambda b, i: (b, i, 0)),
            pl.BlockSpec((None, tm, inner), lambda b, i: (b, i, 0)),
        ],
        out_shape=[
            jax.ShapeDtypeStruct((bsz, seq, inner), BF16),
            jax.ShapeDtypeStruct((bsz, seq, inner), BF16),
        ],
        compiler_params=_params("parallel", "parallel"),
        name="fox_qz_proj",
    )(x, mod, gain, wq, wz, qgain, bd)


def _fox_kernel(q_ref, k_ref, v_ref, fq_ref, fk_ref, z_ref, o_ref, s_ref, p_ref, acc_ref, *, tq, dh, rb):
    i = pl.program_id(2)
    first = _iota2((tq, 2 * dh), 1) < dh
    q = q_ref[...]
    zero = jnp.zeros_like(q)
    qs = (jnp.where(first, q, zero), jnp.where(first, zero, q))
    fq = fq_ref[...]
    acc_ref[...] = jnp.zeros(acc_ref.shape, F32)
    rows = _iota2((rb, tq), 0)
    cols = _iota2((rb, tq), 1)

    def block(kb, ms, masked):
        start = pl.multiple_of(kb * tq, tq)
        kblk = k_ref[pl.ds(start, tq), :]
        vblk = v_ref[pl.ds(start, tq), :]
        fk = fk_ref[:, pl.ds(start, tq)]
        one = jnp.ones_like(vblk)
        vaug = (jnp.where(first, vblk, one), jnp.where(first, one, vblk))
        out = []
        for h in range(2):
            s_ref[h] = (lax.dot_general(qs[h], kblk, NT_DIMS, preferred_element_type=F32)
                        + fq[:, h:h + 1] - fk[h:h + 1, :])
            alphas, m_news = [], []
            for r in range(tq // rb):
                s = s_ref[h, r * rb:(r + 1) * rb, :]
                if masked:
                    s = jnp.where(rows + r * rb >= cols, s, -jnp.inf)
                m_old = ms[h][r * rb:(r + 1) * rb, :]
                tile_max = s[:, 0:LANES]
                for t in range(1, tq // LANES):
                    tile_max = jnp.maximum(tile_max, s[:, t * LANES:(t + 1) * LANES])
                m_new = jnp.maximum(m_old, jnp.max(tile_max, axis=-1, keepdims=True))
                p_ref[h, r * rb:(r + 1) * rb, :] = jnp.exp((s - m_new).astype(BF16))
                alphas.append(jnp.exp(m_old - m_new))
                m_news.append(m_new)
            alpha = jnp.concatenate(alphas, axis=0)
            acc_ref[h] = alpha * acc_ref[h] + jnp.dot(p_ref[h], vaug[h], preferred_element_type=F32)
            out.append(jnp.concatenate(m_news, axis=0))
        return tuple(out)

    neg = jnp.full((tq, 1), -jnp.inf, F32)
    ms = lax.fori_loop(0, i, lambda kb, ms: block(kb, ms, False), (neg, neg))
    block(i, ms, True)
    a0 = acc_ref[0]
    a1 = acc_ref[1]
    num = jnp.where(first, a0, a1)
    den = pltpu.roll(jnp.where(first, a1, a0), dh, 1)
    o_ref[...] = (num / den * _sigmoid(z_ref[...].astype(F32))).astype(o_ref.dtype)


def _fox_attention(q, k, v, fq, fk, z, tq):
    bsz, seq, inner = q.shape
    dh = FOX_HEAD_DIM
    pairs = inner // (2 * dh)
    return pl.pallas_call(
        functools.partial(_fox_kernel, tq=tq, dh=dh, rb=64),
        grid=(bsz, pairs, seq // tq),
        scratch_shapes=[
            pltpu.VMEM((2, tq, tq), F32),
            pltpu.VMEM((2, tq, tq), BF16),
            pltpu.VMEM((2, tq, 2 * dh), F32),
        ],
        in_specs=[
            pl.BlockSpec((None, tq, 2 * dh), lambda b, p, i: (b, i, p)),
            pl.BlockSpec((None, seq, 2 * dh), lambda b, p, i: (b, 0, p)),
            pl.BlockSpec((None, seq, 2 * dh), lambda b, p, i: (b, 0, p)),
            pl.BlockSpec((None, None, tq, 2), lambda b, p, i: (b, p, i, 0)),
            pl.BlockSpec((None, None, 2, seq), lambda b, p, i: (b, p, 0, 0)),
            pl.BlockSpec((None, tq, 2 * dh), lambda b, p, i: (b, i, p)),
        ],
        out_specs=pl.BlockSpec((None, tq, 2 * dh), lambda b, p, i: (b, i, p)),
        out_shape=jax.ShapeDtypeStruct((bsz, seq, inner), BF16),
        compiler_params=_params("parallel", "parallel", "parallel"),
        name="fox_attention",
    )(q, k, v, fq, fk, z)


def _router_kernel(x_ref, mod_ref, g_ref, wr_ref, br_ref, h_ref, meta_ref, cnt_ref, carry_ref,
                   *, d, tm, groups, per_group):
    step = pl.program_id(0) * pl.num_programs(1) + pl.program_id(1)

    @pl.when(step == 0)
    def _():
        carry_ref[...] = jnp.zeros(carry_ref.shape, F32)

    h = _modulated(x_ref[...], g_ref[...], mod_ref[:, 4 * d:5 * d], mod_ref[:, 3 * d:4 * d])
    h_ref[...] = h
    logits = _hdot(h, wr_ref[...]) + br_ref[...]
    lane = _iota2((tm, LANES), 1)
    neg_inf = jnp.float32(-jnp.inf)
    big = jnp.int32(LANES)

    def first_argmax(vals, mask):
        mv = jnp.where(mask, vals, neg_inf)
        mx = jnp.max(mv, axis=-1, keepdims=True)
        idx = jnp.min(jnp.where(mask & (mv == mx), lane, big), axis=-1, keepdims=True)
        return mx, idx

    gmask = lane < groups
    gmax, gidx = first_argmax(logits, gmask)
    gsum = jnp.sum(jnp.where(gmask, jnp.exp(logits - gmax), 0.0), axis=-1, keepdims=True)
    group_gate = 1.0 / gsum
    lo = groups + gidx * per_group
    emask = (lane >= lo) & (lane < lo + per_group)
    e1max, e1lane = first_argmax(logits, emask)
    e2max, e2lane = first_argmax(logits, emask & (lane != e1lane))
    esum = jnp.sum(jnp.where(emask, jnp.exp(logits - e1max), 0.0), axis=-1, keepdims=True)
    p1 = 1.0 / esum
    p2 = jnp.exp(e2max - e1max) / esum
    psum = p1 + p2
    w1 = group_gate * (p1 / psum)
    w2 = group_gate * (p2 / psum)

    oh1 = (lane == e1lane).astype(F32)
    oh2 = (lane == e2lane).astype(F32)
    strict = jnp.where(_iota2((tm, tm), 0) > _iota2((tm, tm), 1), 1.0, 0.0).astype(BF16)
    c1 = jnp.dot(strict, oh1.astype(BF16), preferred_element_type=F32)
    c2 = jnp.dot(strict, oh2.astype(BF16), preferred_element_type=F32)
    tot1 = jnp.sum(oh1, axis=0, keepdims=True)
    tot2 = jnp.sum(oh2, axis=0, keepdims=True)
    carry = carry_ref[...]
    rank1 = jnp.sum(oh1 * (c1 + carry), axis=-1, keepdims=True)
    rank2 = jnp.sum(oh2 * (c2 + carry + tot1), axis=-1, keepdims=True)
    carry = carry + tot1 + tot2
    carry_ref[...] = carry
    cnt_ref[...] = carry

    e1 = (e1lane - groups).astype(F32)
    e2 = (e2lane - groups).astype(F32)
    mlane = _iota2((tm, 8), 1)
    meta = jnp.where(mlane == 0, e1, 0.0)
    meta = jnp.where(mlane == 1, e2, meta)
    meta = jnp.where(mlane == 2, rank1, meta)
    meta = jnp.where(mlane == 3, rank2, meta)
    meta = jnp.where(mlane == 4, w1, meta)
    meta = jnp.where(mlane == 5, w2, meta)
    meta_ref[...] = meta


def _router(x, mod, gain, w_router, b_router, groups, per_group, tm):
    bsz, seq, d = x.shape
    return pl.pallas_call(
        functools.partial(_router_kernel, d=d, tm=tm, groups=groups, per_group=per_group),
        grid=(bsz, seq // tm),
        in_specs=[
            pl.BlockSpec((None, tm, d), lambda b, i: (b, i, 0)),
            pl.BlockSpec((None, 1, mod.shape[-1]), lambda b, i: (b, 0, 0)),
            pl.BlockSpec((1, d), lambda b, i: (0, 0)),
            pl.BlockSpec((d, LANES), lambda b, i: (0, 0)),
            pl.BlockSpec((1, LANES), lambda b, i: (0, 0)),
        ],
        out_specs=[
            pl.BlockSpec((None, tm, d), lambda b, i: (b, i, 0)),
            pl.BlockSpec((None, tm, 8), lambda b, i: (b, i, 0)),
            pl.BlockSpec((1, LANES), lambda b, i: (0, 0)),
        ],
        out_shape=[
            jax.ShapeDtypeStruct((bsz, seq, d), F32),
            jax.ShapeDtypeStruct((bsz, seq, 8), F32),
            jax.ShapeDtypeStruct((1, LANES), F32),
        ],
        scratch_shapes=[pltpu.VMEM((1, LANES), F32)],
        compiler_params=_params("arbitrary", "arbitrary"),
        name="moe_router",
    )(x, mod, gain, w_router, b_router)


def _dispatch_kernel(dest_ref, h_ref, xs_in_ref, xs_ref, sem, *, tm):
    del xs_in_ref

    def row_copy(r, k):
        dst = dest_ref[r * TOP_K + k]
        return pltpu.make_async_copy(h_ref.at[pl.ds(r, 1)], xs_ref.at[pl.ds(dst, 1)], sem)

    def start(r, carry):
        for k in range(TOP_K):
            row_copy(r, k).start()
        return carry

    def wait(r, carry):
        for k in range(TOP_K):
            row_copy(r, k).wait()
        return carry

    lax.fori_loop(0, tm, start, 0)
    lax.fori_loop(0, tm, wait, 0)


def _dispatch(h_flat, dest, xs_init, tm):
    n_tok, d = h_flat.shape
    return pl.pallas_call(
        functools.partial(_dispatch_kernel, tm=tm),
        grid=(n_tok // tm,),
        in_specs=[
            pl.BlockSpec((tm * TOP_K,), lambda i: (i,), memory_space=pltpu.SMEM),
            pl.BlockSpec((tm, d), lambda i: (i, 0)),
            pl.BlockSpec(memory_space=pl.ANY),
        ],
        out_specs=pl.BlockSpec(memory_space=pl.ANY),
        out_shape=jax.ShapeDtypeStruct(xs_init.shape, xs_init.dtype),
        scratch_shapes=[pltpu.SemaphoreType.DMA(())],
        input_output_aliases={2: 0},
        compiler_params=_params("arbitrary"),
        name="moe_dispatch",
    )(dest, h_flat, xs_init)


def _expert_kernel(be_ref, nu_ref, xs_ref, wg_ref, wu_ref, wd_ref, ys_ref, wgb, wub, wdb):
    i = pl.program_id(0)
    prev = be_ref[jnp.maximum(i - 1, 0)]
    fresh = (i == 0) | (be_ref[i] != prev)

    @pl.when(fresh)
    def _():
        wgb[...] = wg_ref[...].astype(BF16)
        wub[...] = wu_ref[...].astype(BF16)
        wdb[...] = wd_ref[...].astype(BF16)

    @pl.when(i < nu_ref[0])
    def _():
        x = xs_ref[...].astype(BF16)
        g = jnp.dot(x, wgb[...], preferred_element_type=F32)
        u = jnp.dot(x, wub[...], preferred_element_type=F32)
        mid = (_silu(g) * u).astype(BF16)
        ys_ref[...] = jnp.dot(mid, wdb[...], preferred_element_type=F32)

    @pl.when(i >= nu_ref[0])
    def _():
        ys_ref[...] = jnp.zeros(ys_ref.shape, ys_ref.dtype)


def _experts(xs, block_expert, n_used, w_gate, w_up, w_down, layer, blk):
    rows, d = xs.shape
    de = w_gate.shape[-1]
    n_blocks = rows // blk
    return pl.pallas_call(
        _expert_kernel,
        grid_spec=pltpu.PrefetchScalarGridSpec(
            num_scalar_prefetch=2,
            grid=(n_blocks,),
            in_specs=[
                pl.BlockSpec((blk, d), lambda i, be, nu: (i, 0)),
                pl.BlockSpec((None, None, d, de), lambda i, be, nu: (layer, be[i], 0, 0)),
                pl.BlockSpec((None, None, d, de), lambda i, be, nu: (layer, be[i], 0, 0)),
                pl.BlockSpec((None, None, de, d), lambda i, be, nu: (layer, be[i], 0, 0)),
            ],
            out_specs=pl.BlockSpec((blk, d), lambda i, be, nu: (i, 0)),
            scratch_shapes=[
                pltpu.VMEM((d, de), BF16),
                pltpu.VMEM((d, de), BF16),
                pltpu.VMEM((de, d), BF16),
            ],
        ),
        out_shape=jax.ShapeDtypeStruct((rows, d), F32),
        compiler_params=_params("arbitrary"),
        name="moe_experts",
    )(block_expert, n_used, xs, w_gate, w_up, w_down)


def _combine_kernel(dest_ref, x_ref, mod_ref, meta_ref, ys_ref, o_ref, ybuf, sem, *, tm, d):
    def row_copy(r, k):
        src = dest_ref[r * TOP_K + k]
        return pltpu.make_async_copy(ys_ref.at[pl.ds(src, 1)], ybuf.at[k, pl.ds(r, 1)], sem)

    def start(r, carry):
        for k in range(TOP_K):
            row_copy(r, k).start()
        return carry

    def wait(r, carry):
        for k in range(TOP_K):
            row_copy(r, k).wait()
        return carry

    lax.fori_loop(0, tm, start, 0)
    lax.fori_loop(0, tm, wait, 0)
    meta = meta_ref[...]
    y = meta[:, 4:5] * ybuf[0] + meta[:, 5:6] * ybuf[1]
    o_ref[...] = x_ref[...] + mod_ref[:, 5 * d:6 * d] * y


def _combine(dest, x, mod, meta, ys, tm):
    bsz, seq, d = x.shape
    nb = seq // tm
    return pl.pallas_call(
        functools.partial(_combine_kernel, tm=tm, d=d),
        grid=(bsz, nb),
        in_specs=[
            pl.BlockSpec((tm * TOP_K,), lambda b, i: (b * nb + i,), memory_space=pltpu.SMEM),
            pl.BlockSpec((None, tm, d), lambda b, i: (b, i, 0)),
            pl.BlockSpec((None, 1, mod.shape[-1]), lambda b, i: (b, 0, 0)),
            pl.BlockSpec((None, tm, 8), lambda b, i: (b, i, 0)),
            pl.BlockSpec(memory_space=pl.ANY),
        ],
        out_specs=pl.BlockSpec((None, tm, d), lambda b, i: (b, i, 0)),
        scratch_shapes=[
            pltpu.VMEM((TOP_K, tm, d), F32),
            pltpu.SemaphoreType.DMA(()),
        ],
        out_shape=jax.ShapeDtypeStruct((bsz, seq, d), F32),
        compiler_params=_params("arbitrary", "arbitrary"),
        name="moe_combine",
    )(dest, x, mod, meta, ys)


def _hier_moe(x, mod, gain, w_group, b_group, w_expert, b_expert, w_gate, w_up, w_down, layer):
    bsz, seq, d = x.shape
    groups = w_group.shape[1]
    n_exp = w_expert.shape[1]
    per_group = n_exp // groups
    n_tok = bsz * seq
    m = n_tok * TOP_K
    blk = 512
    tm = 512

    pad = LANES - groups - n_exp
    w_router = jnp.concatenate([w_group, w_expert, jnp.zeros((d, pad), F32)], axis=1)
    b_router = jnp.concatenate([b_group, b_expert, jnp.zeros((pad,), F32)]).reshape(1, LANES)
    h, meta, cnt = _router(x, mod, gain, w_router, b_router, groups, per_group, tm)

    counts = cnt[0, groups:groups + n_exp].astype(I32)
    padded = (counts + blk - 1) // blk * blk
    pad_end = jnp.cumsum(padded)
    pad_start = pad_end - padded
    n_blocks = -(-m // blk) + n_exp
    block_expert = jnp.minimum(
        jnp.searchsorted(pad_end, jnp.arange(n_blocks, dtype=I32) * blk, side='right'), n_exp - 1).astype(I32)
    n_used = (pad_end[-1] // blk).astype(I32).reshape(1)
    meta_flat = meta.reshape(n_tok, 8)
    e_idx = meta_flat[:, 0:TOP_K].astype(I32)
    rank = meta_flat[:, TOP_K:2 * TOP_K].astype(I32)
    dest = (pad_start[e_idx] + rank).reshape(m)

    xs = _dispatch(h.reshape(n_tok, d), dest, jnp.zeros((n_blocks * blk, d), F32), tm)
    ys = _experts(xs, block_expert, n_used, w_gate, w_up, w_down, layer, blk)
    return _combine(dest, x, mod, meta, ys, tm)


def kernel(x, c, mod_w, mod_b, norm_mix_g, norm_ffn_g, gdn_w_in, gdn_conv_w, gdn_a_log, gdn_dt_bias, gdn_out_norm_g, gdn_w_out, kv_mod_w, kv_mod_b, kv_norm_g, kv_w, kv_forget_b, k_norm_g, fox_w_qz, fox_q_norm_g, fox_w_out, moe_w_group, moe_b_group, moe_w_expert, moe_b_expert, moe_w_gate, moe_w_up, moe_w_down):
    bsz, seq, d = x.shape
    depth = mod_w.shape[0]
    n_a = gdn_w_in.shape[0]
    gdn_heads = gdn_a_log.shape[1]
    gdn_inner = gdn_heads * GDN_HEAD_DIM
    fox_heads = kv_forget_b.shape[0]
    fox_inner = fox_heads * FOX_HEAD_DIM
    tm = 512

    mod_all = _mod_vectors(c, mod_w, mod_b).reshape(depth, bsz, 1, 6 * d)
    kv_mod = _mod_vectors(c, kv_mod_w[None], kv_mod_b[None]).reshape(bsz, 1, 2 * d)
    bd = _block_diag_ones(fox_inner, FOX_HEAD_DIM)

    k_sh = v_sh = fq = fk = None
    for layer in range(depth):
        mod = mod_all[layer]
        gain_mix = norm_mix_g[layer].reshape(1, d)
        if layer < n_a:
            w_in = gdn_w_in[layer]
            qkv, sz, ab = _gdn_in_proj(x, mod, gain_mix, w_in[:, :4 * gdn_inner].astype(BF16),
                                       w_in[:, 4 * gdn_inner:].astype(BF16), gdn_conv_w[layer], gdn_heads, tm)
            o = _gdn_core(qkv, sz, ab, gdn_a_log[layer], gdn_dt_bias[layer],
                          gdn_out_norm_g[layer], gdn_heads, 2 * GDN_CHUNK)
            x = _out_proj_residual(o, x, mod, gdn_w_out[layer].astype(BF16), 2 * d, tm)
        else:
            j = layer - n_a
            wqz = fox_w_qz[j]
            qg = jnp.tile(fox_q_norm_g[j], fox_heads).reshape(1, fox_inner)
            q, z = _fox_qz_proj(x, mod, gain_mix, wqz[:, :fox_inner].astype(BF16),
                                wqz[:, fox_inner:].astype(BF16), qg, bd, tm)
            o = _fox_attention(q, k_sh, v_sh, fq, fk, z, 512)
            x = _out_proj_residual(o, x, mod, fox_w_out[j].astype(BF16), 2 * d, tm)
        x = _hier_moe(x, mod, norm_ffn_g[layer].reshape(1, d), moe_w_group[layer], moe_b_group[layer],
                      moe_w_expert[layer], moe_b_expert[layer], moe_w_gate, moe_w_up, moe_w_down, layer)
        if layer == n_a - 1:
            kg = jnp.tile(k_norm_g, fox_heads).reshape(1, fox_inner)
            k_sh, v_sh, fq, fk = _shared_kv(
                x, kv_mod, kv_norm_g.reshape(1, d), kv_w[:, :fox_inner].astype(BF16),
                kv_w[:, fox_inner:2 * fox_inner].astype(BF16), kv_w[:, 2 * fox_inner:].astype(BF16),
                kv_forget_b.reshape(1, fox_heads), kg, bd, fox_heads, tm)
    return x
```

```python
import functools

import jax
import jax.numpy as jnp
import numpy as np
from jax import lax
from jax.experimental import pallas as pl
from jax.experimental.pallas import tpu as pltpu

F32 = jnp.float32
BF16 = jnp.bfloat16
I32 = jnp.int32

EPS = 1e-6
GDN_CHUNK = 64
GDN_HEAD_DIM = 128
FOX_HEAD_DIM = 64
TOP_K = 2
LANES = 128
SUBLANES = 8
VMEM_LIMIT = 56 * 1024 * 1024
HIGHEST = lax.Precision.HIGHEST

NT_DIMS = (((1,), (1,)), ((), ()))
TN_DIMS = (((0,), (0,)), ((), ()))


def _params(*sem):
    return pltpu.CompilerParams(dimension_semantics=sem, vmem_limit_bytes=VMEM_LIMIT)


def _sigmoid(x):
    return 1.0 / (1.0 + jnp.exp(-x))


def _silu(x):
    return x * _sigmoid(x)


def _softplus(x):
    return jnp.maximum(x, 0.0) + jnp.log(1.0 + jnp.exp(-jnp.abs(x)))


def _modulated(x, gain, scale, shift):
    ms = jnp.mean(x * x, axis=-1, keepdims=True)
    y = x * lax.rsqrt(ms + EPS)
    return (y * gain) * (1.0 + scale) + shift


def _bdot(a, b, dims=None):
    a = a.astype(BF16)
    b = b.astype(BF16)
    if dims is None:
        return jnp.dot(a, b, preferred_element_type=F32)
    return lax.dot_general(a, b, dims, preferred_element_type=F32)


def _hdot(a, b, dims=None):
    if dims is None:
        return jnp.dot(a, b, preferred_element_type=F32, precision=HIGHEST)
    return lax.dot_general(a, b, dims, preferred_element_type=F32, precision=HIGHEST)


def _tile_of(row):
    return lax.shift_right_logical(row, SUBLANES.bit_length() - 1)


def _row_in_tile(row):
    return row & (SUBLANES - 1)


def _iota2(shape, dim):
    return lax.broadcasted_iota(I32, shape, dim)


def _mod_kernel(c_ref, w_ref, b_ref, o_ref):
    c = c_ref[...]
    o_ref[...] = _hdot(_silu(c), w_ref[...]) + b_ref[...]


def _mod_vectors(c, w, b):
    n_layers, d, n = w.shape
    bsz = c.shape[0]
    tn = 1536 if n % 1536 == 0 else n
    return pl.pallas_call(
        _mod_kernel,
        grid=(n_layers, n // tn),
        in_specs=[
            pl.BlockSpec((bsz, d), lambda l, j: (0, 0)),
            pl.BlockSpec((None, d, tn), lambda l, j: (l, 0, j)),
            pl.BlockSpec((None, 1, tn), lambda l, j: (l, 0, j)),
        ],
        out_specs=pl.BlockSpec((None, bsz, tn), lambda l, j: (l, 0, j)),
        out_shape=jax.ShapeDtypeStruct((n_layers, bsz, n), F32),
        compiler_params=_params("parallel", "parallel"),
        name="mod_vectors",
    )(c, w, b.reshape(n_layers, 1, n))


def _gdn_in_kernel(x_ref, mod_ref, g_ref, w_ref, wab_ref, cw_ref, qkv_ref, sz_ref, oab_ref, xpad_ref,
                   *, d, tm, heads, dh, width):
    inner = heads * dh
    halo = 8
    i = pl.program_id(1)

    @pl.when(i == 0)
    def _():
        xpad_ref[0:halo, :] = jnp.zeros((halo, 3 * inner), F32)

    h = _modulated(x_ref[...], g_ref[...], mod_ref[:, d:2 * d], mod_ref[:, 0:d]).astype(BF16)
    oab_ref[...] = jnp.dot(h, wab_ref[...], preferred_element_type=F32)
    z = jnp.dot(h, w_ref[:, 3 * inner:4 * inner], preferred_element_type=F32)
    sz_ref[...] = _silu(z).astype(sz_ref.dtype)
    xpad_ref[halo:halo + tm, :] = jnp.dot(h, w_ref[:, 0:3 * inner], preferred_element_type=F32)
    for j in range(3 * heads):
        cols = slice(j * dh, (j + 1) * dh)
        acc = xpad_ref[halo:halo + tm, cols] * cw_ref[width - 1:width, cols]
        for s in range(1, width):
            acc = acc + xpad_ref[halo - s:halo - s + tm, cols] * cw_ref[width - 1 - s:width - s, cols]
        y = _silu(acc)
        if j < 2 * heads:
            y = y * lax.rsqrt(jnp.sum(y * y, axis=-1, keepdims=True) + EPS)
        if j < heads:
            y = y * (dh ** -0.5)
        qkv_ref[:, cols] = y.astype(qkv_ref.dtype)
    xpad_ref[0:halo, :] = xpad_ref[tm:tm + halo, :]


def _gdn_in_proj(x, mod, gain, w_main, w_ab, conv_w, heads, tm):
    bsz, seq, d = x.shape
    dh = GDN_HEAD_DIM
    inner = heads * dh
    nab = w_ab.shape[1]
    width = conv_w.shape[0]
    return pl.pallas_call(
        functools.partial(_gdn_in_kernel, d=d, tm=tm, heads=heads, dh=dh, width=width),
        grid=(bsz, seq // tm),
        in_specs=[
            pl.BlockSpec((None, tm, d), lambda b, i: (b, i, 0)),
            pl.BlockSpec((None, 1, mod.shape[-1]), lambda b, i: (b, 0, 0)),
            pl.BlockSpec((1, d), lambda b, i: (0, 0)),
            pl.BlockSpec((d, 4 * inner), lambda b, i: (0, 0)),
            pl.BlockSpec((d, nab), lambda b, i: (0, 0)),
            pl.BlockSpec((width, 3 * inner), lambda b, i: (0, 0)),
        ],
        out_specs=[
            pl.BlockSpec((None, tm, 3 * inner), lambda b, i: (b, i, 0)),
            pl.BlockSpec((None, tm, inner), lambda b, i: (b, i, 0)),
            pl.BlockSpec((None, tm, nab), lambda b, i: (b, i, 0)),
        ],
        out_shape=[
            jax.ShapeDtypeStruct((bsz, seq, 3 * inner), BF16),
            jax.ShapeDtypeStruct((bsz, seq, inner), BF16),
            jax.ShapeDtypeStruct((bsz, seq, nab), F32),
        ],
        scratch_shapes=[pltpu.VMEM((8 + tm, 3 * inner), F32)],
        compiler_params=_params("parallel", "arbitrary"),
        name="gdn_in_proj",
    )(x, mod, gain, w_main, w_ab, conv_w)


def _gdn_kernel(q_ref, k_ref, v_ref, sz_ref, ab_ref, alog_ref, dtb_ref, og_ref, o_ref, state_ref,
                *, tb, heads, dh):
    c = GDN_CHUNK
    pairs = heads // 2
    i = pl.program_id(1)

    @pl.when(i == 0)
    def _():
        state_ref[...] = jnp.zeros(state_ref.shape, F32)

    ab = ab_ref[...]
    beta_all = _sigmoid(ab[:, heads:2 * heads])
    g_all = -jnp.exp(alog_ref[...]) * _softplus(ab[:, 0:heads] + dtb_ref[...])
    og = og_ref[...]

    row = _iota2((c, 2 * c), 0)
    lane = _iota2((c, 2 * c), 1)
    first = lane < c
    col = jnp.where(first, lane, lane - c)
    incl = row >= col
    strict = row > col
    eye_p = jnp.where(row == col, 1.0, 0.0).astype(F32)
    tri = jnp.where(_iota2((c, c), 0) >= _iota2((c, c), 1), 1.0, 0.0).astype(F32)
    bd_small = (_iota2((2 * c, 2 * c), 0) < c) == (_iota2((2 * c, 2 * c), 1) < c)
    bd_wide = (_iota2((2 * c, 2 * dh), 0) < c) == (_iota2((2 * c, 2 * dh), 1) < dh)
    first_h = _iota2((pairs, 2 * c), 1) < c
    sel0 = jnp.where(_iota2((pairs, heads), 1) == 2 * _iota2((pairs, heads), 0), 1.0, 0.0).astype(F32)
    sel1 = jnp.where(_iota2((pairs, heads), 1) == 2 * _iota2((pairs, heads), 0) + 1, 1.0, 0.0).astype(F32)

    def block_diag(x, mask):
        return jnp.where(mask, jnp.concatenate([x, x], axis=0), 0.0).astype(BF16)

    def pair_cols(x, p):
        return jnp.concatenate([jnp.broadcast_to(x[:, 2 * p:2 * p + 1], (c, dh)),
                                jnp.broadcast_to(x[:, 2 * p + 1:2 * p + 2], (c, dh))], axis=1)

    units = [(ci, p) for ci in range(tb // c) for p in range(pairs)]
    gcums, glasts, pk, tt, attn, w_u, qg, kdec = {}, {}, {}, {}, {}, {}, {}, {}

    for ci in range(tb // c):
        r0 = ci * c
        gcum = _hdot(tri, g_all[r0:r0 + c, :])
        gc2 = jnp.concatenate([gcum, gcum], axis=0)
        gt = jnp.where(first_h, _hdot(sel0, gc2, NT_DIMS), _hdot(sel1, gc2, NT_DIMS))
        glast = gcum[c - 1:c, :]
        gcums[ci] = gcum
        glasts[ci] = glast
        for p in range(pairs):
            u = (ci, p)
            cols = slice(2 * p * dh, (2 * p + 2) * dh)
            kp = k_ref[r0:r0 + c, cols].astype(F32)
            qp = q_ref[r0:r0 + c, cols].astype(F32)
            vp = v_ref[r0:r0 + c, cols].astype(F32)
            gcol = jnp.where(first, gcum[:, 2 * p:2 * p + 1], gcum[:, 2 * p + 1:2 * p + 2])
            decay = jnp.exp(jnp.where(incl, gcol - gt[p:p + 1, :], -jnp.inf))
            beta2 = pair_cols(beta_all[r0:r0 + c, :], p)
            gcum2 = pair_cols(gcum, p)
            egc2 = jnp.exp(gcum2)
            kb = kp * beta2
            y = block_diag(kp, bd_wide)
            kq = _bdot(jnp.concatenate([kb, qp], axis=0), y, NT_DIMS)
            a_mat = jnp.where(strict, kq[0:c, :] * decay, 0.0)
            attn[u] = kq[c:2 * c, :] * decay
            pk[u] = -a_mat
            tt[u] = eye_p - a_mat
            kbg = kb * egc2
            vb = vp * beta2
            w_u[u] = jnp.concatenate(
                [jnp.concatenate([kbg[:, 0:dh], vb[:, 0:dh]], axis=1),
                 jnp.concatenate([kbg[:, dh:2 * dh], vb[:, dh:2 * dh]], axis=1)], axis=0).astype(BF16)
            qg[u] = qp * egc2
            kdec[u] = kp * jnp.exp(pair_cols(jnp.broadcast_to(glast, (c, heads)), p) - gcum2)

    span = 2
    while span <= c:
        last = span == c
        for u in units:
            bd = block_diag(pk[u], bd_small)
            if span == 2:
                pk[u] = _bdot(pk[u], bd)
            elif last:
                tt[u] = tt[u] + _bdot(tt[u], bd)
            else:
                both = _bdot(jnp.concatenate([pk[u], tt[u]], axis=0), bd)
                pk[u] = both[0:c, :]
                tt[u] = tt[u] + both[c:2 * c, :]
        span *= 2
    for u in units:
        t = tt[u]
        lhs = jnp.concatenate([jnp.where(first, t, 0.0), jnp.where(first, 0.0, t)], axis=0)
        w_u[u] = _bdot(lhs, w_u[u])

    for ci in range(tb // c):
        r0 = ci * c
        for p in range(pairs):
            u = (ci, p)
            wu = w_u[u]
            st, wq, vn = [], [], []
            for s in range(2):
                h = 2 * p + s
                st.append(state_ref[h])
                wq.append(_bdot(jnp.concatenate([wu[s * c:(s + 1) * c, 0:dh], qg[u][:, s * dh:(s + 1) * dh]], axis=0),
                                st[s]))
                vn.append(wu[s * c:(s + 1) * c, dh:2 * dh] - wq[s][0:c, :])
            vn2 = jnp.concatenate(vn, axis=0).astype(BF16)
            for s in range(2):
                h = 2 * p + s
                am = jnp.where(first, attn[u], 0.0) if s == 0 else jnp.where(first, 0.0, attn[u])
                o = wq[s][c:2 * c, :] + _bdot(am, vn2)
                gl = glasts[ci][:, h:h + 1]
                state_ref[h] = st[s] * jnp.exp(gl) + _bdot(kdec[u][:, s * dh:(s + 1) * dh], vn[s], TN_DIMS)
                on = o * lax.rsqrt(jnp.mean(o * o, axis=-1, keepdims=True) + EPS) * og
                szh = sz_ref[r0:r0 + c, h * dh:(h + 1) * dh].astype(F32)
                o_ref[r0:r0 + c, h * dh:(h + 1) * dh] = (on * szh).astype(o_ref.dtype)


def _gdn_core(qkv, sz, ab, a_log, dt_bias, out_g, heads, tb):
    bsz, seq, _ = qkv.shape
    dh = GDN_HEAD_DIM
    inner = heads * dh
    return pl.pallas_call(
        functools.partial(_gdn_kernel, tb=tb, heads=heads, dh=dh),
        grid=(bsz, seq // tb),
        in_specs=[
            pl.BlockSpec((None, tb, inner), lambda b, i: (b, i, 0)),
            pl.BlockSpec((None, tb, inner), lambda b, i: (b, i, 1)),
            pl.BlockSpec((None, tb, inner), lambda b, i: (b, i, 2)),
            pl.BlockSpec((None, tb, inner), lambda b, i: (b, i, 0)),
            pl.BlockSpec((None, tb, 2 * heads), lambda b, i: (b, i, 0)),
            pl.BlockSpec((1, heads), lambda b, i: (0, 0)),
            pl.BlockSpec((1, heads), lambda b, i: (0, 0)),
            pl.BlockSpec((1, dh), lambda b, i: (0, 0)),
        ],
        out_specs=pl.BlockSpec((None, tb, inner), lambda b, i: (b, i, 0)),
        out_shape=jax.ShapeDtypeStruct((bsz, seq, inner), BF16),
        scratch_shapes=[pltpu.VMEM((heads, dh, dh), F32)],
        compiler_params=_params("parallel", "arbitrary"),
        name="gdn_core",
    )(qkv, qkv, qkv, sz, ab, a_log.reshape(1, heads), dt_bias.reshape(1, heads), out_g.reshape(1, dh))


def _out_proj_kernel(o_ref, x_ref, mod_ref, w_ref, y_ref, *, d, gate_off):
    y = jnp.dot(o_ref[...], w_ref[...], preferred_element_type=F32)
    y_ref[...] = x_ref[...] + mod_ref[:, gate_off:gate_off + d] * y


def _out_proj_residual(o, x, mod, w, gate_off, tm):
    bsz, seq, d = x.shape
    k = o.shape[-1]
    return pl.pallas_call(
        functools.partial(_out_proj_kernel, d=d, gate_off=gate_off),
        grid=(bsz, seq // tm),
        in_specs=[
            pl.BlockSpec((None, tm, k), lambda b, i: (b, i, 0)),
            pl.BlockSpec((None, tm, d), lambda b, i: (b, i, 0)),
            pl.BlockSpec((None, 1, mod.shape[-1]), lambda b, i: (b, 0, 0)),
            pl.BlockSpec((k, d), lambda b, i: (0, 0)),
        ],
        out_specs=pl.BlockSpec((None, tm, d), lambda b, i: (b, i, 0)),
        out_shape=jax.ShapeDtypeStruct((bsz, seq, d), F32),
        compiler_params=_params("parallel", "parallel"),
        name="out_proj_residual",
    )(o, x, mod, w)


def _head_rms(x, bd, gain, dh):
    ss = jnp.dot((x * x).astype(BF16), bd, preferred_element_type=F32) * (1.0 / dh)
    return x * lax.rsqrt(ss + EPS) * gain


def _block_diag_ones(n, blk):
    r = jnp.arange(n, dtype=I32) // blk
    return (r[:, None] == r[None, :]).astype(BF16)


N_SPLIT = 3


def _bias_base(head, dh):
    return dh if head % 2 == 0 else 0


def _split3(x):
    hi = x.astype(BF16)
    r = x - hi.astype(F32)
    mid = r.astype(BF16)
    lo = (r - mid.astype(F32)).astype(BF16)
    return hi, mid, lo


def _key_bias_placement(heads, dh):
    place = np.zeros((N_SPLIT, heads, heads * 2 * dh), np.float32)
    for j in range(N_SPLIT):
        for h in range(heads):
            place[j, h, h * 2 * dh + _bias_base(h, dh) + j] = 1.0
    return jnp.asarray(place, BF16)


def _kv_kernel(x_ref, mod_ref, g_ref, wk_ref, wv_ref, wf_ref, fb_ref, kg_ref, bd_ref, place_ref,
               ka_ref, vt_ref, ft_ref, carry_ref, *, d, tm, heads, dh):
    i = pl.program_id(1)

    @pl.when(i == 0)
    def _():
        carry_ref[...] = jnp.zeros(carry_ref.shape, F32)

    h = _modulated(x_ref[...], g_ref[...], mod_ref[:, d:2 * d], mod_ref[:, 0:d]).astype(BF16)
    kraw = jnp.dot(h, wk_ref[...], preferred_element_type=F32)
    kn = _head_rms(kraw, bd_ref[...], kg_ref[...], dh)
    v = jnp.dot(h, wv_ref[...], preferred_element_type=F32).astype(BF16)
    f = jnp.dot(h, wf_ref[...], preferred_element_type=F32) + fb_ref[...]
    log_f = -_softplus(-f)
    tri = jnp.where(_iota2((tm, tm), 0) >= _iota2((tm, tm), 1), 1.0, 0.0).astype(F32)
    fcum = _hdot(tri, log_f) + carry_ref[...]
    carry_ref[...] = fcum[tm - 1:tm, :]
    eye_h = jnp.where(_iota2((heads, heads), 0) == _iota2((heads, heads), 1), 1.0, 0.0).astype(F32)
    fcum_t = _hdot(eye_h, fcum, NT_DIMS)
    for p in range(heads // 2):
        ft_ref[p] = fcum_t[2 * p:2 * p + 2, :]

    pieces = _split3(-fcum)
    bias = jnp.dot(pieces[0], place_ref[0], preferred_element_type=F32)
    for j in range(1, N_SPLIT):
        bias = bias + jnp.dot(pieces[j], place_ref[j], preferred_element_type=F32)
    lane = _iota2((tm, 2 * dh), 1)
    for hh in range(heads):
        p = hh // 2
        base = _bias_base(hh, dh)
        real = (lane < dh) if hh % 2 == 0 else (lane >= dh)
        ones = (lane >= base + N_SPLIT) & (lane < base + 2 * N_SPLIT)
        blk = jnp.where(real, kn[:, 2 * p * dh:(2 * p + 2) * dh],
                        jnp.where(ones, 1.0, bias[:, hh * 2 * dh:(hh + 1) * 2 * dh]))
        ka_ref[hh] = blk.astype(ka_ref.dtype)

    eye = jnp.where(_iota2((2 * dh, 2 * dh), 0) == _iota2((2 * dh, 2 * dh), 1), 1.0, 0.0).astype(BF16)
    rowi = _iota2((2 * dh, tm), 0)
    for p in range(heads // 2):
        vt = lax.dot_general(eye, v[:, 2 * p * dh:(2 * p + 2) * dh], NT_DIMS, preferred_element_type=F32)
        vt_ref[2 * p] = jnp.where(rowi < dh, vt, 1.0).astype(vt_ref.dtype)
        vt_ref[2 * p + 1] = jnp.where(rowi >= dh, vt, 1.0).astype(vt_ref.dtype)


def _shared_kv(x, mod, gain, wk, wv, wf, fb, kgain, bd, heads, tm):
    bsz, seq, d = x.shape
    dh = FOX_HEAD_DIM
    inner = heads * dh
    pairs = heads // 2
    place = _key_bias_placement(heads, dh)
    return pl.pallas_call(
        functools.partial(_kv_kernel, d=d, tm=tm, heads=heads, dh=dh),
        grid=(bsz, seq // tm),
        in_specs=[
            pl.BlockSpec((None, tm, d), lambda b, i: (b, i, 0)),
            pl.BlockSpec((None, 1, mod.shape[-1]), lambda b, i: (b, 0, 0)),
            pl.BlockSpec((1, d), lambda b, i: (0, 0)),
            pl.BlockSpec((d, inner), lambda b, i: (0, 0)),
            pl.BlockSpec((d, inner), lambda b, i: (0, 0)),
            pl.BlockSpec((d, heads), lambda b, i: (0, 0)),
            pl.BlockSpec((1, heads), lambda b, i: (0, 0)),
            pl.BlockSpec((1, inner), lambda b, i: (0, 0)),
            pl.BlockSpec((inner, inner), lambda b, i: (0, 0)),
            pl.BlockSpec(place.shape, lambda b, i: (0, 0, 0)),
        ],
        out_specs=[
            pl.BlockSpec((None, heads, tm, 2 * dh), lambda b, i: (b, 0, i, 0)),
            pl.BlockSpec((None, heads, 2 * dh, tm), lambda b, i: (b, 0, 0, i)),
            pl.BlockSpec((None, pairs, 2, tm), lambda b, i: (b, 0, 0, i)),
        ],
        out_shape=[
            jax.ShapeDtypeStruct((bsz, heads, seq, 2 * dh), BF16),
            jax.ShapeDtypeStruct((bsz, heads, 2 * dh, seq), BF16),
            jax.ShapeDtypeStruct((bsz, pairs, 2, seq), F32),
        ],
        scratch_shapes=[pltpu.VMEM((1, heads), F32)],
        compiler_params=_params("parallel", "arbitrary"),
        name="shared_kv",
    )(x, mod, gain, wk, wv, wf, fb, kgain, bd, place)


def _fox_qz_kernel(x_ref, mod_ref, g_ref, wq_ref, wz_ref, qg_ref, bd_ref, ft_ref, qt_ref, z_ref,
                   *, d, dh, heads, tm):
    h = _modulated(x_ref[...], g_ref[...], mod_ref[:, d:2 * d], mod_ref[:, 0:d]).astype(BF16)
    qraw = jnp.dot(h, wq_ref[...], preferred_element_type=F32)
    qn = (_head_rms(qraw, bd_ref[...], qg_ref[...], dh) * (dh ** -0.5)).astype(BF16)
    z_ref[...] = jnp.dot(h, wz_ref[...], preferred_element_type=F32).astype(z_ref.dtype)
    eye = jnp.where(_iota2((2 * dh, 2 * dh), 0) == _iota2((2 * dh, 2 * dh), 1), 1.0, 0.0).astype(BF16)
    rowi = _iota2((2 * dh, tm), 0)
    for p in range(heads // 2):
        qt = lax.dot_general(eye, qn[:, 2 * p * dh:(2 * p + 2) * dh], NT_DIMS, preferred_element_type=F32)
        for s in range(2):
            hh = 2 * p + s
            base = _bias_base(hh, dh)
            real = (rowi < dh) if s == 0 else (rowi >= dh)
            aug = jnp.where(real, qt, 0.0)
            aug = jnp.where((rowi >= base) & (rowi < base + N_SPLIT), 1.0, aug)
            pieces = _split3(ft_ref[p, s:s + 1, :])
            for j in range(N_SPLIT):
                aug = jnp.where(rowi == base + N_SPLIT + j, pieces[j].astype(F32), aug)
            qt_ref[hh] = aug.astype(qt_ref.dtype)


def _fox_qz_proj(x, mod, gain, wq, wz, qgain, bd, ft, heads, tm):
    bsz, seq, d = x.shape
    dh = FOX_HEAD_DIM
    inner = wq.shape[1]
    pairs = heads // 2
    return pl.pallas_call(
        functools.partial(_fox_qz_kernel, d=d, dh=dh, heads=heads, tm=tm),
        grid=(bsz, seq // tm),
        in_specs=[
            pl.BlockSpec((None, tm, d), lambda b, i: (b, i, 0)),
            pl.BlockSpec((None, 1, mod.shape[-1]), lambda b, i: (b, 0, 0)),
            pl.BlockSpec((1, d), lambda b, i: (0, 0)),
            pl.BlockSpec((d, inner), lambda b, i: (0, 0)),
            pl.BlockSpec((d, inner), lambda b, i: (0, 0)),
            pl.BlockSpec((1, inner), lambda b, i: (0, 0)),
            pl.BlockSpec((inner, inner), lambda b, i: (0, 0)),
            pl.BlockSpec((None, pairs, 2, tm), lambda b, i: (b, 0, 0, i)),
        ],
        out_specs=[
            pl.BlockSpec((None, heads, 2 * dh, tm), lambda b, i: (b, 0, 0, i)),
            pl.BlockSpec((None, tm, inner), lambda b, i: (b, i, 0)),
        ],
        out_shape=[
            jax.ShapeDtypeStruct((bsz, heads, 2 * dh, seq), BF16),
            jax.ShapeDtypeStruct((bsz, seq, inner), BF16),
        ],
        compiler_params=_params("parallel", "parallel"),
        name="fox_qz_proj",
    )(x, mod, gain, wq, wz, qgain, bd, ft)


def _fox_kernel(qt_ref, ka_ref, vt_ref, z_ref, o_ref, s_ref, p_ref, acc_ref, *, tq, dh):
    i = pl.program_id(2)
    acc_ref[...] = jnp.zeros(acc_ref.shape, F32)
    key_idx = _iota2((tq, LANES), 0)
    qry_idx = _iota2((tq, LANES), 1)

    wide = 2 * LANES
    units = [(h, c) for h in range(2) for c in range(tq // wide)]

    def scores(kb, slot, u):
        h, c = u
        start = pl.multiple_of(kb * tq, tq)
        s_ref[slot, h, :, c * wide:(c + 1) * wide] = jnp.dot(
            ka_ref[h, pl.ds(start, tq), :], qt_ref[h, :, c * wide:(c + 1) * wide], preferred_element_type=F32)

    def softmax_pv(kb, slot, u, ms, masked, m_out):
        h, c = u
        start = pl.multiple_of(kb * tq, tq)
        alphas = []
        for t in range(wide // LANES):
            q0 = c * wide + t * LANES
            cols = slice(q0, q0 + LANES)
            s = s_ref[slot, h, :, cols]
            if masked:
                s = jnp.where(key_idx <= qry_idx + q0, s, -jnp.inf)
            m_old = ms[h][:, cols]
            parts = [s[r:r + 64, :] for r in range(0, tq, 64)]
            while len(parts) > 1:
                parts = [jnp.maximum(parts[n], parts[n + 1]) for n in range(0, len(parts), 2)]
            m_new = jnp.maximum(m_old, jnp.max(parts[0], axis=0, keepdims=True))
            p_ref[h, :, cols] = jnp.exp((s - m_new).astype(BF16))
            alphas.append(jnp.exp(m_old - m_new))
            m_out[(h, q0)] = m_new
        cols = slice(c * wide, (c + 1) * wide)
        alpha = jnp.concatenate(alphas, axis=1)
        acc_ref[h, :, cols] = alpha * acc_ref[h, :, cols] + jnp.dot(
            vt_ref[h, :, pl.ds(start, tq)], p_ref[h, :, cols], preferred_element_type=F32)

    def step(kb, slot, ms, masked, prefetch):
        m_out = {}
        if prefetch:
            for u in units:
                scores(kb + 1, 1 - slot, u)
        for u in units:
            softmax_pv(kb, slot, u, ms, masked, m_out)
        return tuple(jnp.concatenate([m_out[(h, q0)] for q0 in range(0, tq, LANES)], axis=1) for h in range(2))

    def by_parity(kb, ms, masked, prefetch):
        return lax.cond(kb % 2 == 0,
                        lambda ms: step(kb, 0, ms, masked, prefetch),
                        lambda ms: step(kb, 1, ms, masked, prefetch), ms)

    for u in units:
        scores(0, 0, u)
    neg = jnp.full((1, tq), -jnp.inf, F32)
    ms = lax.fori_loop(0, i, lambda kb, ms: by_parity(kb, ms, False, True), (neg, neg))
    by_parity(i, ms, True, False)
    a0 = acc_ref[0]
    a1 = acc_ref[1]
    num = jnp.concatenate([a0[0:dh, :], a1[dh:2 * dh, :]], axis=0)
    den = jnp.concatenate([a0[dh:2 * dh, :], a1[0:dh, :]], axis=0)
    o = jnp.transpose(num / den)
    o_ref[...] = (o * _sigmoid(z_ref[...].astype(F32))).astype(o_ref.dtype)


def _fox_attention(qt, ka, vt, z, tq):
    bsz, heads, feat, seq = qt.shape
    dh = FOX_HEAD_DIM
    pairs = heads // 2
    return pl.pallas_call(
        functools.partial(_fox_kernel, tq=tq, dh=dh),
        grid=(bsz, pairs, seq // tq),
        in_specs=[
            pl.BlockSpec((None, 2, feat, tq), lambda b, p, i: (b, p, 0, i)),
            pl.BlockSpec((None, 2, seq, feat), lambda b, p, i: (b, p, 0, 0)),
            pl.BlockSpec((None, 2, feat, seq), lambda b, p, i: (b, p, 0, 0)),
            pl.BlockSpec((None, tq, 2 * dh), lambda b, p, i: (b, i, p)),
        ],
        out_specs=pl.BlockSpec((None, tq, 2 * dh), lambda b, p, i: (b, i, p)),
        out_shape=jax.ShapeDtypeStruct((bsz, seq, heads * dh), BF16),
        scratch_shapes=[
            pltpu.VMEM((2, 2, tq, tq), F32),
            pltpu.VMEM((2, tq, tq), BF16),
            pltpu.VMEM((2, feat, tq), F32),
        ],
        compiler_params=_params("parallel", "parallel", "parallel"),
        name="fox_attention",
    )(qt, ka, vt, z)


def _router_kernel(x_ref, mod_ref, g_ref, wr_ref, br_ref, h_ref, meta_ref, cnt_ref, carry_ref,
                   *, d, tm, groups, per_group):
    step = pl.program_id(0) * pl.num_programs(1) + pl.program_id(1)

    @pl.when(step == 0)
    def _():
        carry_ref[...] = jnp.zeros(carry_ref.shape, F32)

    h = _modulated(x_ref[...], g_ref[...], mod_ref[:, 4 * d:5 * d], mod_ref[:, 3 * d:4 * d])
    h_ref[...] = h
    logits = _hdot(h, wr_ref[...]) + br_ref[...]
    lane = _iota2((tm, LANES), 1)
    neg_inf = jnp.float32(-jnp.inf)
    big = jnp.int32(LANES)

    def first_argmax(vals, mask):
        mv = jnp.where(mask, vals, neg_inf)
        mx = jnp.max(mv, axis=-1, keepdims=True)
        idx = jnp.min(jnp.where(mask & (mv == mx), lane, big), axis=-1, keepdims=True)
        return mx, idx

    gmask = lane < groups
    gmax, gidx = first_argmax(logits, gmask)
    gsum = jnp.sum(jnp.where(gmask, jnp.exp(logits - gmax), 0.0), axis=-1, keepdims=True)
    group_gate = 1.0 / gsum
    lo = groups + gidx * per_group
    emask = (lane >= lo) & (lane < lo + per_group)
    e1max, e1lane = first_argmax(logits, emask)
    e2max, e2lane = first_argmax(logits, emask & (lane != e1lane))
    esum = jnp.sum(jnp.where(emask, jnp.exp(logits - e1max), 0.0), axis=-1, keepdims=True)
    p1 = 1.0 / esum
    p2 = jnp.exp(e2max - e1max) / esum
    psum = p1 + p2
    w1 = group_gate * (p1 / psum)
    w2 = group_gate * (p2 / psum)

    oh1 = (lane == e1lane).astype(F32)
    oh2 = (lane == e2lane).astype(F32)
    strict = jnp.where(_iota2((tm, tm), 0) > _iota2((tm, tm), 1), 1.0, 0.0).astype(BF16)
    c1 = jnp.dot(strict, oh1.astype(BF16), preferred_element_type=F32)
    c2 = jnp.dot(strict, oh2.astype(BF16), preferred_element_type=F32)
    tot1 = jnp.sum(oh1, axis=0, keepdims=True)
    tot2 = jnp.sum(oh2, axis=0, keepdims=True)
    carry = carry_ref[...]
    rank1 = jnp.sum(oh1 * (c1 + carry), axis=-1, keepdims=True)
    rank2 = jnp.sum(oh2 * (c2 + carry + tot1), axis=-1, keepdims=True)
    carry = carry + tot1 + tot2
    carry_ref[...] = carry
    cnt_ref[...] = carry

    e1 = (e1lane - groups).astype(F32)
    e2 = (e2lane - groups).astype(F32)
    mlane = _iota2((tm, 8), 1)
    meta = jnp.where(mlane == 0, e1, 0.0)
    meta = jnp.where(mlane == 1, e2, meta)
    meta = jnp.where(mlane == 2, rank1, meta)
    meta = jnp.where(mlane == 3, rank2, meta)
    meta = jnp.where(mlane == 4, w1, meta)
    meta = jnp.where(mlane == 5, w2, meta)
    meta_ref[...] = meta


def _router(x, mod, gain, w_router, b_router, groups, per_group, tm):
    bsz, seq, d = x.shape
    return pl.pallas_call(
        functools.partial(_router_kernel, d=d, tm=tm, groups=groups, per_group=per_group),
        grid=(bsz, seq // tm),
        in_specs=[
            pl.BlockSpec((None, tm, d), lambda b, i: (b, i, 0)),
            pl.BlockSpec((None, 1, mod.shape[-1]), lambda b, i: (b, 0, 0)),
            pl.BlockSpec((1, d), lambda b, i: (0, 0)),
            pl.BlockSpec((d, LANES), lambda b, i: (0, 0)),
            pl.BlockSpec((1, LANES), lambda b, i: (0, 0)),
        ],
        out_specs=[
            pl.BlockSpec((None, tm, d), lambda b, i: (b, i, 0)),
            pl.BlockSpec((None, tm, 8), lambda b, i: (b, i, 0)),
            pl.BlockSpec((1, LANES), lambda b, i: (0, 0)),
        ],
        out_shape=[
            jax.ShapeDtypeStruct((bsz, seq, d), F32),
            jax.ShapeDtypeStruct((bsz, seq, 8), F32),
            jax.ShapeDtypeStruct((1, LANES), F32),
        ],
        scratch_shapes=[pltpu.VMEM((1, LANES), F32)],
        compiler_params=_params("arbitrary", "arbitrary"),
        name="moe_router",
    )(x, mod, gain, w_router, b_router)


def _dispatch_kernel(dest_ref, h_ref, xs_in_ref, xs_ref, sem, *, tm):
    del xs_in_ref

    def row_copy(g, s, k):
        dst = dest_ref[(g * SUBLANES + s) * TOP_K + k]
        return pltpu.make_async_copy(h_ref.at[g, pl.ds(s, 1)],
                                     xs_ref.at[_tile_of(dst), pl.ds(_row_in_tile(dst), 1)], sem)

    def start(g, carry):
        for s in range(SUBLANES):
            for k in range(TOP_K):
                row_copy(g, s, k).start(priority=k % 2)
        return carry

    def wait(g, carry):
        for s in range(SUBLANES):
            for k in range(TOP_K):
                row_copy(g, s, k).wait()
        return carry

    lax.fori_loop(0, tm // SUBLANES, start, 0)
    lax.fori_loop(0, tm // SUBLANES, wait, 0)


def _dispatch(h_flat, dest, xs_init, tm):
    n_tok, d = h_flat.shape
    rows = xs_init.shape[0]
    xs = pl.pallas_call(
        functools.partial(_dispatch_kernel, tm=tm),
        grid=(n_tok // tm,),
        in_specs=[
            pl.BlockSpec((tm * TOP_K,), lambda i: (i,), memory_space=pltpu.SMEM),
            pl.BlockSpec((tm // SUBLANES, SUBLANES, d), lambda i: (i, 0, 0)),
            pl.BlockSpec(memory_space=pl.ANY),
        ],
        out_specs=pl.BlockSpec(memory_space=pl.ANY),
        out_shape=jax.ShapeDtypeStruct((rows // SUBLANES, SUBLANES, d), xs_init.dtype),
        scratch_shapes=[pltpu.SemaphoreType.DMA(())],
        input_output_aliases={2: 0},
        compiler_params=_params("arbitrary"),
        name="moe_dispatch",
    )(dest, h_flat.reshape(n_tok // SUBLANES, SUBLANES, d), xs_init.reshape(rows // SUBLANES, SUBLANES, d))
    return xs.reshape(rows, d)


def _expert_kernel(be_ref, nu_ref, xs_ref, wg_ref, wu_ref, wd_ref, ys_ref, wgb, wub, wdb):
    i = pl.program_id(0)
    prev = be_ref[jnp.maximum(i - 1, 0)]
    fresh = (i == 0) | (be_ref[i] != prev)

    @pl.when(fresh)
    def _():
        wgb[...] = wg_ref[...].astype(BF16)
        wub[...] = wu_ref[...].astype(BF16)
        wdb[...] = wd_ref[...].astype(BF16)

    @pl.when(i < nu_ref[0])
    def _():
        x = xs_ref[...].astype(BF16)
        g = jnp.dot(x, wgb[...], preferred_element_type=F32)
        u = jnp.dot(x, wub[...], preferred_element_type=F32)
        mid = (_silu(g) * u).astype(BF16)
        ys_ref[...] = jnp.dot(mid, wdb[...], preferred_element_type=F32)

    @pl.when(i >= nu_ref[0])
    def _():
        ys_ref[...] = jnp.zeros(ys_ref.shape, ys_ref.dtype)


def _experts(xs, block_expert, n_used, w_gate, w_up, w_down, layer, blk):
    rows, d = xs.shape
    de = w_gate.shape[-1]
    n_blocks = rows // blk
    return pl.pallas_call(
        _expert_kernel,
        grid_spec=pltpu.PrefetchScalarGridSpec(
            num_scalar_prefetch=2,
            grid=(n_blocks,),
            in_specs=[
                pl.BlockSpec((blk, d), lambda i, be, nu: (i, 0)),
                pl.BlockSpec((None, None, d, de), lambda i, be, nu: (layer, be[i], 0, 0)),
                pl.BlockSpec((None, None, d, de), lambda i, be, nu: (layer, be[i], 0, 0)),
                pl.BlockSpec((None, None, de, d), lambda i, be, nu: (layer, be[i], 0, 0)),
            ],
            out_specs=pl.BlockSpec((blk, d), lambda i, be, nu: (i, 0)),
            scratch_shapes=[
                pltpu.VMEM((d, de), BF16),
                pltpu.VMEM((d, de), BF16),
                pltpu.VMEM((de, d), BF16),
            ],
        ),
        out_shape=jax.ShapeDtypeStruct((rows, d), F32),
        compiler_params=_params("arbitrary"),
        name="moe_experts",
    )(block_expert, n_used, xs, w_gate, w_up, w_down)


def _combine_kernel(dest_ref, x_ref, mod_ref, meta_ref, ys_ref, o_ref, ybuf, sem, *, tm, d):
    def row_copy(g, s, k):
        src = dest_ref[(g * SUBLANES + s) * TOP_K + k]
        return pltpu.make_async_copy(ys_ref.at[_tile_of(src), pl.ds(_row_in_tile(src), 1)],
                                     ybuf.at[k, g, pl.ds(s, 1)], sem)

    def start(g, carry):
        for s in range(SUBLANES):
            for k in range(TOP_K):
                row_copy(g, s, k).start(priority=k % 2)
        return carry

    def wait(g, carry):
        for s in range(SUBLANES):
            for k in range(TOP_K):
                row_copy(g, s, k).wait()
        return carry

    lax.fori_loop(0, tm // SUBLANES, start, 0)
    lax.fori_loop(0, tm // SUBLANES, wait, 0)
    meta = meta_ref[...]
    y = meta[:, 4:5] * ybuf[0].reshape(tm, d) + meta[:, 5:6] * ybuf[1].reshape(tm, d)
    o_ref[...] = x_ref[...] + mod_ref[:, 5 * d:6 * d] * y


def _combine(dest, x, mod, meta, ys, tm):
    bsz, seq, d = x.shape
    nb = seq // tm
    return pl.pallas_call(
        functools.partial(_combine_kernel, tm=tm, d=d),
        grid=(bsz, nb),
        in_specs=[
            pl.BlockSpec((tm * TOP_K,), lambda b, i: (b * nb + i,), memory_space=pltpu.SMEM),
            pl.BlockSpec((None, tm, d), lambda b, i: (b, i, 0)),
            pl.BlockSpec((None, 1, mod.shape[-1]), lambda b, i: (b, 0, 0)),
            pl.BlockSpec((None, tm, 8), lambda b, i: (b, i, 0)),
            pl.BlockSpec(memory_space=pl.ANY),
        ],
        out_specs=pl.BlockSpec((None, tm, d), lambda b, i: (b, i, 0)),
        scratch_shapes=[
            pltpu.VMEM((TOP_K, tm // SUBLANES, SUBLANES, d), F32),
            pltpu.SemaphoreType.DMA(()),
        ],
        out_shape=jax.ShapeDtypeStruct((bsz, seq, d), F32),
        compiler_params=_params("arbitrary", "arbitrary"),
        name="moe_combine",
    )(dest, x, mod, meta, ys.reshape(ys.shape[0] // SUBLANES, SUBLANES, d))


def _hier_moe(x, mod, gain, w_group, b_group, w_expert, b_expert, w_gate, w_up, w_down, layer):
    bsz, seq, d = x.shape
    groups = w_group.shape[1]
    n_exp = w_expert.shape[1]
    per_group = n_exp // groups
    n_tok = bsz * seq
    m = n_tok * TOP_K
    blk = 512
    tm = 512

    pad = LANES - groups - n_exp
    w_router = jnp.concatenate([w_group, w_expert, jnp.zeros((d, pad), F32)], axis=1)
    b_router = jnp.concatenate([b_group, b_expert, jnp.zeros((pad,), F32)]).reshape(1, LANES)
    h, meta, cnt = _router(x, mod, gain, w_router, b_router, groups, per_group, tm)

    counts = cnt[0, groups:groups + n_exp].astype(I32)
    padded = (counts + blk - 1) // blk * blk
    pad_end = jnp.cumsum(padded)
    pad_start = pad_end - padded
    n_blocks = -(-m // blk) + n_exp
    block_expert = jnp.minimum(
        jnp.searchsorted(pad_end, jnp.arange(n_blocks, dtype=I32) * blk, side='right'), n_exp - 1).astype(I32)
    n_used = (pad_end[-1] // blk).astype(I32).reshape(1)
    meta_flat = meta.reshape(n_tok, 8)
    e_idx = meta_flat[:, 0:TOP_K].astype(I32)
    rank = meta_flat[:, TOP_K:2 * TOP_K].astype(I32)
    dest = (pad_start[e_idx] + rank).reshape(m)

    xs = _dispatch(h.reshape(n_tok, d), dest, jnp.zeros((n_blocks * blk, d), F32), tm)
    ys = _experts(xs, block_expert, n_used, w_gate, w_up, w_down, layer, blk)
    return _combine(dest, x, mod, meta, ys, tm)


def kernel(x, c, mod_w, mod_b, norm_mix_g, norm_ffn_g, gdn_w_in, gdn_conv_w, gdn_a_log, gdn_dt_bias, gdn_out_norm_g, gdn_w_out, kv_mod_w, kv_mod_b, kv_norm_g, kv_w, kv_forget_b, k_norm_g, fox_w_qz, fox_q_norm_g, fox_w_out, moe_w_group, moe_b_group, moe_w_expert, moe_b_expert, moe_w_gate, moe_w_up, moe_w_down):
    bsz, seq, d = x.shape
    depth = mod_w.shape[0]
    n_a = gdn_w_in.shape[0]
    gdn_heads = gdn_a_log.shape[1]
    gdn_inner = gdn_heads * GDN_HEAD_DIM
    fox_heads = kv_forget_b.shape[0]
    fox_inner = fox_heads * FOX_HEAD_DIM
    tm = 512

    mod_all = _mod_vectors(c, mod_w, mod_b).reshape(depth, bsz, 1, 6 * d)
    kv_mod = _mod_vectors(c, kv_mod_w[None], kv_mod_b[None]).reshape(bsz, 1, 2 * d)
    bd = _block_diag_ones(fox_inner, FOX_HEAD_DIM)

    ka_sh = vt_sh = ft_sh = None
    for layer in range(depth):
        mod = mod_all[layer]
        gain_mix = norm_mix_g[layer].reshape(1, d)
        if layer < n_a:
            w_in = gdn_w_in[layer]
            qkv, sz, ab = _gdn_in_proj(x, mod, gain_mix, w_in[:, :4 * gdn_inner].astype(BF16),
                                       w_in[:, 4 * gdn_inner:].astype(BF16), gdn_conv_w[layer], gdn_heads, tm)
            o = _gdn_core(qkv, sz, ab, gdn_a_log[layer], gdn_dt_bias[layer],
                          gdn_out_norm_g[layer], gdn_heads, 2 * GDN_CHUNK)
            x = _out_proj_residual(o, x, mod, gdn_w_out[layer].astype(BF16), 2 * d, tm)
        else:
            j = layer - n_a
            wqz = fox_w_qz[j]
            qg = jnp.tile(fox_q_norm_g[j], fox_heads).reshape(1, fox_inner)
            qt, z = _fox_qz_proj(x, mod, gain_mix, wqz[:, :fox_inner].astype(BF16),
                                 wqz[:, fox_inner:].astype(BF16), qg, bd, ft_sh, fox_heads, tm)
            o = _fox_attention(qt, ka_sh, vt_sh, z, 512)
            x = _out_proj_residual(o, x, mod, fox_w_out[j].astype(BF16), 2 * d, tm)
        x = _hier_moe(x, mod, norm_ffn_g[layer].reshape(1, d), moe_w_group[layer], moe_b_group[layer],
                      moe_w_expert[layer], moe_b_expert[layer], moe_w_gate, moe_w_up, moe_w_down, layer)
        if layer == n_a - 1:
            kg = jnp.tile(k_norm_g, fox_heads).reshape(1, fox_inner)
            ka_sh, vt_sh, ft_sh = _shared_kv(
                x, kv_mod, kv_norm_g.reshape(1, d), kv_w[:, :fox_inner].astype(BF16),
                kv_w[:, fox_inner:2 * fox_inner].astype(BF16), kv_w[:, 2 * fox_inner:].astype(BF16),
                kv_forget_b.reshape(1, fox_heads), kg, bd, fox_heads, tm)
    return x
```

```python
import functools

import jax
import jax.numpy as jnp
import numpy as np
from jax import lax
from jax.experimental import pallas as pl
from jax.experimental.pallas import tpu as pltpu

F32 = jnp.float32
BF16 = jnp.bfloat16
I32 = jnp.int32

EPS = 1e-6
GDN_CHUNK = 64
GDN_HEAD_DIM = 128
FOX_HEAD_DIM = 64
TOP_K = 2
LANES = 128
SUBLANES = 8
VMEM_LIMIT = 56 * 1024 * 1024
HIGHEST = lax.Precision.HIGHEST

NT_DIMS = (((1,), (1,)), ((), ()))
TN_DIMS = (((0,), (0,)), ((), ()))


def _params(*sem):
    return pltpu.CompilerParams(dimension_semantics=sem, vmem_limit_bytes=VMEM_LIMIT)


def _sigmoid(x):
    return 1.0 / (1.0 + jnp.exp(-x))


def _silu(x):
    return x * _sigmoid(x)


def _softplus(x):
    return jnp.maximum(x, 0.0) + jnp.log(1.0 + jnp.exp(-jnp.abs(x)))


def _modulated(x, gain, scale, shift):
    ms = jnp.mean(x * x, axis=-1, keepdims=True)
    y = x * lax.rsqrt(ms + EPS)
    return (y * gain) * (1.0 + scale) + shift


def _bdot(a, b, dims=None):
    a = a.astype(BF16)
    b = b.astype(BF16)
    if dims is None:
        return jnp.dot(a, b, preferred_element_type=F32)
    return lax.dot_general(a, b, dims, preferred_element_type=F32)


def _hdot(a, b, dims=None):
    if dims is None:
        return jnp.dot(a, b, preferred_element_type=F32, precision=HIGHEST)
    return lax.dot_general(a, b, dims, preferred_element_type=F32, precision=HIGHEST)


def _tile_of(row):
    return lax.shift_right_logical(row, SUBLANES.bit_length() - 1)


def _row_in_tile(row):
    return row & (SUBLANES - 1)


def _iota2(shape, dim):
    return lax.broadcasted_iota(I32, shape, dim)


def _mod_kernel(c_ref, w_ref, b_ref, o_ref):
    c = c_ref[...]
    o_ref[...] = _hdot(_silu(c), w_ref[...]) + b_ref[...]


def _mod_vectors(c, w, b):
    n_layers, d, n = w.shape
    bsz = c.shape[0]
    tn = 1536 if n % 1536 == 0 else n
    return pl.pallas_call(
        _mod_kernel,
        grid=(n_layers, n // tn),
        in_specs=[
            pl.BlockSpec((bsz, d), lambda l, j: (0, 0)),
            pl.BlockSpec((None, d, tn), lambda l, j: (l, 0, j)),
            pl.BlockSpec((None, 1, tn), lambda l, j: (l, 0, j)),
        ],
        out_specs=pl.BlockSpec((None, bsz, tn), lambda l, j: (l, 0, j)),
        out_shape=jax.ShapeDtypeStruct((n_layers, bsz, n), F32),
        compiler_params=_params("parallel", "parallel"),
        name="mod_vectors",
    )(c, w, b.reshape(n_layers, 1, n))


def _gdn_in_kernel(x_ref, mod_ref, g_ref, w_ref, wab_ref, cw_ref, qkv_ref, sz_ref, oab_ref, xpad_ref,
                   *, d, tm, heads, dh, width):
    inner = heads * dh
    halo = 8
    i = pl.program_id(1)

    @pl.when(i == 0)
    def _():
        xpad_ref[0:halo, :] = jnp.zeros((halo, 3 * inner), F32)

    h = _modulated(x_ref[...], g_ref[...], mod_ref[:, d:2 * d], mod_ref[:, 0:d]).astype(BF16)
    oab_ref[...] = jnp.dot(h, wab_ref[...], preferred_element_type=F32)
    z = jnp.dot(h, w_ref[:, 3 * inner:4 * inner], preferred_element_type=F32)
    sz_ref[...] = _silu(z).astype(sz_ref.dtype)
    xpad_ref[halo:halo + tm, :] = jnp.dot(h, w_ref[:, 0:3 * inner], preferred_element_type=F32)
    for j in range(3 * heads):
        cols = slice(j * dh, (j + 1) * dh)
        acc = xpad_ref[halo:halo + tm, cols] * cw_ref[width - 1:width, cols]
        for s in range(1, width):
            acc = acc + xpad_ref[halo - s:halo - s + tm, cols] * cw_ref[width - 1 - s:width - s, cols]
        y = _silu(acc)
        if j < 2 * heads:
            y = y * lax.rsqrt(jnp.sum(y * y, axis=-1, keepdims=True) + EPS)
        if j < heads:
            y = y * (dh ** -0.5)
        qkv_ref[:, cols] = y.astype(qkv_ref.dtype)
    xpad_ref[0:halo, :] = xpad_ref[tm:tm + halo, :]


def _gdn_in_proj(x, mod, gain, w_main, w_ab, conv_w, heads, tm):
    bsz, seq, d = x.shape
    dh = GDN_HEAD_DIM
    inner = heads * dh
    nab = w_ab.shape[1]
    width = conv_w.shape[0]
    return pl.pallas_call(
        functools.partial(_gdn_in_kernel, d=d, tm=tm, heads=heads, dh=dh, width=width),
        grid=(bsz, seq // tm),
        in_specs=[
            pl.BlockSpec((None, tm, d), lambda b, i: (b, i, 0)),
            pl.BlockSpec((None, 1, mod.shape[-1]), lambda b, i: (b, 0, 0)),
            pl.BlockSpec((1, d), lambda b, i: (0, 0)),
            pl.BlockSpec((d, 4 * inner), lambda b, i: (0, 0)),
            pl.BlockSpec((d, nab), lambda b, i: (0, 0)),
            pl.BlockSpec((width, 3 * inner), lambda b, i: (0, 0)),
        ],
        out_specs=[
            pl.BlockSpec((None, tm, 3 * inner), lambda b, i: (b, i, 0)),
            pl.BlockSpec((None, tm, inner), lambda b, i: (b, i, 0)),
            pl.BlockSpec((None, tm, nab), lambda b, i: (b, i, 0)),
        ],
        out_shape=[
            jax.ShapeDtypeStruct((bsz, seq, 3 * inner), BF16),
            jax.ShapeDtypeStruct((bsz, seq, inner), BF16),
            jax.ShapeDtypeStruct((bsz, seq, nab), F32),
        ],
        scratch_shapes=[pltpu.VMEM((8 + tm, 3 * inner), F32)],
        compiler_params=_params("parallel", "arbitrary"),
        name="gdn_in_proj",
    )(x, mod, gain, w_main, w_ab, conv_w)


def _gdn_kernel(q_ref, k_ref, v_ref, sz_ref, ab_ref, alog_ref, dtb_ref, og_ref, o_ref, state_ref,
                *, tb, heads, dh):
    c = GDN_CHUNK
    pairs = heads // 2
    i = pl.program_id(1)

    @pl.when(i == 0)
    def _():
        state_ref[...] = jnp.zeros(state_ref.shape, F32)

    ab = ab_ref[...]
    beta_all = _sigmoid(ab[:, heads:2 * heads])
    g_all = -jnp.exp(alog_ref[...]) * _softplus(ab[:, 0:heads] + dtb_ref[...])
    og = og_ref[...]

    row = _iota2((c, 2 * c), 0)
    lane = _iota2((c, 2 * c), 1)
    first = lane < c
    col = jnp.where(first, lane, lane - c)
    incl = row >= col
    strict = row > col
    eye_p = jnp.where(row == col, 1.0, 0.0).astype(F32)
    tri = jnp.where(_iota2((c, c), 0) >= _iota2((c, c), 1), 1.0, 0.0).astype(F32)
    bd_small = (_iota2((2 * c, 2 * c), 0) < c) == (_iota2((2 * c, 2 * c), 1) < c)
    bd_wide = (_iota2((2 * c, 2 * dh), 0) < c) == (_iota2((2 * c, 2 * dh), 1) < dh)
    first_h = _iota2((pairs, 2 * c), 1) < c
    sel0 = jnp.where(_iota2((pairs, heads), 1) == 2 * _iota2((pairs, heads), 0), 1.0, 0.0).astype(F32)
    sel1 = jnp.where(_iota2((pairs, heads), 1) == 2 * _iota2((pairs, heads), 0) + 1, 1.0, 0.0).astype(F32)

    def block_diag(x, mask):
        return jnp.where(mask, jnp.concatenate([x, x], axis=0), 0.0).astype(BF16)

    def pair_cols(x, p):
        return jnp.concatenate([jnp.broadcast_to(x[:, 2 * p:2 * p + 1], (c, dh)),
                                jnp.broadcast_to(x[:, 2 * p + 1:2 * p + 2], (c, dh))], axis=1)

    units = [(ci, p) for ci in range(tb // c) for p in range(pairs)]
    gcums, glasts, pk, tt, attn, w_u, qg, kdec = {}, {}, {}, {}, {}, {}, {}, {}

    for ci in range(tb // c):
        r0 = ci * c
        gcum = _hdot(tri, g_all[r0:r0 + c, :])
        gc2 = jnp.concatenate([gcum, gcum], axis=0)
        gt = jnp.where(first_h, _hdot(sel0, gc2, NT_DIMS), _hdot(sel1, gc2, NT_DIMS))
        glast = gcum[c - 1:c, :]
        gcums[ci] = gcum
        glasts[ci] = glast
        for p in range(pairs):
            u = (ci, p)
            cols = slice(2 * p * dh, (2 * p + 2) * dh)
            kp = k_ref[r0:r0 + c, cols].astype(F32)
            qp = q_ref[r0:r0 + c, cols].astype(F32)
            vp = v_ref[r0:r0 + c, cols].astype(F32)
            gcol = jnp.where(first, gcum[:, 2 * p:2 * p + 1], gcum[:, 2 * p + 1:2 * p + 2])
            decay = jnp.exp(jnp.where(incl, gcol - gt[p:p + 1, :], -jnp.inf))
            beta2 = pair_cols(beta_all[r0:r0 + c, :], p)
            gcum2 = pair_cols(gcum, p)
            egc2 = jnp.exp(gcum2)
            kb = kp * beta2
            y = block_diag(kp, bd_wide)
            kq = _bdot(jnp.concatenate([kb, qp], axis=0), y, NT_DIMS)
            a_mat = jnp.where(strict, kq[0:c, :] * decay, 0.0)
            attn[u] = kq[c:2 * c, :] * decay
            pk[u] = -a_mat
            tt[u] = eye_p - a_mat
            kbg = kb * egc2
            vb = vp * beta2
            w_u[u] = jnp.concatenate(
                [jnp.concatenate([kbg[:, 0:dh], vb[:, 0:dh]], axis=1),
                 jnp.concatenate([kbg[:, dh:2 * dh], vb[:, dh:2 * dh]], axis=1)], axis=0).astype(BF16)
            qg[u] = qp * egc2
            kdec[u] = kp * jnp.exp(pair_cols(jnp.broadcast_to(glast, (c, heads)), p) - gcum2)

    span = 2
    while span <= c:
        last = span == c
        for u in units:
            bd = block_diag(pk[u], bd_small)
            if span == 2:
                pk[u] = _bdot(pk[u], bd)
            elif last:
                tt[u] = tt[u] + _bdot(tt[u], bd)
            else:
                both = _bdot(jnp.concatenate([pk[u], tt[u]], axis=0), bd)
                pk[u] = both[0:c, :]
                tt[u] = tt[u] + both[c:2 * c, :]
        span *= 2
    for u in units:
        t = tt[u]
        lhs = jnp.concatenate([jnp.where(first, t, 0.0), jnp.where(first, 0.0, t)], axis=0)
        w_u[u] = _bdot(lhs, w_u[u])

    for ci in range(tb // c):
        r0 = ci * c
        for p in range(pairs):
            u = (ci, p)
            wu = w_u[u]
            st, wq, vn = [], [], []
            for s in range(2):
                h = 2 * p + s
                st.append(state_ref[h])
                wq.append(_bdot(jnp.concatenate([wu[s * c:(s + 1) * c, 0:dh], qg[u][:, s * dh:(s + 1) * dh]], axis=0),
                                st[s]))
                vn.append(wu[s * c:(s + 1) * c, dh:2 * dh] - wq[s][0:c, :])
            vn2 = jnp.concatenate(vn, axis=0).astype(BF16)
            for s in range(2):
                h = 2 * p + s
                am = jnp.where(first, attn[u], 0.0) if s == 0 else jnp.where(first, 0.0, attn[u])
                o = wq[s][c:2 * c, :] + _bdot(am, vn2)
                gl = glasts[ci][:, h:h + 1]
                state_ref[h] = st[s] * jnp.exp(gl) + _bdot(kdec[u][:, s * dh:(s + 1) * dh], vn[s], TN_DIMS)
                on = o * lax.rsqrt(jnp.mean(o * o, axis=-1, keepdims=True) + EPS) * og
                szh = sz_ref[r0:r0 + c, h * dh:(h + 1) * dh].astype(F32)
                o_ref[r0:r0 + c, h * dh:(h + 1) * dh] = (on * szh).astype(o_ref.dtype)


def _gdn_core(qkv, sz, ab, a_log, dt_bias, out_g, heads, tb):
    bsz, seq, _ = qkv.shape
    dh = GDN_HEAD_DIM
    inner = heads * dh
    return pl.pallas_call(
        functools.partial(_gdn_kernel, tb=tb, heads=heads, dh=dh),
        grid=(bsz, seq // tb),
        in_specs=[
            pl.BlockSpec((None, tb, inner), lambda b, i: (b, i, 0)),
            pl.BlockSpec((None, tb, inner), lambda b, i: (b, i, 1)),
            pl.BlockSpec((None, tb, inner), lambda b, i: (b, i, 2)),
            pl.BlockSpec((None, tb, inner), lambda b, i: (b, i, 0)),
            pl.BlockSpec((None, tb, 2 * heads), lambda b, i: (b, i, 0)),
            pl.BlockSpec((1, heads), lambda b, i: (0, 0)),
            pl.BlockSpec((1, heads), lambda b, i: (0, 0)),
            pl.BlockSpec((1, dh), lambda b, i: (0, 0)),
        ],
        out_specs=pl.BlockSpec((None, tb, inner), lambda b, i: (b, i, 0)),
        out_shape=jax.ShapeDtypeStruct((bsz, seq, inner), BF16),
        scratch_shapes=[pltpu.VMEM((heads, dh, dh), F32)],
        compiler_params=_params("parallel", "arbitrary"),
        name="gdn_core",
    )(qkv, qkv, qkv, sz, ab, a_log.reshape(1, heads), dt_bias.reshape(1, heads), out_g.reshape(1, dh))


def _out_proj_kernel(o_ref, x_ref, mod_ref, w_ref, y_ref, *, d, gate_off):
    y = jnp.dot(o_ref[...], w_ref[...], preferred_element_type=F32)
    y_ref[...] = x_ref[...] + mod_ref[:, gate_off:gate_off + d] * y


def _out_proj_residual(o, x, mod, w, gate_off, tm):
    bsz, seq, d = x.shape
    k = o.shape[-1]
    return pl.pallas_call(
        functools.partial(_out_proj_kernel, d=d, gate_off=gate_off),
        grid=(bsz, seq // tm),
        in_specs=[
            pl.BlockSpec((None, tm, k), lambda b, i: (b, i, 0)),
            pl.BlockSpec((None, tm, d), lambda b, i: (b, i, 0)),
            pl.BlockSpec((None, 1, mod.shape[-1]), lambda b, i: (b, 0, 0)),
            pl.BlockSpec((k, d), lambda b, i: (0, 0)),
        ],
        out_specs=pl.BlockSpec((None, tm, d), lambda b, i: (b, i, 0)),
        out_shape=jax.ShapeDtypeStruct((bsz, seq, d), F32),
        compiler_params=_params("parallel", "parallel"),
        name="out_proj_residual",
    )(o, x, mod, w)


def _head_rms(x, bd, gain, dh):
    ss = jnp.dot((x * x).astype(BF16), bd, preferred_element_type=F32) * (1.0 / dh)
    return x * lax.rsqrt(ss + EPS) * gain


def _block_diag_ones(n, blk):
    r = jnp.arange(n, dtype=I32) // blk
    return (r[:, None] == r[None, :]).astype(BF16)


N_SPLIT = 3


def _bias_base(head, dh):
    return dh if head % 2 == 0 else 0


def _split3(x):
    hi = x.astype(BF16)
    r = x - hi.astype(F32)
    mid = r.astype(BF16)
    lo = (r - mid.astype(F32)).astype(BF16)
    return hi, mid, lo


def _key_bias_placement(heads, dh):
    place = np.zeros((N_SPLIT, heads, heads * 2 * dh), np.float32)
    for j in range(N_SPLIT):
        for h in range(heads):
            place[j, h, h * 2 * dh + _bias_base(h, dh) + j] = 1.0
    return jnp.asarray(place, BF16)


def _kv_kernel(x_ref, mod_ref, g_ref, wk_ref, wv_ref, wf_ref, fb_ref, kg_ref, bd_ref, place_ref,
               ka_ref, vt_ref, ft_ref, carry_ref, *, d, tm, heads, dh):
    i = pl.program_id(1)

    @pl.when(i == 0)
    def _():
        carry_ref[...] = jnp.zeros(carry_ref.shape, F32)

    h = _modulated(x_ref[...], g_ref[...], mod_ref[:, d:2 * d], mod_ref[:, 0:d]).astype(BF16)
    kraw = jnp.dot(h, wk_ref[...], preferred_element_type=F32)
    kn = _head_rms(kraw, bd_ref[...], kg_ref[...], dh)
    v = jnp.dot(h, wv_ref[...], preferred_element_type=F32).astype(BF16)
    f = jnp.dot(h, wf_ref[...], preferred_element_type=F32) + fb_ref[...]
    log_f = -_softplus(-f)
    tri = jnp.where(_iota2((tm, tm), 0) >= _iota2((tm, tm), 1), 1.0, 0.0).astype(F32)
    fcum = _hdot(tri, log_f) + carry_ref[...]
    carry_ref[...] = fcum[tm - 1:tm, :]
    eye_h = jnp.where(_iota2((heads, heads), 0) == _iota2((heads, heads), 1), 1.0, 0.0).astype(F32)
    fcum_t = _hdot(eye_h, fcum, NT_DIMS)
    for p in range(heads // 2):
        ft_ref[p] = fcum_t[2 * p:2 * p + 2, :]

    pieces = _split3(-fcum)
    bias = jnp.dot(pieces[0], place_ref[0], preferred_element_type=F32)
    for j in range(1, N_SPLIT):
        bias = bias + jnp.dot(pieces[j], place_ref[j], preferred_element_type=F32)
    lane = _iota2((tm, 2 * dh), 1)
    for hh in range(heads):
        p = hh // 2
        base = _bias_base(hh, dh)
        real = (lane < dh) if hh % 2 == 0 else (lane >= dh)
        ones = (lane >= base + N_SPLIT) & (lane < base + 2 * N_SPLIT)
        blk = jnp.where(real, kn[:, 2 * p * dh:(2 * p + 2) * dh],
                        jnp.where(ones, 1.0, bias[:, hh * 2 * dh:(hh + 1) * 2 * dh]))
        ka_ref[hh] = blk.astype(ka_ref.dtype)

    eye = jnp.where(_iota2((2 * dh, 2 * dh), 0) == _iota2((2 * dh, 2 * dh), 1), 1.0, 0.0).astype(BF16)
    rowi = _iota2((2 * dh, tm), 0)
    for p in range(heads // 2):
        vt = lax.dot_general(eye, v[:, 2 * p * dh:(2 * p + 2) * dh], NT_DIMS, preferred_element_type=F32)
        vt_ref[2 * p] = jnp.where(rowi < dh, vt, 1.0).astype(vt_ref.dtype)
        vt_ref[2 * p + 1] = jnp.where(rowi >= dh, vt, 1.0).astype(vt_ref.dtype)


def _shared_kv(x, mod, gain, wk, wv, wf, fb, kgain, bd, heads, tm):
    bsz, seq, d = x.shape
    dh = FOX_HEAD_DIM
    inner = heads * dh
    pairs = heads // 2
    place = _key_bias_placement(heads, dh)
    return pl.pallas_call(
        functools.partial(_kv_kernel, d=d, tm=tm, heads=heads, dh=dh),
        grid=(bsz, seq // tm),
        in_specs=[
            pl.BlockSpec((None, tm, d), lambda b, i: (b, i, 0)),
            pl.BlockSpec((None, 1, mod.shape[-1]), lambda b, i: (b, 0, 0)),
            pl.BlockSpec((1, d), lambda b, i: (0, 0)),
            pl.BlockSpec((d, inner), lambda b, i: (0, 0)),
            pl.BlockSpec((d, inner), lambda b, i: (0, 0)),
            pl.BlockSpec((d, heads), lambda b, i: (0, 0)),
            pl.BlockSpec((1, heads), lambda b, i: (0, 0)),
            pl.BlockSpec((1, inner), lambda b, i: (0, 0)),
            pl.BlockSpec((inner, inner), lambda b, i: (0, 0)),
            pl.BlockSpec(place.shape, lambda b, i: (0, 0, 0)),
        ],
        out_specs=[
            pl.BlockSpec((None, heads, tm, 2 * dh), lambda b, i: (b, 0, i, 0)),
            pl.BlockSpec((None, heads, 2 * dh, tm), lambda b, i: (b, 0, 0, i)),
            pl.BlockSpec((None, pairs, 2, tm), lambda b, i: (b, 0, 0, i)),
        ],
        out_shape=[
            jax.ShapeDtypeStruct((bsz, heads, seq, 2 * dh), BF16),
            jax.ShapeDtypeStruct((bsz, heads, 2 * dh, seq), BF16),
            jax.ShapeDtypeStruct((bsz, pairs, 2, seq), F32),
        ],
        scratch_shapes=[pltpu.VMEM((1, heads), F32)],
        compiler_params=_params("parallel", "arbitrary"),
        name="shared_kv",
    )(x, mod, gain, wk, wv, wf, fb, kgain, bd, place)


def _fox_qz_kernel(x_ref, mod_ref, g_ref, wq_ref, wz_ref, qg_ref, bd_ref, ft_ref, qt_ref, z_ref,
                   *, d, dh, heads, tm):
    h = _modulated(x_ref[...], g_ref[...], mod_ref[:, d:2 * d], mod_ref[:, 0:d]).astype(BF16)
    qraw = jnp.dot(h, wq_ref[...], preferred_element_type=F32)
    qn = (_head_rms(qraw, bd_ref[...], qg_ref[...], dh) * (dh ** -0.5)).astype(BF16)
    z_ref[...] = jnp.dot(h, wz_ref[...], preferred_element_type=F32).astype(z_ref.dtype)
    eye = jnp.where(_iota2((2 * dh, 2 * dh), 0) == _iota2((2 * dh, 2 * dh), 1), 1.0, 0.0).astype(BF16)
    rowi = _iota2((2 * dh, tm), 0)
    for p in range(heads // 2):
        qt = lax.dot_general(eye, qn[:, 2 * p * dh:(2 * p + 2) * dh], NT_DIMS, preferred_element_type=F32)
        for s in range(2):
            hh = 2 * p + s
            base = _bias_base(hh, dh)
            real = (rowi < dh) if s == 0 else (rowi >= dh)
            aug = jnp.where(real, qt, 0.0)
            aug = jnp.where((rowi >= base) & (rowi < base + N_SPLIT), 1.0, aug)
            pieces = _split3(ft_ref[p, s:s + 1, :])
            for j in range(N_SPLIT):
                aug = jnp.where(rowi == base + N_SPLIT + j, pieces[j].astype(F32), aug)
            qt_ref[hh] = aug.astype(qt_ref.dtype)


def _fox_qz_proj(x, mod, gain, wq, wz, qgain, bd, ft, heads, tm):
    bsz, seq, d = x.shape
    dh = FOX_HEAD_DIM
    inner = wq.shape[1]
    pairs = heads // 2
    return pl.pallas_call(
        functools.partial(_fox_qz_kernel, d=d, dh=dh, heads=heads, tm=tm),
        grid=(bsz, seq // tm),
        in_specs=[
            pl.BlockSpec((None, tm, d), lambda b, i: (b, i, 0)),
            pl.BlockSpec((None, 1, mod.shape[-1]), lambda b, i: (b, 0, 0)),
            pl.BlockSpec((1, d), lambda b, i: (0, 0)),
            pl.BlockSpec((d, inner), lambda b, i: (0, 0)),
            pl.BlockSpec((d, inner), lambda b, i: (0, 0)),
            pl.BlockSpec((1, inner), lambda b, i: (0, 0)),
            pl.BlockSpec((inner, inner), lambda b, i: (0, 0)),
            pl.BlockSpec((None, pairs, 2, tm), lambda b, i: (b, 0, 0, i)),
        ],
        out_specs=[
            pl.BlockSpec((None, heads, 2 * dh, tm), lambda b, i: (b, 0, 0, i)),
            pl.BlockSpec((None, tm, inner), lambda b, i: (b, i, 0)),
        ],
        out_shape=[
            jax.ShapeDtypeStruct((bsz, heads, 2 * dh, seq), BF16),
            jax.ShapeDtypeStruct((bsz, seq, inner), BF16),
        ],
        compiler_params=_params("parallel", "parallel"),
        name="fox_qz_proj",
    )(x, mod, gain, wq, wz, qgain, bd, ft)


def _fox_kernel(qt_ref, ka_ref, vt_ref, z_ref, o_ref, s_ref, p_ref, acc_ref, *, tq, dh):
    i = pl.program_id(2)
    acc_ref[...] = jnp.zeros(acc_ref.shape, F32)
    key_idx = _iota2((tq, LANES), 0)
    qry_idx = _iota2((tq, LANES), 1)

    wide = 2 * LANES
    units = [(h, c) for h in range(2) for c in range(tq // wide)]

    def scores(kb, slot, u):
        h, c = u
        start = pl.multiple_of(kb * tq, tq)
        s = jnp.dot(ka_ref[h, pl.ds(start, tq), :], qt_ref[h, :, c * wide:(c + 1) * wide],
                    preferred_element_type=F32)
        for t in range(wide // LANES):
            s_ref[slot, h, c * (wide // LANES) + t] = s[:, t * LANES:(t + 1) * LANES]

    def softmax_pv(kb, slot, u, ms, masked, m_out):
        h, c = u
        start = pl.multiple_of(kb * tq, tq)
        alphas = []
        for t in range(wide // LANES):
            q0 = c * wide + t * LANES
            cols = slice(q0, q0 + LANES)
            s = s_ref[slot, h, q0 // LANES]
            if masked:
                s = jnp.where(key_idx <= qry_idx + q0, s, -jnp.inf)
            m_old = ms[h][:, cols]
            parts = [s[r:r + 64, :] for r in range(0, tq, 64)]
            while len(parts) > 1:
                parts = [jnp.maximum(parts[n], parts[n + 1]) for n in range(0, len(parts), 2)]
            m_new = jnp.maximum(m_old, jnp.max(parts[0], axis=0, keepdims=True))
            p_ref[h, q0 // LANES] = jnp.exp((s - m_new).astype(BF16))
            alphas.append(jnp.exp(m_old - m_new))
            m_out[(h, q0)] = m_new
        cols = slice(c * wide, (c + 1) * wide)
        alpha = jnp.concatenate(alphas, axis=1)
        p = jnp.concatenate([p_ref[h, c * (wide // LANES) + t] for t in range(wide // LANES)], axis=1)
        acc_ref[h, :, cols] = alpha * acc_ref[h, :, cols] + jnp.dot(
            vt_ref[h, :, pl.ds(start, tq)], p, preferred_element_type=F32)

    def step(kb, slot, ms, masked, prefetch):
        m_out = {}
        if prefetch:
            for u in units:
                scores(kb + 1, 1 - slot, u)
        for u in units:
            softmax_pv(kb, slot, u, ms, masked, m_out)
        return tuple(jnp.concatenate([m_out[(h, q0)] for q0 in range(0, tq, LANES)], axis=1) for h in range(2))

    def by_parity(kb, ms, masked, prefetch):
        return lax.cond(kb % 2 == 0,
                        lambda ms: step(kb, 0, ms, masked, prefetch),
                        lambda ms: step(kb, 1, ms, masked, prefetch), ms)

    for u in units:
        scores(0, 0, u)
    neg = jnp.full((1, tq), -jnp.inf, F32)
    ms = lax.fori_loop(0, i, lambda kb, ms: by_parity(kb, ms, False, True), (neg, neg))
    by_parity(i, ms, True, False)
    a0 = acc_ref[0]
    a1 = acc_ref[1]
    num = jnp.concatenate([a0[0:dh, :], a1[dh:2 * dh, :]], axis=0)
    den = jnp.concatenate([a0[dh:2 * dh, :], a1[0:dh, :]], axis=0)
    o = jnp.transpose(num / den)
    o_ref[...] = (o * _sigmoid(z_ref[...].astype(F32))).astype(o_ref.dtype)


def _fox_attention(qt, ka, vt, z, tq):
    bsz, heads, feat, seq = qt.shape
    dh = FOX_HEAD_DIM
    pairs = heads // 2
    return pl.pallas_call(
        functools.partial(_fox_kernel, tq=tq, dh=dh),
        grid=(bsz, pairs, seq // tq),
        in_specs=[
            pl.BlockSpec((None, 2, feat, tq), lambda b, p, i: (b, p, 0, i)),
            pl.BlockSpec((None, 2, seq, feat), lambda b, p, i: (b, p, 0, 0)),
            pl.BlockSpec((None, 2, feat, seq), lambda b, p, i: (b, p, 0, 0)),
            pl.BlockSpec((None, tq, 2 * dh), lambda b, p, i: (b, i, p)),
        ],
        out_specs=pl.BlockSpec((None, tq, 2 * dh), lambda b, p, i: (b, i, p)),
        out_shape=jax.ShapeDtypeStruct((bsz, seq, heads * dh), BF16),
        scratch_shapes=[
            pltpu.VMEM((2, 2, tq // LANES, tq, LANES), F32),
            pltpu.VMEM((2, tq // LANES, tq, LANES), BF16),
            pltpu.VMEM((2, feat, tq), F32),
        ],
        compiler_params=_params("parallel", "parallel", "parallel"),
        name="fox_attention",
    )(qt, ka, vt, z)


def _router_kernel(x_ref, mod_ref, g_ref, wr_ref, br_ref, h_ref, meta_ref, cnt_ref, carry_ref,
                   *, d, tm, groups, per_group):
    step = pl.program_id(0) * pl.num_programs(1) + pl.program_id(1)

    @pl.when(step == 0)
    def _():
        carry_ref[...] = jnp.zeros(carry_ref.shape, F32)

    h = _modulated(x_ref[...], g_ref[...], mod_ref[:, 4 * d:5 * d], mod_ref[:, 3 * d:4 * d])
    h_ref[...] = h.reshape(h_ref.shape)
    logits = _hdot(h, wr_ref[...]) + br_ref[...]
    lane = _iota2((tm, LANES), 1)
    neg_inf = jnp.float32(-jnp.inf)
    big = jnp.int32(LANES)

    def first_argmax(vals, mask):
        mv = jnp.where(mask, vals, neg_inf)
        mx = jnp.max(mv, axis=-1, keepdims=True)
        idx = jnp.min(jnp.where(mask & (mv == mx), lane, big), axis=-1, keepdims=True)
        return mx, idx

    gmask = lane < groups
    gmax, gidx = first_argmax(logits, gmask)
    gsum = jnp.sum(jnp.where(gmask, jnp.exp(logits - gmax), 0.0), axis=-1, keepdims=True)
    group_gate = 1.0 / gsum
    lo = groups + gidx * per_group
    emask = (lane >= lo) & (lane < lo + per_group)
    e1max, e1lane = first_argmax(logits, emask)
    e2max, e2lane = first_argmax(logits, emask & (lane != e1lane))
    esum = jnp.sum(jnp.where(emask, jnp.exp(logits - e1max), 0.0), axis=-1, keepdims=True)
    p1 = 1.0 / esum
    p2 = jnp.exp(e2max - e1max) / esum
    psum = p1 + p2
    w1 = group_gate * (p1 / psum)
    w2 = group_gate * (p2 / psum)

    oh1 = (lane == e1lane).astype(F32)
    oh2 = (lane == e2lane).astype(F32)
    strict = jnp.where(_iota2((tm, tm), 0) > _iota2((tm, tm), 1), 1.0, 0.0).astype(BF16)
    c1 = jnp.dot(strict, oh1.astype(BF16), preferred_element_type=F32)
    c2 = jnp.dot(strict, oh2.astype(BF16), preferred_element_type=F32)
    tot1 = jnp.sum(oh1, axis=0, keepdims=True)
    tot2 = jnp.sum(oh2, axis=0, keepdims=True)
    carry = carry_ref[...]
    rank1 = jnp.sum(oh1 * (c1 + carry), axis=-1, keepdims=True)
    rank2 = jnp.sum(oh2 * (c2 + carry + tot1), axis=-1, keepdims=True)
    carry = carry + tot1 + tot2
    carry_ref[...] = carry
    cnt_ref[...] = carry

    e1 = (e1lane - groups).astype(F32)
    e2 = (e2lane - groups).astype(F32)
    mlane = _iota2((tm, 8), 1)
    meta = jnp.where(mlane == 0, e1, 0.0)
    meta = jnp.where(mlane == 1, e2, meta)
    meta = jnp.where(mlane == 2, rank1, meta)
    meta = jnp.where(mlane == 3, rank2, meta)
    meta = jnp.where(mlane == 4, w1, meta)
    meta = jnp.where(mlane == 5, w2, meta)
    meta_ref[...] = meta


def _router(x, mod, gain, w_router, b_router, groups, per_group, tm):
    bsz, seq, d = x.shape
    nb = seq // tm
    return pl.pallas_call(
        functools.partial(_router_kernel, d=d, tm=tm, groups=groups, per_group=per_group),
        grid=(bsz, seq // tm),
        in_specs=[
            pl.BlockSpec((None, tm, d), lambda b, i: (b, i, 0)),
            pl.BlockSpec((None, 1, mod.shape[-1]), lambda b, i: (b, 0, 0)),
            pl.BlockSpec((1, d), lambda b, i: (0, 0)),
            pl.BlockSpec((d, LANES), lambda b, i: (0, 0)),
            pl.BlockSpec((1, LANES), lambda b, i: (0, 0)),
        ],
        out_specs=[
            pl.BlockSpec((tm, SUBLANES, d // SUBLANES), lambda b, i: (b * nb + i, 0, 0)),
            pl.BlockSpec((None, tm, 8), lambda b, i: (b, i, 0)),
            pl.BlockSpec((1, LANES), lambda b, i: (0, 0)),
        ],
        out_shape=[
            jax.ShapeDtypeStruct((bsz * seq, SUBLANES, d // SUBLANES), F32),
            jax.ShapeDtypeStruct((bsz, seq, 8), F32),
            jax.ShapeDtypeStruct((1, LANES), F32),
        ],
        scratch_shapes=[pltpu.VMEM((1, LANES), F32)],
        compiler_params=_params("arbitrary", "arbitrary"),
        name="moe_router",
    )(x, mod, gain, w_router, b_router)


DMA_GROUP = 16


def _dispatch_kernel(pad_ref, dest_ref, h_ref, xs_ref, zero_ref, sem, zsem, *, tm, blk, n_exp):
    @pl.when(pl.program_id(0) == 0)
    def _():
        zero_ref[...] = jnp.zeros(zero_ref.shape, zero_ref.dtype)

        def zero_copy(e):
            return pltpu.make_async_copy(zero_ref, xs_ref.at[pl.ds(pad_ref[e] - blk, blk)], zsem)

        def start(e, carry):
            @pl.when(pad_ref[n_exp + e] > 0)
            def _():
                zero_copy(e).start()
            return carry

        def wait(e, carry):
            @pl.when(pad_ref[n_exp + e] > 0)
            def _():
                zero_copy(e).wait()
            return carry

        lax.fori_loop(0, n_exp, start, 0)
        lax.fori_loop(0, n_exp, wait, 0)

        def tail_copy(j):
            return pltpu.make_async_copy(zero_ref, xs_ref.at[pl.ds(j * blk, blk)], zsem)

        def tail_start(j, carry):
            tail_copy(j).start()
            return carry

        def tail_wait(j, carry):
            tail_copy(j).wait()
            return carry

        n_used = pad_ref[2 * n_exp]
        lax.fori_loop(n_used, xs_ref.shape[0] // blk, tail_start, 0)
        lax.fori_loop(n_used, xs_ref.shape[0] // blk, tail_wait, 0)

    def row_copy(r, k):
        return pltpu.make_async_copy(h_ref.at[r], xs_ref.at[dest_ref[r * TOP_K + k]], sem)

    def start(g, carry):
        for s in range(DMA_GROUP):
            for k in range(TOP_K):
                row_copy(g * DMA_GROUP + s, k).start(priority=k % 2)
        return carry

    def wait(g, carry):
        for s in range(DMA_GROUP):
            for k in range(TOP_K):
                row_copy(g * DMA_GROUP + s, k).wait()
        return carry

    lax.fori_loop(0, tm // DMA_GROUP, start, 0)
    lax.fori_loop(0, tm // DMA_GROUP, wait, 0)


def _dispatch(h_tiles, dest, pad_info, rows, tm, blk, n_exp):
    n_tok = h_tiles.shape[0]
    tile = h_tiles.shape[1:]
    return pl.pallas_call(
        functools.partial(_dispatch_kernel, tm=tm, blk=blk, n_exp=n_exp),
        grid_spec=pltpu.PrefetchScalarGridSpec(
            num_scalar_prefetch=1,
            grid=(n_tok // tm,),
            in_specs=[
                pl.BlockSpec((tm * TOP_K,), lambda i, pad: (i,), memory_space=pltpu.SMEM),
                pl.BlockSpec((tm,) + tile, lambda i, pad: (i, 0, 0)),
            ],
            out_specs=pl.BlockSpec(memory_space=pl.ANY),
            scratch_shapes=[
                pltpu.VMEM((blk,) + tile, F32),
                pltpu.SemaphoreType.DMA(()),
                pltpu.SemaphoreType.DMA(()),
            ],
        ),
        out_shape=jax.ShapeDtypeStruct((rows,) + tile, F32),
        compiler_params=_params("arbitrary"),
        name="moe_dispatch",
    )(pad_info, dest, h_tiles)


def _expert_kernel(be_ref, nu_ref, xs_ref, wg_ref, wu_ref, wd_ref, ys_ref, wgb, wub, wdb):
    i = pl.program_id(0)
    prev = be_ref[jnp.maximum(i - 1, 0)]
    fresh = (i == 0) | (be_ref[i] != prev)

    @pl.when(fresh)
    def _():
        wgb[...] = wg_ref[...].astype(BF16)
        wub[...] = wu_ref[...].astype(BF16)
        wdb[...] = wd_ref[...].astype(BF16)

    @pl.when(i < nu_ref[0])
    def _():
        blk = xs_ref.shape[0]
        x = xs_ref[...].reshape(blk, wgb.shape[0]).astype(BF16)
        g = jnp.dot(x, wgb[...], preferred_element_type=F32)
        u = jnp.dot(x, wub[...], preferred_element_type=F32)
        mid = (_silu(g) * u).astype(BF16)
        ys_ref[...] = jnp.dot(mid, wdb[...], preferred_element_type=F32).reshape(ys_ref.shape)

    @pl.when(i >= nu_ref[0])
    def _():
        ys_ref[...] = jnp.zeros(ys_ref.shape, ys_ref.dtype)


def _experts(xs, block_expert, n_used, w_gate, w_up, w_down, layer, blk):
    rows = xs.shape[0]
    tile = xs.shape[1:]
    d, de = w_gate.shape[-2:]
    n_blocks = rows // blk
    return pl.pallas_call(
        _expert_kernel,
        grid_spec=pltpu.PrefetchScalarGridSpec(
            num_scalar_prefetch=2,
            grid=(n_blocks,),
            in_specs=[
                pl.BlockSpec((blk,) + tile, lambda i, be, nu: (jnp.minimum(i, nu[0] - 1), 0, 0)),
                pl.BlockSpec((None, None, d, de), lambda i, be, nu: (layer, be[i], 0, 0)),
                pl.BlockSpec((None, None, d, de), lambda i, be, nu: (layer, be[i], 0, 0)),
                pl.BlockSpec((None, None, de, d), lambda i, be, nu: (layer, be[i], 0, 0)),
            ],
            out_specs=pl.BlockSpec((blk,) + tile, lambda i, be, nu: (i, 0, 0)),
            scratch_shapes=[
                pltpu.VMEM((d, de), BF16),
                pltpu.VMEM((d, de), BF16),
                pltpu.VMEM((de, d), BF16),
            ],
        ),
        out_shape=jax.ShapeDtypeStruct((rows,) + tile, F32),
        compiler_params=_params("arbitrary"),
        name="moe_experts",
    )(block_expert, n_used, xs, w_gate, w_up, w_down)


def _combine_kernel(dest_ref, x_ref, mod_ref, meta_ref, ys_ref, o_ref, ybuf, sem, *, tm, d):
    def row_copy(r, k):
        return pltpu.make_async_copy(ys_ref.at[dest_ref[r * TOP_K + k]], ybuf.at[k, r], sem)

    def start(g, carry):
        for s in range(DMA_GROUP):
            for k in range(TOP_K):
                row_copy(g * DMA_GROUP + s, k).start(priority=k % 2)
        return carry

    def wait(g, carry):
        for s in range(DMA_GROUP):
            for k in range(TOP_K):
                row_copy(g * DMA_GROUP + s, k).wait()
        return carry

    lax.fori_loop(0, tm // DMA_GROUP, start, 0)
    lax.fori_loop(0, tm // DMA_GROUP, wait, 0)
    meta = meta_ref[...]
    y = meta[:, 4:5] * ybuf[0].reshape(tm, d) + meta[:, 5:6] * ybuf[1].reshape(tm, d)
    o_ref[...] = x_ref[...] + mod_ref[:, 5 * d:6 * d] * y


def _combine(dest, x, mod, meta, ys, tm):
    bsz, seq, d = x.shape
    nb = seq // tm
    return pl.pallas_call(
        functools.partial(_combine_kernel, tm=tm, d=d),
        grid=(bsz, nb),
        in_specs=[
            pl.BlockSpec((tm * TOP_K,), lambda b, i: (b * nb + i,), memory_space=pltpu.SMEM),
            pl.BlockSpec((None, tm, d), lambda b, i: (b, i, 0)),
            pl.BlockSpec((None, 1, mod.shape[-1]), lambda b, i: (b, 0, 0)),
            pl.BlockSpec((None, tm, 8), lambda b, i: (b, i, 0)),
            pl.BlockSpec(memory_space=pl.ANY),
        ],
        out_specs=pl.BlockSpec((None, tm, d), lambda b, i: (b, i, 0)),
        scratch_shapes=[
            pltpu.VMEM((TOP_K, tm) + ys.shape[1:], F32),
            pltpu.SemaphoreType.DMA(()),
        ],
        out_shape=jax.ShapeDtypeStruct((bsz, seq, d), F32),
        compiler_params=_params("arbitrary", "arbitrary"),
        name="moe_combine",
    )(dest, x, mod, meta, ys)


def _hier_moe(x, mod, gain, w_group, b_group, w_expert, b_expert, w_gate, w_up, w_down, layer):
    bsz, seq, d = x.shape
    groups = w_group.shape[1]
    n_exp = w_expert.shape[1]
    per_group = n_exp // groups
    n_tok = bsz * seq
    m = n_tok * TOP_K
    blk = 512
    tm = 512

    pad = LANES - groups - n_exp
    w_router = jnp.concatenate([w_group, w_expert, jnp.zeros((d, pad), F32)], axis=1)
    b_router = jnp.concatenate([b_group, b_expert, jnp.zeros((pad,), F32)]).reshape(1, LANES)
    h, meta, cnt = _router(x, mod, gain, w_router, b_router, groups, per_group, tm)

    counts = cnt[0, groups:groups + n_exp].astype(I32)
    padded = (counts + blk - 1) // blk * blk
    pad_end = jnp.cumsum(padded)
    pad_start = pad_end - padded
    n_blocks = -(-m // blk) + n_exp
    block_start = jnp.arange(n_blocks, dtype=I32) * blk
    block_expert = jnp.minimum(jnp.sum((pad_end[None, :] <= block_start[:, None]).astype(I32), axis=1), n_exp - 1)
    n_used = (pad_end[-1] // blk).astype(I32).reshape(1)
    meta_flat = meta.reshape(n_tok, 8)
    e_idx = meta_flat[:, 0:TOP_K].astype(I32)
    rank = meta_flat[:, TOP_K:2 * TOP_K].astype(I32)
    start_of = jnp.sum(jnp.where(e_idx[..., None] == jnp.arange(n_exp, dtype=I32), pad_start, 0), axis=-1)
    dest = (start_of + rank).reshape(m)

    pad_info = jnp.concatenate([pad_end, padded, n_used]).astype(I32)
    xs = _dispatch(h, dest, pad_info, n_blocks * blk, tm, blk, n_exp)
    ys = _experts(xs, block_expert, n_used, w_gate, w_up, w_down, layer, blk)
    return _combine(dest, x, mod, meta, ys, tm)


def kernel(x, c, mod_w, mod_b, norm_mix_g, norm_ffn_g, gdn_w_in, gdn_conv_w, gdn_a_log, gdn_dt_bias, gdn_out_norm_g, gdn_w_out, kv_mod_w, kv_mod_b, kv_norm_g, kv_w, kv_forget_b, k_norm_g, fox_w_qz, fox_q_norm_g, fox_w_out, moe_w_group, moe_b_group, moe_w_expert, moe_b_expert, moe_w_gate, moe_w_up, moe_w_down):
    bsz, seq, d = x.shape
    depth = mod_w.shape[0]
    n_a = gdn_w_in.shape[0]
    gdn_heads = gdn_a_log.shape[1]
    gdn_inner = gdn_heads * GDN_HEAD_DIM
    fox_heads = kv_forget_b.shape[0]
    fox_inner = fox_heads * FOX_HEAD_DIM
    tm = 512

    mod_all = _mod_vectors(c, mod_w, mod_b).reshape(depth, bsz, 1, 6 * d)
    kv_mod = _mod_vectors(c, kv_mod_w[None], kv_mod_b[None]).reshape(bsz, 1, 2 * d)
    bd = _block_diag_ones(fox_inner, FOX_HEAD_DIM)

    ka_sh = vt_sh = ft_sh = None
    for layer in range(depth):
        mod = mod_all[layer]
        gain_mix = norm_mix_g[layer].reshape(1, d)
        if layer < n_a:
            w_in = gdn_w_in[layer]
            qkv, sz, ab = _gdn_in_proj(x, mod, gain_mix, w_in[:, :4 * gdn_inner].astype(BF16),
                                       w_in[:, 4 * gdn_inner:].astype(BF16), gdn_conv_w[layer], gdn_heads, tm)
            o = _gdn_core(qkv, sz, ab, gdn_a_log[layer], gdn_dt_bias[layer],
                          gdn_out_norm_g[layer], gdn_heads, 4 * GDN_CHUNK)
            x = _out_proj_residual(o, x, mod, gdn_w_out[layer].astype(BF16), 2 * d, tm)
        else:
            j = layer - n_a
            wqz = fox_w_qz[j]
            qg = jnp.tile(fox_q_norm_g[j], fox_heads).reshape(1, fox_inner)
            qt, z = _fox_qz_proj(x, mod, gain_mix, wqz[:, :fox_inner].astype(BF16),
                                 wqz[:, fox_inner:].astype(BF16), qg, bd, ft_sh, fox_heads, tm)
            o = _fox_attention(qt, ka_sh, vt_sh, z, 512)
            x = _out_proj_residual(o, x, mod, fox_w_out[j].astype(BF16), 2 * d, tm)
        x = _hier_moe(x, mod, norm_ffn_g[layer].reshape(1, d), moe_w_group[layer], moe_b_group[layer],
                      moe_w_expert[layer], moe_b_expert[layer], moe_w_gate, moe_w_up, moe_w_down, layer)
        if layer == n_a - 1:
            kg = jnp.tile(k_norm_g, fox_heads).reshape(1, fox_inner)
            ka_sh, vt_sh, ft_sh = _shared_kv(
                x, kv_mod, kv_norm_g.reshape(1, d), kv_w[:, :fox_inner].astype(BF16),
                kv_w[:, fox_inner:2 * fox_inner].astype(BF16), kv_w[:, 2 * fox_inner:].astype(BF16),
                kv_forget_b.reshape(1, fox_heads), kg, bd, fox_heads, tm)
    return x
```

```python
import functools

import jax
import jax.numpy as jnp
import numpy as np
from jax import lax
from jax.experimental import pallas as pl
from jax.experimental.pallas import tpu as pltpu

F32 = jnp.float32
BF16 = jnp.bfloat16
I32 = jnp.int32

EPS = 1e-6
GDN_CHUNK = 64
GDN_HEAD_DIM = 128
FOX_HEAD_DIM = 64
TOP_K = 2
LANES = 128
SUBLANES = 8
VMEM_LIMIT = 56 * 1024 * 1024
HIGHEST = lax.Precision.HIGHEST

NT_DIMS = (((1,), (1,)), ((), ()))
TN_DIMS = (((0,), (0,)), ((), ()))


def _params(*sem):
    return pltpu.CompilerParams(dimension_semantics=sem, vmem_limit_bytes=VMEM_LIMIT)


def _sigmoid(x):
    return 1.0 / (1.0 + jnp.exp(-x))


def _silu(x):
    return x * _sigmoid(x)


def _softplus(x):
    return jnp.maximum(x, 0.0) + jnp.log(1.0 + jnp.exp(-jnp.abs(x)))


def _modulated(x, gain, scale, shift):
    ms = jnp.mean(x * x, axis=-1, keepdims=True)
    y = x * lax.rsqrt(ms + EPS)
    return (y * gain) * (1.0 + scale) + shift


def _bdot(a, b, dims=None):
    a = a.astype(BF16)
    b = b.astype(BF16)
    if dims is None:
        return jnp.dot(a, b, preferred_element_type=F32)
    return lax.dot_general(a, b, dims, preferred_element_type=F32)


def _hdot(a, b, dims=None):
    if dims is None:
        return jnp.dot(a, b, preferred_element_type=F32, precision=HIGHEST)
    return lax.dot_general(a, b, dims, preferred_element_type=F32, precision=HIGHEST)


def _tile_of(row):
    return lax.shift_right_logical(row, SUBLANES.bit_length() - 1)


def _row_in_tile(row):
    return row & (SUBLANES - 1)


def _iota2(shape, dim):
    return lax.broadcasted_iota(I32, shape, dim)


def _mod_kernel(c_ref, w_ref, b_ref, o_ref):
    c = c_ref[...]
    o_ref[...] = _hdot(_silu(c), w_ref[...]) + b_ref[...]


def _mod_vectors(c, w, b):
    n_layers, d, n = w.shape
    bsz = c.shape[0]
    tn = 1536 if n % 1536 == 0 else n
    return pl.pallas_call(
        _mod_kernel,
        grid=(n_layers, n // tn),
        in_specs=[
            pl.BlockSpec((bsz, d), lambda l, j: (0, 0)),
            pl.BlockSpec((None, d, tn), lambda l, j: (l, 0, j)),
            pl.BlockSpec((None, 1, tn), lambda l, j: (l, 0, j)),
        ],
        out_specs=pl.BlockSpec((None, bsz, tn), lambda l, j: (l, 0, j)),
        out_shape=jax.ShapeDtypeStruct((n_layers, bsz, n), F32),
        compiler_params=_params("parallel", "parallel"),
        name="mod_vectors",
    )(c, w, b.reshape(n_layers, 1, n))


def _gdn_in_kernel(x_ref, mod_ref, g_ref, w_ref, wab_ref, cw_ref, qkv_ref, sz_ref, oab_ref, xpad_ref,
                   *, d, tm, heads, dh, width):
    inner = heads * dh
    halo = 8
    i = pl.program_id(1)

    @pl.when(i == 0)
    def _():
        xpad_ref[0:halo, :] = jnp.zeros((halo, 3 * inner), F32)

    h = _modulated(x_ref[...], g_ref[...], mod_ref[:, d:2 * d], mod_ref[:, 0:d]).astype(BF16)
    oab_ref[...] = jnp.dot(h, wab_ref[...], preferred_element_type=F32)
    z = jnp.dot(h, w_ref[:, 3 * inner:4 * inner], preferred_element_type=F32)
    sz_ref[...] = _silu(z).astype(sz_ref.dtype)
    xpad_ref[halo:halo + tm, :] = jnp.dot(h, w_ref[:, 0:3 * inner], preferred_element_type=F32)
    for j in range(3 * heads):
        cols = slice(j * dh, (j + 1) * dh)
        acc = xpad_ref[halo:halo + tm, cols] * cw_ref[width - 1:width, cols]
        for s in range(1, width):
            acc = acc + xpad_ref[halo - s:halo - s + tm, cols] * cw_ref[width - 1 - s:width - s, cols]
        y = _silu(acc)
        if j < 2 * heads:
            y = y * lax.rsqrt(jnp.sum(y * y, axis=-1, keepdims=True) + EPS)
        if j < heads:
            y = y * (dh ** -0.5)
        qkv_ref[:, cols] = y.astype(qkv_ref.dtype)
    xpad_ref[0:halo, :] = xpad_ref[tm:tm + halo, :]


def _gdn_in_proj(x, mod, gain, w_main, w_ab, conv_w, heads, tm):
    bsz, seq, d = x.shape
    dh = GDN_HEAD_DIM
    inner = heads * dh
    nab = w_ab.shape[1]
    width = conv_w.shape[0]
    return pl.pallas_call(
        functools.partial(_gdn_in_kernel, d=d, tm=tm, heads=heads, dh=dh, width=width),
        grid=(bsz, seq // tm),
        in_specs=[
            pl.BlockSpec((None, tm, d), lambda b, i: (b, i, 0)),
            pl.BlockSpec((None, 1, mod.shape[-1]), lambda b, i: (b, 0, 0)),
            pl.BlockSpec((1, d), lambda b, i: (0, 0)),
            pl.BlockSpec((d, 4 * inner), lambda b, i: (0, 0)),
            pl.BlockSpec((d, nab), lambda b, i: (0, 0)),
            pl.BlockSpec((width, 3 * inner), lambda b, i: (0, 0)),
        ],
        out_specs=[
            pl.BlockSpec((None, tm, 3 * inner), lambda b, i: (b, i, 0)),
            pl.BlockSpec((None, tm, inner), lambda b, i: (b, i, 0)),
            pl.BlockSpec((None, tm, nab), lambda b, i: (b, i, 0)),
        ],
        out_shape=[
            jax.ShapeDtypeStruct((bsz, seq, 3 * inner), BF16),
            jax.ShapeDtypeStruct((bsz, seq, inner), BF16),
            jax.ShapeDtypeStruct((bsz, seq, nab), F32),
        ],
        scratch_shapes=[pltpu.VMEM((8 + tm, 3 * inner), F32)],
        compiler_params=_params("parallel", "arbitrary"),
        name="gdn_in_proj",
    )(x, mod, gain, w_main, w_ab, conv_w)


def _gdn_kernel(q_ref, k_ref, v_ref, sz_ref, ab_ref, alog_ref, dtb_ref, og_ref, o_ref, state_ref,
                *, tb, heads, dh):
    c = GDN_CHUNK
    pairs = heads // 2
    i = pl.program_id(1)

    @pl.when(i == 0)
    def _():
        state_ref[...] = jnp.zeros(state_ref.shape, F32)

    ab = ab_ref[...]
    beta_all = _sigmoid(ab[:, heads:2 * heads])
    g_all = -jnp.exp(alog_ref[...]) * _softplus(ab[:, 0:heads] + dtb_ref[...])
    og = og_ref[...]

    row = _iota2((c, 2 * c), 0)
    lane = _iota2((c, 2 * c), 1)
    first = lane < c
    col = jnp.where(first, lane, lane - c)
    incl = row >= col
    strict = row > col
    eye_p = jnp.where(row == col, 1.0, 0.0).astype(F32)
    tri = jnp.where(_iota2((c, c), 0) >= _iota2((c, c), 1), 1.0, 0.0).astype(F32)
    bd_small = (_iota2((2 * c, 2 * c), 0) < c) == (_iota2((2 * c, 2 * c), 1) < c)
    bd_wide = (_iota2((2 * c, 2 * dh), 0) < c) == (_iota2((2 * c, 2 * dh), 1) < dh)
    first_h = _iota2((pairs, 2 * c), 1) < c
    sel0 = jnp.where(_iota2((pairs, heads), 1) == 2 * _iota2((pairs, heads), 0), 1.0, 0.0).astype(F32)
    sel1 = jnp.where(_iota2((pairs, heads), 1) == 2 * _iota2((pairs, heads), 0) + 1, 1.0, 0.0).astype(F32)

    def block_diag(x, mask):
        return jnp.where(mask, jnp.concatenate([x, x], axis=0), 0.0).astype(BF16)

    def pair_cols(x, p):
        return jnp.concatenate([jnp.broadcast_to(x[:, 2 * p:2 * p + 1], (c, dh)),
                                jnp.broadcast_to(x[:, 2 * p + 1:2 * p + 2], (c, dh))], axis=1)

    units = [(ci, p) for ci in range(tb // c) for p in range(pairs)]
    gcums, glasts, pk, tt, attn, w_u, qg, kdec = {}, {}, {}, {}, {}, {}, {}, {}

    for ci in range(tb // c):
        r0 = ci * c
        gcum = _hdot(tri, g_all[r0:r0 + c, :])
        gc2 = jnp.concatenate([gcum, gcum], axis=0)
        gt = jnp.where(first_h, _hdot(sel0, gc2, NT_DIMS), _hdot(sel1, gc2, NT_DIMS))
        glast = gcum[c - 1:c, :]
        gcums[ci] = gcum
        glasts[ci] = glast
        for p in range(pairs):
            u = (ci, p)
            cols = slice(2 * p * dh, (2 * p + 2) * dh)
            kp = k_ref[r0:r0 + c, cols].astype(F32)
            qp = q_ref[r0:r0 + c, cols].astype(F32)
            vp = v_ref[r0:r0 + c, cols].astype(F32)
            gcol = jnp.where(first, gcum[:, 2 * p:2 * p + 1], gcum[:, 2 * p + 1:2 * p + 2])
            decay = jnp.exp(jnp.where(incl, gcol - gt[p:p + 1, :], -jnp.inf))
            beta2 = pair_cols(beta_all[r0:r0 + c, :], p)
            gcum2 = pair_cols(gcum, p)
            egc2 = jnp.exp(gcum2)
            kb = kp * beta2
            y = block_diag(kp, bd_wide)
            kq = _bdot(jnp.concatenate([kb, qp], axis=0), y, NT_DIMS)
            a_mat = jnp.where(strict, kq[0:c, :] * decay, 0.0)
            attn[u] = kq[c:2 * c, :] * decay
            pk[u] = -a_mat
            tt[u] = eye_p - a_mat
            kbg = kb * egc2
            vb = vp * beta2
            w_u[u] = jnp.concatenate(
                [jnp.concatenate([kbg[:, 0:dh], vb[:, 0:dh]], axis=1),
                 jnp.concatenate([kbg[:, dh:2 * dh], vb[:, dh:2 * dh]], axis=1)], axis=0).astype(BF16)
            qg[u] = qp * egc2
            kdec[u] = kp * jnp.exp(pair_cols(jnp.broadcast_to(glast, (c, heads)), p) - gcum2)

    span = 2
    while span <= c:
        last = span == c
        for u in units:
            bd = block_diag(pk[u], bd_small)
            if span == 2:
                pk[u] = _bdot(pk[u], bd)
            elif last:
                tt[u] = tt[u] + _bdot(tt[u], bd)
            else:
                both = _bdot(jnp.concatenate([pk[u], tt[u]], axis=0), bd)
                pk[u] = both[0:c, :]
                tt[u] = tt[u] + both[c:2 * c, :]
        span *= 2
    for u in units:
        t = tt[u]
        lhs = jnp.concatenate([jnp.where(first, t, 0.0), jnp.where(first, 0.0, t)], axis=0)
        w_u[u] = _bdot(lhs, w_u[u])

    for ci in range(tb // c):
        r0 = ci * c
        for p in range(pairs):
            u = (ci, p)
            wu = w_u[u]
            st, wq, vn = [], [], []
            for s in range(2):
                h = 2 * p + s
                st.append(state_ref[h])
                wq.append(_bdot(jnp.concatenate([wu[s * c:(s + 1) * c, 0:dh], qg[u][:, s * dh:(s + 1) * dh]], axis=0),
                                st[s]))
                vn.append(wu[s * c:(s + 1) * c, dh:2 * dh] - wq[s][0:c, :])
            vn2 = jnp.concatenate(vn, axis=0).astype(BF16)
            for s in range(2):
                h = 2 * p + s
                am = jnp.where(first, attn[u], 0.0) if s == 0 else jnp.where(first, 0.0, attn[u])
                o = wq[s][c:2 * c, :] + _bdot(am, vn2)
                gl = glasts[ci][:, h:h + 1]
                state_ref[h] = st[s] * jnp.exp(gl) + _bdot(kdec[u][:, s * dh:(s + 1) * dh], vn[s], TN_DIMS)
                on = o * lax.rsqrt(jnp.mean(o * o, axis=-1, keepdims=True) + EPS) * og
                szh = sz_ref[r0:r0 + c, h * dh:(h + 1) * dh].astype(F32)
                o_ref[r0:r0 + c, h * dh:(h + 1) * dh] = (on * szh).astype(o_ref.dtype)


def _gdn_core(qkv, sz, ab, a_log, dt_bias, out_g, heads, tb):
    bsz, seq, _ = qkv.shape
    dh = GDN_HEAD_DIM
    inner = heads * dh
    return pl.pallas_call(
        functools.partial(_gdn_kernel, tb=tb, heads=heads, dh=dh),
        grid=(bsz, seq // tb),
        in_specs=[
            pl.BlockSpec((None, tb, inner), lambda b, i: (b, i, 0)),
            pl.BlockSpec((None, tb, inner), lambda b, i: (b, i, 1)),
            pl.BlockSpec((None, tb, inner), lambda b, i: (b, i, 2)),
            pl.BlockSpec((None, tb, inner), lambda b, i: (b, i, 0)),
            pl.BlockSpec((None, tb, 2 * heads), lambda b, i: (b, i, 0)),
            pl.BlockSpec((1, heads), lambda b, i: (0, 0)),
            pl.BlockSpec((1, heads), lambda b, i: (0, 0)),
            pl.BlockSpec((1, dh), lambda b, i: (0, 0)),
        ],
        out_specs=pl.BlockSpec((None, tb, inner), lambda b, i: (b, i, 0)),
        out_shape=jax.ShapeDtypeStruct((bsz, seq, inner), BF16),
        scratch_shapes=[pltpu.VMEM((heads, dh, dh), F32)],
        compiler_params=_params("parallel", "arbitrary"),
        name="gdn_core",
    )(qkv, qkv, qkv, sz, ab, a_log.reshape(1, heads), dt_bias.reshape(1, heads), out_g.reshape(1, dh))


def _head_rms(x, bd, gain, dh):
    ss = jnp.dot((x * x).astype(BF16), bd, preferred_element_type=F32) * (1.0 / dh)
    return x * lax.rsqrt(ss + EPS) * gain


def _block_diag_ones(n, blk):
    r = jnp.arange(n, dtype=I32) // blk
    return (r[:, None] == r[None, :]).astype(BF16)


N_SPLIT = 3


def _bias_base(head, dh):
    return dh if head % 2 == 0 else 0


def _split3(x):
    hi = x.astype(BF16)
    r = x - hi.astype(F32)
    mid = r.astype(BF16)
    lo = (r - mid.astype(F32)).astype(BF16)
    return hi, mid, lo


def _key_bias_placement(heads, dh):
    place = np.zeros((N_SPLIT, heads, heads * 2 * dh), np.float32)
    for j in range(N_SPLIT):
        for h in range(heads):
            place[j, h, h * 2 * dh + _bias_base(h, dh) + j] = 1.0
    return jnp.asarray(place, BF16)


def _kv_kernel(x_ref, mod_ref, g_ref, wk_ref, wv_ref, wf_ref, fb_ref, kg_ref, bd_ref, place_ref,
               ka_ref, vt_ref, ft_ref, carry_ref, *, d, tm, heads, dh):
    i = pl.program_id(1)

    @pl.when(i == 0)
    def _():
        carry_ref[...] = jnp.zeros(carry_ref.shape, F32)

    h = _modulated(x_ref[...], g_ref[...], mod_ref[:, d:2 * d], mod_ref[:, 0:d]).astype(BF16)
    kraw = jnp.dot(h, wk_ref[...], preferred_element_type=F32)
    kn = _head_rms(kraw, bd_ref[...], kg_ref[...], dh)
    v = jnp.dot(h, wv_ref[...], preferred_element_type=F32).astype(BF16)
    f = jnp.dot(h, wf_ref[...], preferred_element_type=F32) + fb_ref[...]
    log_f = -_softplus(-f)
    tri = jnp.where(_iota2((tm, tm), 0) >= _iota2((tm, tm), 1), 1.0, 0.0).astype(F32)
    fcum = _hdot(tri, log_f) + carry_ref[...]
    carry_ref[...] = fcum[tm - 1:tm, :]
    eye_h = jnp.where(_iota2((heads, heads), 0) == _iota2((heads, heads), 1), 1.0, 0.0).astype(F32)
    fcum_t = _hdot(eye_h, fcum, NT_DIMS)
    for p in range(heads // 2):
        ft_ref[p] = fcum_t[2 * p:2 * p + 2, :]

    pieces = _split3(-fcum)
    bias = jnp.dot(pieces[0], place_ref[0], preferred_element_type=F32)
    for j in range(1, N_SPLIT):
        bias = bias + jnp.dot(pieces[j], place_ref[j], preferred_element_type=F32)
    lane = _iota2((tm, 2 * dh), 1)
    for hh in range(heads):
        p = hh // 2
        base = _bias_base(hh, dh)
        real = (lane < dh) if hh % 2 == 0 else (lane >= dh)
        ones = (lane >= base + N_SPLIT) & (lane < base + 2 * N_SPLIT)
        blk = jnp.where(real, kn[:, 2 * p * dh:(2 * p + 2) * dh],
                        jnp.where(ones, 1.0, bias[:, hh * 2 * dh:(hh + 1) * 2 * dh]))
        ka_ref[hh] = blk.astype(ka_ref.dtype)

    eye = jnp.where(_iota2((2 * dh, 2 * dh), 0) == _iota2((2 * dh, 2 * dh), 1), 1.0, 0.0).astype(BF16)
    rowi = _iota2((2 * dh, tm), 0)
    for p in range(heads // 2):
        vt = lax.dot_general(eye, v[:, 2 * p * dh:(2 * p + 2) * dh], NT_DIMS, preferred_element_type=F32)
        vt_ref[2 * p] = jnp.where(rowi < dh, vt, 1.0).astype(vt_ref.dtype)
        vt_ref[2 * p + 1] = jnp.where(rowi >= dh, vt, 1.0).astype(vt_ref.dtype)


def _shared_kv(x, mod, gain, wk, wv, wf, fb, kgain, bd, heads, tm):
    bsz, seq, d = x.shape
    dh = FOX_HEAD_DIM
    inner = heads * dh
    pairs = heads // 2
    place = _key_bias_placement(heads, dh)
    return pl.pallas_call(
        functools.partial(_kv_kernel, d=d, tm=tm, heads=heads, dh=dh),
        grid=(bsz, seq // tm),
        in_specs=[
            pl.BlockSpec((None, tm, d), lambda b, i: (b, i, 0)),
            pl.BlockSpec((None, 1, mod.shape[-1]), lambda b, i: (b, 0, 0)),
            pl.BlockSpec((1, d), lambda b, i: (0, 0)),
            pl.BlockSpec((d, inner), lambda b, i: (0, 0)),
            pl.BlockSpec((d, inner), lambda b, i: (0, 0)),
            pl.BlockSpec((d, heads), lambda b, i: (0, 0)),
            pl.BlockSpec((1, heads), lambda b, i: (0, 0)),
            pl.BlockSpec((1, inner), lambda b, i: (0, 0)),
            pl.BlockSpec((inner, inner), lambda b, i: (0, 0)),
            pl.BlockSpec(place.shape, lambda b, i: (0, 0, 0)),
        ],
        out_specs=[
            pl.BlockSpec((None, heads, tm, 2 * dh), lambda b, i: (b, 0, i, 0)),
            pl.BlockSpec((None, heads, 2 * dh, tm), lambda b, i: (b, 0, 0, i)),
            pl.BlockSpec((None, pairs, 2, tm), lambda b, i: (b, 0, 0, i)),
        ],
        out_shape=[
            jax.ShapeDtypeStruct((bsz, heads, seq, 2 * dh), BF16),
            jax.ShapeDtypeStruct((bsz, heads, 2 * dh, seq), BF16),
            jax.ShapeDtypeStruct((bsz, pairs, 2, seq), F32),
        ],
        scratch_shapes=[pltpu.VMEM((1, heads), F32)],
        compiler_params=_params("parallel", "arbitrary"),
        name="shared_kv",
    )(x, mod, gain, wk, wv, wf, fb, kgain, bd, place)


def _fox_qz_kernel(x_ref, mod_ref, g_ref, wq_ref, wz_ref, qg_ref, bd_ref, ft_ref, qt_ref, z_ref,
                   *, d, dh, heads, tm):
    h = _modulated(x_ref[...], g_ref[...], mod_ref[:, d:2 * d], mod_ref[:, 0:d]).astype(BF16)
    qraw = jnp.dot(h, wq_ref[...], preferred_element_type=F32)
    qn = (_head_rms(qraw, bd_ref[...], qg_ref[...], dh) * (dh ** -0.5)).astype(BF16)
    z_ref[...] = jnp.dot(h, wz_ref[...], preferred_element_type=F32).astype(z_ref.dtype)
    eye = jnp.where(_iota2((2 * dh, 2 * dh), 0) == _iota2((2 * dh, 2 * dh), 1), 1.0, 0.0).astype(BF16)
    rowi = _iota2((2 * dh, tm), 0)
    for p in range(heads // 2):
        qt = lax.dot_general(eye, qn[:, 2 * p * dh:(2 * p + 2) * dh], NT_DIMS, preferred_element_type=F32)
        for s in range(2):
            hh = 2 * p + s
            base = _bias_base(hh, dh)
            real = (rowi < dh) if s == 0 else (rowi >= dh)
            aug = jnp.where(real, qt, 0.0)
            aug = jnp.where((rowi >= base) & (rowi < base + N_SPLIT), 1.0, aug)
            pieces = _split3(ft_ref[p, s:s + 1, :])
            for j in range(N_SPLIT):
                aug = jnp.where(rowi == base + N_SPLIT + j, pieces[j].astype(F32), aug)
            qt_ref[hh] = aug.astype(qt_ref.dtype)


def _fox_qz_proj(x, mod, gain, wq, wz, qgain, bd, ft, heads, tm):
    bsz, seq, d = x.shape
    dh = FOX_HEAD_DIM
    inner = wq.shape[1]
    pairs = heads // 2
    return pl.pallas_call(
        functools.partial(_fox_qz_kernel, d=d, dh=dh, heads=heads, tm=tm),
        grid=(bsz, seq // tm),
        in_specs=[
            pl.BlockSpec((None, tm, d), lambda b, i: (b, i, 0)),
            pl.BlockSpec((None, 1, mod.shape[-1]), lambda b, i: (b, 0, 0)),
            pl.BlockSpec((1, d), lambda b, i: (0, 0)),
            pl.BlockSpec((d, inner), lambda b, i: (0, 0)),
            pl.BlockSpec((d, inner), lambda b, i: (0, 0)),
            pl.BlockSpec((1, inner), lambda b, i: (0, 0)),
            pl.BlockSpec((inner, inner), lambda b, i: (0, 0)),
            pl.BlockSpec((None, pairs, 2, tm), lambda b, i: (b, 0, 0, i)),
        ],
        out_specs=[
            pl.BlockSpec((None, heads, 2 * dh, tm), lambda b, i: (b, 0, 0, i)),
            pl.BlockSpec((None, tm, inner), lambda b, i: (b, i, 0)),
        ],
        out_shape=[
            jax.ShapeDtypeStruct((bsz, heads, 2 * dh, seq), BF16),
            jax.ShapeDtypeStruct((bsz, seq, inner), BF16),
        ],
        compiler_params=_params("parallel", "parallel"),
        name="fox_qz_proj",
    )(x, mod, gain, wq, wz, qgain, bd, ft)


def _fox_kernel(qt_ref, ka_ref, vt_ref, z_ref, o_ref, s_ref, p_ref, acc_ref, *, tq, dh):
    i = pl.program_id(2)
    acc_ref[...] = jnp.zeros(acc_ref.shape, F32)
    key_idx = _iota2((tq, LANES), 0)
    qry_idx = _iota2((tq, LANES), 1)

    wide = 2 * LANES
    units = [(h, c) for h in range(2) for c in range(tq // wide)]

    def scores(kb, slot, u):
        h, c = u
        start = pl.multiple_of(kb * tq, tq)
        s = jnp.dot(ka_ref[h, pl.ds(start, tq), :], qt_ref[h, :, c * wide:(c + 1) * wide],
                    preferred_element_type=F32)
        for t in range(wide // LANES):
            s_ref[slot, h, c * (wide // LANES) + t, 0:tq, :] = s[:, t * LANES:(t + 1) * LANES]

    def softmax_pv(kb, slot, u, ms, masked, m_out):
        h, c = u
        start = pl.multiple_of(kb * tq, tq)
        alphas = []
        for t in range(wide // LANES):
            q0 = c * wide + t * LANES
            cols = slice(q0, q0 + LANES)
            s = s_ref[slot, h, q0 // LANES, 0:tq, :]
            if masked:
                s = jnp.where(key_idx <= qry_idx + q0, s, -jnp.inf)
            m_old = ms[h][:, cols]
            parts = [s[r:r + 64, :] for r in range(0, tq, 64)]
            while len(parts) > 1:
                parts = [jnp.maximum(parts[n], parts[n + 1]) for n in range(0, len(parts), 2)]
            m_new = jnp.maximum(m_old, jnp.max(parts[0], axis=0, keepdims=True))
            p_ref[h, q0 // LANES, 0:tq, :] = jnp.exp((s - m_new).astype(BF16))
            alphas.append(jnp.exp(m_old - m_new))
            m_out[(h, q0)] = m_new
        cols = slice(c * wide, (c + 1) * wide)
        alpha = jnp.concatenate(alphas, axis=1)
        p = jnp.concatenate([p_ref[h, c * (wide // LANES) + t, 0:tq, :] for t in range(wide // LANES)], axis=1)
        acc_ref[h, :, cols] = alpha * acc_ref[h, :, cols] + jnp.dot(
            vt_ref[h, :, pl.ds(start, tq)], p, preferred_element_type=F32)

    def step(kb, slot, ms, masked, prefetch):
        m_out = {}
        if prefetch:
            for u in units:
                scores(kb + 1, 1 - slot, u)
        for u in units:
            softmax_pv(kb, slot, u, ms, masked, m_out)
        return tuple(jnp.concatenate([m_out[(h, q0)] for q0 in range(0, tq, LANES)], axis=1) for h in range(2))

    def by_parity(kb, ms, masked, prefetch):
        return lax.cond(kb % 2 == 0,
                        lambda ms: step(kb, 0, ms, masked, prefetch),
                        lambda ms: step(kb, 1, ms, masked, prefetch), ms)

    for u in units:
        scores(0, 0, u)
    neg = jnp.full((1, tq), -jnp.inf, F32)
    ms = lax.fori_loop(0, i, lambda kb, ms: by_parity(kb, ms, False, True), (neg, neg))
    by_parity(i, ms, True, False)
    a0 = acc_ref[0]
    a1 = acc_ref[1]
    num = jnp.concatenate([a0[0:dh, :], a1[dh:2 * dh, :]], axis=0)
    den = jnp.concatenate([a0[dh:2 * dh, :], a1[0:dh, :]], axis=0)
    o = jnp.transpose(num / den)
    o_ref[...] = (o * _sigmoid(z_ref[...].astype(F32))).astype(o_ref.dtype)


def _fox_attention(qt, ka, vt, z, tq):
    bsz, heads, feat, seq = qt.shape
    dh = FOX_HEAD_DIM
    pairs = heads // 2
    return pl.pallas_call(
        functools.partial(_fox_kernel, tq=tq, dh=dh),
        grid=(bsz, pairs, seq // tq),
        in_specs=[
            pl.BlockSpec((None, 2, feat, tq), lambda b, p, i: (b, p, 0, i)),
            pl.BlockSpec((None, 2, seq, feat), lambda b, p, i: (b, p, 0, 0)),
            pl.BlockSpec((None, 2, feat, seq), lambda b, p, i: (b, p, 0, 0)),
            pl.BlockSpec((None, tq, 2 * dh), lambda b, p, i: (b, i, p)),
        ],
        out_specs=pl.BlockSpec((None, tq, 2 * dh), lambda b, p, i: (b, i, p)),
        out_shape=jax.ShapeDtypeStruct((bsz, seq, heads * dh), BF16),
        scratch_shapes=[
            pltpu.VMEM((2, 2, tq // LANES, tq + SUBLANES, LANES), F32),
            pltpu.VMEM((2, tq // LANES, tq + 2 * SUBLANES, LANES), BF16),
            pltpu.VMEM((2, feat, tq), F32),
        ],
        compiler_params=_params("parallel", "parallel", "parallel"),
        name="fox_attention",
    )(qt, ka, vt, z)


def _router_kernel(o_ref, wo_ref, x_ref, mod_ref, g_ref, wr_ref, br_ref, xo_ref, h_ref, meta_ref, cnt_ref,
                   carry_ref, *, d, tm, groups, per_group):
    step = pl.program_id(0) * pl.num_programs(1) + pl.program_id(1)

    @pl.when(step == 0)
    def _():
        carry_ref[...] = jnp.zeros(carry_ref.shape, F32)

    x = x_ref[...] + mod_ref[:, 2 * d:3 * d] * jnp.dot(o_ref[...], wo_ref[...], preferred_element_type=F32)
    xo_ref[...] = x
    h = _modulated(x, g_ref[...], mod_ref[:, 4 * d:5 * d], mod_ref[:, 3 * d:4 * d])
    h_ref[...] = h.reshape(h_ref.shape)
    h_hi = h.astype(BF16)
    h_lo = (h - h_hi.astype(F32)).astype(BF16)
    logits = (jnp.dot(h_hi, wr_ref[0], preferred_element_type=F32)
              + (jnp.dot(h_lo, wr_ref[0], preferred_element_type=F32)
                 + jnp.dot(h_hi, wr_ref[1], preferred_element_type=F32))
              + br_ref[...])
    lane = _iota2((tm, LANES), 1)
    neg_inf = jnp.float32(-jnp.inf)
    big = jnp.int32(LANES)

    def first_argmax(vals, mask):
        mv = jnp.where(mask, vals, neg_inf)
        mx = jnp.max(mv, axis=-1, keepdims=True)
        idx = jnp.min(jnp.where(mask & (mv == mx), lane, big), axis=-1, keepdims=True)
        return mx, idx

    gmask = lane < groups
    gmax, gidx = first_argmax(logits, gmask)
    gsum = jnp.sum(jnp.where(gmask, jnp.exp(logits - gmax), 0.0), axis=-1, keepdims=True)
    group_gate = 1.0 / gsum
    lo = groups + gidx * per_group
    emask = (lane >= lo) & (lane < lo + per_group)
    e1max, e1lane = first_argmax(logits, emask)
    e2max, e2lane = first_argmax(logits, emask & (lane != e1lane))
    esum = jnp.sum(jnp.where(emask, jnp.exp(logits - e1max), 0.0), axis=-1, keepdims=True)
    p1 = 1.0 / esum
    p2 = jnp.exp(e2max - e1max) / esum
    psum = p1 + p2
    w1 = group_gate * (p1 / psum)
    w2 = group_gate * (p2 / psum)

    oh1 = (lane == e1lane).astype(F32)
    oh2 = (lane == e2lane).astype(F32)
    strict = jnp.where(_iota2((tm, tm), 0) > _iota2((tm, tm), 1), 1.0, 0.0).astype(BF16)
    c1 = jnp.dot(strict, oh1.astype(BF16), preferred_element_type=F32)
    c2 = jnp.dot(strict, oh2.astype(BF16), preferred_element_type=F32)
    tot1 = jnp.sum(oh1, axis=0, keepdims=True)
    tot2 = jnp.sum(oh2, axis=0, keepdims=True)
    carry = carry_ref[...]
    rank1 = jnp.sum(oh1 * (c1 + carry), axis=-1, keepdims=True)
    rank2 = jnp.sum(oh2 * (c2 + carry + tot1), axis=-1, keepdims=True)
    carry = carry + tot1 + tot2
    carry_ref[...] = carry
    cnt_ref[...] = carry

    e1 = (e1lane - groups).astype(F32)
    e2 = (e2lane - groups).astype(F32)
    mlane = _iota2((tm, 8), 1)
    meta = jnp.where(mlane == 0, e1, 0.0)
    meta = jnp.where(mlane == 1, e2, meta)
    meta = jnp.where(mlane == 2, rank1, meta)
    meta = jnp.where(mlane == 3, rank2, meta)
    meta = jnp.where(mlane == 4, w1, meta)
    meta = jnp.where(mlane == 5, w2, meta)
    meta_ref[...] = meta


def _router(o, w_out, x, mod, gain, w_router, b_router, groups, per_group, tm):
    bsz, seq, d = x.shape
    k = o.shape[-1]
    nb = seq // tm
    return pl.pallas_call(
        functools.partial(_router_kernel, d=d, tm=tm, groups=groups, per_group=per_group),
        grid=(bsz, seq // tm),
        in_specs=[
            pl.BlockSpec((None, tm, k), lambda b, i: (b, i, 0)),
            pl.BlockSpec((k, d), lambda b, i: (0, 0)),
            pl.BlockSpec((None, tm, d), lambda b, i: (b, i, 0)),
            pl.BlockSpec((None, 1, mod.shape[-1]), lambda b, i: (b, 0, 0)),
            pl.BlockSpec((1, d), lambda b, i: (0, 0)),
            pl.BlockSpec((2, d, LANES), lambda b, i: (0, 0, 0)),
            pl.BlockSpec((1, LANES), lambda b, i: (0, 0)),
        ],
        out_specs=[
            pl.BlockSpec((None, tm, d), lambda b, i: (b, i, 0)),
            pl.BlockSpec((tm, SUBLANES, d // SUBLANES), lambda b, i: (b * nb + i, 0, 0)),
            pl.BlockSpec((None, tm, 8), lambda b, i: (b, i, 0)),
            pl.BlockSpec((1, LANES), lambda b, i: (0, 0)),
        ],
        out_shape=[
            jax.ShapeDtypeStruct((bsz, seq, d), F32),
            jax.ShapeDtypeStruct((bsz * seq, SUBLANES, d // SUBLANES), F32),
            jax.ShapeDtypeStruct((bsz, seq, 8), F32),
            jax.ShapeDtypeStruct((1, LANES), F32),
        ],
        scratch_shapes=[pltpu.VMEM((1, LANES), F32)],
        compiler_params=_params("arbitrary", "arbitrary"),
        name="moe_router",
    )(o, w_out, x, mod, gain, w_router, b_router)


DMA_GROUP = 16


def _dispatch_kernel(pad_ref, dest_ref, h_ref, xs_ref, zero_ref, sem, zsem, *, tm, blk, n_exp):
    @pl.when(pl.program_id(0) == 0)
    def _():
        zero_ref[...] = jnp.zeros(zero_ref.shape, zero_ref.dtype)

        def zero_copy(e):
            return pltpu.make_async_copy(zero_ref, xs_ref.at[pl.ds(pad_ref[e] - blk, blk)], zsem)

        def start(e, carry):
            @pl.when(pad_ref[n_exp + e] > 0)
            def _():
                zero_copy(e).start()
            return carry

        def wait(e, carry):
            @pl.when(pad_ref[n_exp + e] > 0)
            def _():
                zero_copy(e).wait()
            return carry

        lax.fori_loop(0, n_exp, start, 0)
        lax.fori_loop(0, n_exp, wait, 0)

        def tail_copy(j):
            return pltpu.make_async_copy(zero_ref, xs_ref.at[pl.ds(j * blk, blk)], zsem)

        def tail_start(j, carry):
            tail_copy(j).start()
            return carry

        def tail_wait(j, carry):
            tail_copy(j).wait()
            return carry

        n_used = pad_ref[2 * n_exp]
        lax.fori_loop(n_used, xs_ref.shape[0] // blk, tail_start, 0)
        lax.fori_loop(n_used, xs_ref.shape[0] // blk, tail_wait, 0)

    def row_copy(r, k):
        return pltpu.make_async_copy(h_ref.at[r], xs_ref.at[dest_ref[r * TOP_K + k]], sem)

    def start(g, carry):
        for s in range(DMA_GROUP):
            for k in range(TOP_K):
                row_copy(g * DMA_GROUP + s, k).start(priority=k % 2)
        return carry

    def wait(g, carry):
        for s in range(DMA_GROUP):
            for k in range(TOP_K):
                row_copy(g * DMA_GROUP + s, k).wait()
        return carry

    lax.fori_loop(0, tm // DMA_GROUP, start, 0)
    lax.fori_loop(0, tm // DMA_GROUP, wait, 0)


def _dispatch(h_tiles, dest, pad_info, rows, tm, blk, n_exp):
    n_tok = h_tiles.shape[0]
    tile = h_tiles.shape[1:]
    return pl.pallas_call(
        functools.partial(_dispatch_kernel, tm=tm, blk=blk, n_exp=n_exp),
        grid_spec=pltpu.PrefetchScalarGridSpec(
            num_scalar_prefetch=1,
            grid=(n_tok // tm,),
            in_specs=[
                pl.BlockSpec((tm * TOP_K,), lambda i, pad: (i,), memory_space=pltpu.SMEM),
                pl.BlockSpec((tm,) + tile, lambda i, pad: (i, 0, 0)),
            ],
            out_specs=pl.BlockSpec(memory_space=pl.ANY),
            scratch_shapes=[
                pltpu.VMEM((blk,) + tile, F32),
                pltpu.SemaphoreType.DMA(()),
                pltpu.SemaphoreType.DMA(()),
            ],
        ),
        out_shape=jax.ShapeDtypeStruct((rows,) + tile, F32),
        compiler_params=_params("arbitrary"),
        name="moe_dispatch",
    )(pad_info, dest, h_tiles)


def _expert_kernel(be_ref, nu_ref, xs_ref, wg_ref, wu_ref, wd_ref, ys_ref, wgb, wub, wdb):
    i = pl.program_id(0)
    prev = be_ref[jnp.maximum(i - 1, 0)]
    fresh = (i == 0) | (be_ref[i] != prev)

    @pl.when(fresh)
    def _():
        wgb[...] = wg_ref[...].astype(BF16)
        wub[...] = wu_ref[...].astype(BF16)
        wdb[...] = wd_ref[...].astype(BF16)

    @pl.when(i < nu_ref[0])
    def _():
        blk = xs_ref.shape[0]
        x = xs_ref[...].reshape(blk, wgb.shape[0]).astype(BF16)
        g = jnp.dot(x, wgb[...], preferred_element_type=F32)
        u = jnp.dot(x, wub[...], preferred_element_type=F32)
        mid = (_silu(g) * u).astype(BF16)
        ys_ref[...] = jnp.dot(mid, wdb[...], preferred_element_type=F32).reshape(ys_ref.shape)

    @pl.when(i >= nu_ref[0])
    def _():
        ys_ref[...] = jnp.zeros(ys_ref.shape, ys_ref.dtype)


def _experts(xs, block_expert, n_used, w_gate, w_up, w_down, layer, blk):
    rows = xs.shape[0]
    tile = xs.shape[1:]
    d, de = w_gate.shape[-2:]
    n_blocks = rows // blk
    return pl.pallas_call(
        _expert_kernel,
        grid_spec=pltpu.PrefetchScalarGridSpec(
            num_scalar_prefetch=2,
            grid=(n_blocks,),
            in_specs=[
                pl.BlockSpec((blk,) + tile, lambda i, be, nu: (jnp.minimum(i, nu[0] - 1), 0, 0)),
                pl.BlockSpec((None, None, d, de), lambda i, be, nu: (layer, be[i], 0, 0)),
                pl.BlockSpec((None, None, d, de), lambda i, be, nu: (layer, be[i], 0, 0)),
                pl.BlockSpec((None, None, de, d), lambda i, be, nu: (layer, be[i], 0, 0)),
            ],
            out_specs=pl.BlockSpec((blk,) + tile, lambda i, be, nu: (i, 0, 0)),
            scratch_shapes=[
                pltpu.VMEM((d, de), BF16),
                pltpu.VMEM((d, de), BF16),
                pltpu.VMEM((de, d), BF16),
            ],
        ),
        out_shape=jax.ShapeDtypeStruct((rows,) + tile, F32),
        compiler_params=_params("arbitrary"),
        name="moe_experts",
    )(block_expert, n_used, xs, w_gate, w_up, w_down)


def _combine_kernel(dest_ref, x_ref, mod_ref, meta_ref, ys_ref, o_ref, ybuf, sem, *, tm, d):
    def row_copy(r, k):
        return pltpu.make_async_copy(ys_ref.at[dest_ref[r * TOP_K + k]], ybuf.at[k, r], sem)

    def start(g, carry):
        for s in range(DMA_GROUP):
            for k in range(TOP_K):
                row_copy(g * DMA_GROUP + s, k).start(priority=k % 2)
        return carry

    def wait(g, carry):
        for s in range(DMA_GROUP):
            for k in range(TOP_K):
                row_copy(g * DMA_GROUP + s, k).wait()
        return carry

    lax.fori_loop(0, tm // DMA_GROUP, start, 0)
    lax.fori_loop(0, tm // DMA_GROUP, wait, 0)
    meta = meta_ref[...]
    y = meta[:, 4:5] * ybuf[0].reshape(tm, d) + meta[:, 5:6] * ybuf[1].reshape(tm, d)
    o_ref[...] = x_ref[...] + mod_ref[:, 5 * d:6 * d] * y


def _combine(dest, x, mod, meta, ys, tm):
    bsz, seq, d = x.shape
    nb = seq // tm
    return pl.pallas_call(
        functools.partial(_combine_kernel, tm=tm, d=d),
        grid=(bsz, nb),
        in_specs=[
            pl.BlockSpec((tm * TOP_K,), lambda b, i: (b * nb + i,), memory_space=pltpu.SMEM),
            pl.BlockSpec((None, tm, d), lambda b, i: (b, i, 0)),
            pl.BlockSpec((None, 1, mod.shape[-1]), lambda b, i: (b, 0, 0)),
            pl.BlockSpec((None, tm, 8), lambda b, i: (b, i, 0)),
            pl.BlockSpec(memory_space=pl.ANY),
        ],
        out_specs=pl.BlockSpec((None, tm, d), lambda b, i: (b, i, 0)),
        scratch_shapes=[
            pltpu.VMEM((TOP_K, tm) + ys.shape[1:], F32),
            pltpu.SemaphoreType.DMA(()),
        ],
        out_shape=jax.ShapeDtypeStruct((bsz, seq, d), F32),
        compiler_params=_params("arbitrary", "arbitrary"),
        name="moe_combine",
    )(dest, x, mod, meta, ys)


def _mixer_out_and_moe(o, w_out, x, mod, gain, w_group, b_group, w_expert, b_expert, w_gate, w_up, w_down, layer):
    bsz, seq, d = x.shape
    groups = w_group.shape[1]
    n_exp = w_expert.shape[1]
    per_group = n_exp // groups
    n_tok = bsz * seq
    m = n_tok * TOP_K
    blk = 512
    tm = 512

    pad = LANES - groups - n_exp
    w_router = jnp.concatenate([w_group, w_expert, jnp.zeros((d, pad), F32)], axis=1)
    w_router_hi = w_router.astype(BF16)
    w_router = jnp.stack([w_router_hi, (w_router - w_router_hi.astype(F32)).astype(BF16)])
    b_router = jnp.concatenate([b_group, b_expert, jnp.zeros((pad,), F32)]).reshape(1, LANES)
    x, h, meta, cnt = _router(o, w_out, x, mod, gain, w_router, b_router, groups, per_group, tm)

    counts = cnt[0, groups:groups + n_exp].astype(I32)
    padded = (counts + blk - 1) // blk * blk
    pad_end = jnp.cumsum(padded)
    pad_start = pad_end - padded
    n_blocks = -(-m // blk) + n_exp
    block_start = jnp.arange(n_blocks, dtype=I32) * blk
    block_expert = jnp.minimum(jnp.sum((pad_end[None, :] <= block_start[:, None]).astype(I32), axis=1), n_exp - 1)
    n_used = (pad_end[-1] // blk).astype(I32).reshape(1)
    meta_flat = meta.reshape(n_tok, 8)
    e_idx = meta_flat[:, 0:TOP_K].astype(I32)
    rank = meta_flat[:, TOP_K:2 * TOP_K].astype(I32)
    start_of = jnp.sum(jnp.where(e_idx[..., None] == jnp.arange(n_exp, dtype=I32), pad_start, 0), axis=-1)
    dest = (start_of + rank).reshape(m)

    pad_info = jnp.concatenate([pad_end, padded, n_used]).astype(I32)
    xs = _dispatch(h, dest, pad_info, n_blocks * blk, tm, blk, n_exp)
    ys = _experts(xs, block_expert, n_used, w_gate, w_up, w_down, layer, blk)
    return _combine(dest, x, mod, meta, ys, tm)


def kernel(x, c, mod_w, mod_b, norm_mix_g, norm_ffn_g, gdn_w_in, gdn_conv_w, gdn_a_log, gdn_dt_bias, gdn_out_norm_g, gdn_w_out, kv_mod_w, kv_mod_b, kv_norm_g, kv_w, kv_forget_b, k_norm_g, fox_w_qz, fox_q_norm_g, fox_w_out, moe_w_group, moe_b_group, moe_w_expert, moe_b_expert, moe_w_gate, moe_w_up, moe_w_down):
    bsz, seq, d = x.shape
    depth = mod_w.shape[0]
    n_a = gdn_w_in.shape[0]
    gdn_heads = gdn_a_log.shape[1]
    gdn_inner = gdn_heads * GDN_HEAD_DIM
    fox_heads = kv_forget_b.shape[0]
    fox_inner = fox_heads * FOX_HEAD_DIM
    tm = 512

    mod_all = _mod_vectors(c, mod_w, mod_b).reshape(depth, bsz, 1, 6 * d)
    kv_mod = _mod_vectors(c, kv_mod_w[None], kv_mod_b[None]).reshape(bsz, 1, 2 * d)
    bd = _block_diag_ones(fox_inner, FOX_HEAD_DIM)

    ka_sh = vt_sh = ft_sh = None
    for layer in range(depth):
        mod = mod_all[layer]
        gain_mix = norm_mix_g[layer].reshape(1, d)
        if layer < n_a:
            w_in = gdn_w_in[layer]
            qkv, sz, ab = _gdn_in_proj(x, mod, gain_mix, w_in[:, :4 * gdn_inner].astype(BF16),
                                       w_in[:, 4 * gdn_inner:].astype(BF16), gdn_conv_w[layer], gdn_heads, tm)
            o = _gdn_core(qkv, sz, ab, gdn_a_log[layer], gdn_dt_bias[layer],
                          gdn_out_norm_g[layer], gdn_heads, 4 * GDN_CHUNK)
            w_out = gdn_w_out[layer]
        else:
            j = layer - n_a
            wqz = fox_w_qz[j]
            qg = jnp.tile(fox_q_norm_g[j], fox_heads).reshape(1, fox_inner)
            qt, z = _fox_qz_proj(x, mod, gain_mix, wqz[:, :fox_inner].astype(BF16),
                                 wqz[:, fox_inner:].astype(BF16), qg, bd, ft_sh, fox_heads, tm)
            o = _fox_attention(qt, ka_sh, vt_sh, z, 512)
            w_out = fox_w_out[j]
        x = _mixer_out_and_moe(o, w_out.astype(BF16), x, mod, norm_ffn_g[layer].reshape(1, d),
                               moe_w_group[layer], moe_b_group[layer], moe_w_expert[layer], moe_b_expert[layer],
                               moe_w_gate, moe_w_up, moe_w_down, layer)
        if layer == n_a - 1:
            kg = jnp.tile(k_norm_g, fox_heads).reshape(1, fox_inner)
            ka_sh, vt_sh, ft_sh = _shared_kv(
                x, kv_mod, kv_norm_g.reshape(1, d), kv_w[:, :fox_inner].astype(BF16),
                kv_w[:, fox_inner:2 * fox_inner].astype(BF16), kv_w[:, 2 * fox_inner:].astype(BF16),
                kv_forget_b.reshape(1, fox_heads), kg, bd, fox_heads, tm)
    return x
```

```python
import functools

import jax
import jax.numpy as jnp
import numpy as np
from jax import lax
from jax.experimental import pallas as pl
from jax.experimental.pallas import tpu as pltpu

F32 = jnp.float32
BF16 = jnp.bfloat16
I32 = jnp.int32

EPS = 1e-6
GDN_CHUNK = 64
GDN_HEAD_DIM = 128
FOX_HEAD_DIM = 64
TOP_K = 2
LANES = 128
SUBLANES = 8
VMEM_LIMIT = 56 * 1024 * 1024
HIGHEST = lax.Precision.HIGHEST

NT_DIMS = (((1,), (1,)), ((), ()))
TN_DIMS = (((0,), (0,)), ((), ()))


def _params(*sem):
    return pltpu.CompilerParams(dimension_semantics=sem, vmem_limit_bytes=VMEM_LIMIT)


def _sigmoid(x):
    return 1.0 / (1.0 + jnp.exp(-x))


def _silu(x):
    return x * _sigmoid(x)


def _softplus(x):
    return jnp.maximum(x, 0.0) + jnp.log(1.0 + jnp.exp(-jnp.abs(x)))


def _modulated(x, gain, scale, shift):
    ms = jnp.mean(x * x, axis=-1, keepdims=True)
    y = x * lax.rsqrt(ms + EPS)
    return (y * gain) * (1.0 + scale) + shift


def _bdot(a, b, dims=None):
    a = a.astype(BF16)
    b = b.astype(BF16)
    if dims is None:
        return jnp.dot(a, b, preferred_element_type=F32)
    return lax.dot_general(a, b, dims, preferred_element_type=F32)


def _hdot(a, b, dims=None):
    if dims is None:
        return jnp.dot(a, b, preferred_element_type=F32, precision=HIGHEST)
    return lax.dot_general(a, b, dims, preferred_element_type=F32, precision=HIGHEST)


def _tile_of(row):
    return lax.shift_right_logical(row, SUBLANES.bit_length() - 1)


def _row_in_tile(row):
    return row & (SUBLANES - 1)


def _iota2(shape, dim):
    return lax.broadcasted_iota(I32, shape, dim)


def _mod_kernel(c_ref, w_ref, b_ref, o_ref):
    c = c_ref[...]
    o_ref[...] = _hdot(_silu(c), w_ref[...]) + b_ref[...]


def _mod_vectors(c, w, b):
    n_layers, d, n = w.shape
    bsz = c.shape[0]
    tn = 1536 if n % 1536 == 0 else n
    return pl.pallas_call(
        _mod_kernel,
        grid=(n_layers, n // tn),
        in_specs=[
            pl.BlockSpec((bsz, d), lambda l, j: (0, 0)),
            pl.BlockSpec((None, d, tn), lambda l, j: (l, 0, j)),
            pl.BlockSpec((None, 1, tn), lambda l, j: (l, 0, j)),
        ],
        out_specs=pl.BlockSpec((None, bsz, tn), lambda l, j: (l, 0, j)),
        out_shape=jax.ShapeDtypeStruct((n_layers, bsz, n), F32),
        compiler_params=_params("parallel", "parallel"),
        name="mod_vectors",
    )(c, w, b.reshape(n_layers, 1, n))


def _gdn_in_kernel(x_ref, mod_ref, g_ref, w_ref, wab_ref, cw_ref, qkv_ref, sz_ref, oab_ref, xpad_ref,
                   *, d, tm, heads, dh, width):
    inner = heads * dh
    halo = 8
    i = pl.program_id(1)

    @pl.when(i == 0)
    def _():
        xpad_ref[0:halo, :] = jnp.zeros((halo, 3 * inner), F32)

    h = _modulated(x_ref[...], g_ref[...], mod_ref[:, d:2 * d], mod_ref[:, 0:d]).astype(BF16)
    oab_ref[...] = jnp.dot(h, wab_ref[...], preferred_element_type=F32)
    z = jnp.dot(h, w_ref[:, 3 * inner:4 * inner], preferred_element_type=F32)
    sz_ref[...] = _silu(z).astype(sz_ref.dtype)
    xpad_ref[halo:halo + tm, :] = jnp.dot(h, w_ref[:, 0:3 * inner], preferred_element_type=F32)
    for j in range(3 * heads):
        cols = slice(j * dh, (j + 1) * dh)
        acc = xpad_ref[halo:halo + tm, cols] * cw_ref[width - 1:width, cols]
        for s in range(1, width):
            acc = acc + xpad_ref[halo - s:halo - s + tm, cols] * cw_ref[width - 1 - s:width - s, cols]
        y = _silu(acc)
        if j < 2 * heads:
            y = y * lax.rsqrt(jnp.sum(y * y, axis=-1, keepdims=True) + EPS)
        if j < heads:
            y = y * (dh ** -0.5)
        qkv_ref[:, cols] = y.astype(qkv_ref.dtype)
    xpad_ref[0:halo, :] = xpad_ref[tm:tm + halo, :]


def _gdn_in_proj(x, mod, gain, w_main, w_ab, conv_w, heads, tm):
    bsz, seq, d = x.shape
    dh = GDN_HEAD_DIM
    inner = heads * dh
    nab = w_ab.shape[1]
    width = conv_w.shape[0]
    return pl.pallas_call(
        functools.partial(_gdn_in_kernel, d=d, tm=tm, heads=heads, dh=dh, width=width),
        grid=(bsz, seq // tm),
        in_specs=[
            pl.BlockSpec((None, tm, d), lambda b, i: (b, i, 0)),
            pl.BlockSpec((None, 1, mod.shape[-1]), lambda b, i: (b, 0, 0)),
            pl.BlockSpec((1, d), lambda b, i: (0, 0)),
            pl.BlockSpec((d, 4 * inner), lambda b, i: (0, 0)),
            pl.BlockSpec((d, nab), lambda b, i: (0, 0)),
            pl.BlockSpec((width, 3 * inner), lambda b, i: (0, 0)),
        ],
        out_specs=[
            pl.BlockSpec((None, tm, 3 * inner), lambda b, i: (b, i, 0)),
            pl.BlockSpec((None, tm, inner), lambda b, i: (b, i, 0)),
            pl.BlockSpec((None, tm, nab), lambda b, i: (b, i, 0)),
        ],
        out_shape=[
            jax.ShapeDtypeStruct((bsz, seq, 3 * inner), BF16),
            jax.ShapeDtypeStruct((bsz, seq, inner), BF16),
            jax.ShapeDtypeStruct((bsz, seq, nab), F32),
        ],
        scratch_shapes=[pltpu.VMEM((8 + tm, 3 * inner), F32)],
        compiler_params=_params("parallel", "arbitrary"),
        name="gdn_in_proj",
    )(x, mod, gain, w_main, w_ab, conv_w)


def _gdn_kernel(q_ref, k_ref, v_ref, sz_ref, ab_ref, alog_ref, dtb_ref, og_ref, o_ref, state_ref,
                *, tb, heads, dh):
    c = GDN_CHUNK
    pairs = heads // 2
    i = pl.program_id(1)

    @pl.when(i == 0)
    def _():
        state_ref[...] = jnp.zeros(state_ref.shape, F32)

    ab = ab_ref[...]
    beta_all = _sigmoid(ab[:, heads:2 * heads])
    g_all = -jnp.exp(alog_ref[...]) * _softplus(ab[:, 0:heads] + dtb_ref[...])
    og = og_ref[...]

    row = _iota2((c, 2 * c), 0)
    lane = _iota2((c, 2 * c), 1)
    first = lane < c
    col = jnp.where(first, lane, lane - c)
    incl = row >= col
    strict = row > col
    eye_p = jnp.where(row == col, 1.0, 0.0).astype(F32)
    tri = jnp.where(_iota2((c, c), 0) >= _iota2((c, c), 1), 1.0, 0.0).astype(F32)
    bd_small = (_iota2((2 * c, 2 * c), 0) < c) == (_iota2((2 * c, 2 * c), 1) < c)
    bd_wide = (_iota2((2 * c, 2 * dh), 0) < c) == (_iota2((2 * c, 2 * dh), 1) < dh)
    first_h = _iota2((pairs, 2 * c), 1) < c
    sel0 = jnp.where(_iota2((pairs, heads), 1) == 2 * _iota2((pairs, heads), 0), 1.0, 0.0).astype(F32)
    sel1 = jnp.where(_iota2((pairs, heads), 1) == 2 * _iota2((pairs, heads), 0) + 1, 1.0, 0.0).astype(F32)

    def block_diag(x, mask):
        return jnp.where(mask, jnp.concatenate([x, x], axis=0), 0.0).astype(BF16)

    def pair_cols(x, p):
        return jnp.concatenate([jnp.broadcast_to(x[:, 2 * p:2 * p + 1], (c, dh)),
                                jnp.broadcast_to(x[:, 2 * p + 1:2 * p + 2], (c, dh))], axis=1)

    units = [(ci, p) for ci in range(tb // c) for p in range(pairs)]
    gcums, glasts, pk, tt, attn, w_u, qg, kdec = {}, {}, {}, {}, {}, {}, {}, {}

    for ci in range(tb // c):
        r0 = ci * c
        gcum = _hdot(tri, g_all[r0:r0 + c, :])
        gc2 = jnp.concatenate([gcum, gcum], axis=0)
        gt = jnp.where(first_h, _hdot(sel0, gc2, NT_DIMS), _hdot(sel1, gc2, NT_DIMS))
        glast = gcum[c - 1:c, :]
        gcums[ci] = gcum
        glasts[ci] = glast
        for p in range(pairs):
            u = (ci, p)
            cols = slice(2 * p * dh, (2 * p + 2) * dh)
            kp = k_ref[r0:r0 + c, cols].astype(F32)
            qp = q_ref[r0:r0 + c, cols].astype(F32)
            vp = v_ref[r0:r0 + c, cols].astype(F32)
            gcol = jnp.where(first, gcum[:, 2 * p:2 * p + 1], gcum[:, 2 * p + 1:2 * p + 2])
            decay = jnp.exp(jnp.where(incl, gcol - gt[p:p + 1, :], -jnp.inf))
            beta2 = pair_cols(beta_all[r0:r0 + c, :], p)
            gcum2 = pair_cols(gcum, p)
            egc2 = jnp.exp(gcum2)
            kb = kp * beta2
            y = block_diag(kp, bd_wide)
            kq = _bdot(jnp.concatenate([kb, qp], axis=0), y, NT_DIMS)
            a_mat = jnp.where(strict, kq[0:c, :] * decay, 0.0)
            attn[u] = kq[c:2 * c, :] * decay
            pk[u] = -a_mat
            tt[u] = eye_p - a_mat
            kbg = kb * egc2
            vb = vp * beta2
            w_u[u] = jnp.concatenate(
                [jnp.concatenate([kbg[:, 0:dh], vb[:, 0:dh]], axis=1),
                 jnp.concatenate([kbg[:, dh:2 * dh], vb[:, dh:2 * dh]], axis=1)], axis=0).astype(BF16)
            qg[u] = qp * egc2
            kdec[u] = kp * jnp.exp(pair_cols(jnp.broadcast_to(glast, (c, heads)), p) - gcum2)

    span = 2
    while span <= c:
        last = span == c
        for u in units:
            bd = block_diag(pk[u], bd_small)
            if span == 2:
                pk[u] = _bdot(pk[u], bd)
            elif last:
                tt[u] = tt[u] + _bdot(tt[u], bd)
            else:
                both = _bdot(jnp.concatenate([pk[u], tt[u]], axis=0), bd)
                pk[u] = both[0:c, :]
                tt[u] = tt[u] + both[c:2 * c, :]
        span *= 2
    for u in units:
        t = tt[u]
        lhs = jnp.concatenate([jnp.where(first, t, 0.0), jnp.where(first, 0.0, t)], axis=0)
        w_u[u] = _bdot(lhs, w_u[u])

    for ci in range(tb // c):
        r0 = ci * c
        for p in range(pairs):
            u = (ci, p)
            wu = w_u[u]
            st, wq, vn = [], [], []
            for s in range(2):
                h = 2 * p + s
                st.append(state_ref[h])
                wq.append(_bdot(jnp.concatenate([wu[s * c:(s + 1) * c, 0:dh], qg[u][:, s * dh:(s + 1) * dh]], axis=0),
                                st[s]))
                vn.append(wu[s * c:(s + 1) * c, dh:2 * dh] - wq[s][0:c, :])
            vn2 = jnp.concatenate(vn, axis=0).astype(BF16)
            for s in range(2):
                h = 2 * p + s
                am = jnp.where(first, attn[u], 0.0) if s == 0 else jnp.where(first, 0.0, attn[u])
                o = wq[s][c:2 * c, :] + _bdot(am, vn2)
                gl = glasts[ci][:, h:h + 1]
                state_ref[h] = st[s] * jnp.exp(gl) + _bdot(kdec[u][:, s * dh:(s + 1) * dh], vn[s], TN_DIMS)
                on = o * lax.rsqrt(jnp.mean(o * o, axis=-1, keepdims=True) + EPS) * og
                szh = sz_ref[r0:r0 + c, h * dh:(h + 1) * dh].astype(F32)
                o_ref[r0:r0 + c, h * dh:(h + 1) * dh] = (on * szh).astype(o_ref.dtype)


def _gdn_core(qkv, sz, ab, a_log, dt_bias, out_g, heads, tb):
    bsz, seq, _ = qkv.shape
    dh = GDN_HEAD_DIM
    inner = heads * dh
    return pl.pallas_call(
        functools.partial(_gdn_kernel, tb=tb, heads=heads, dh=dh),
        grid=(bsz, seq // tb),
        in_specs=[
            pl.BlockSpec((None, tb, inner), lambda b, i: (b, i, 0)),
            pl.BlockSpec((None, tb, inner), lambda b, i: (b, i, 1)),
            pl.BlockSpec((None, tb, inner), lambda b, i: (b, i, 2)),
            pl.BlockSpec((None, tb, inner), lambda b, i: (b, i, 0)),
            pl.BlockSpec((None, tb, 2 * heads), lambda b, i: (b, i, 0)),
            pl.BlockSpec((1, heads), lambda b, i: (0, 0)),
            pl.BlockSpec((1, heads), lambda b, i: (0, 0)),
            pl.BlockSpec((1, dh), lambda b, i: (0, 0)),
        ],
        out_specs=pl.BlockSpec((None, tb, inner), lambda b, i: (b, i, 0)),
        out_shape=jax.ShapeDtypeStruct((bsz, seq, inner), BF16),
        scratch_shapes=[pltpu.VMEM((heads, dh, dh), F32)],
        compiler_params=_params("parallel", "arbitrary"),
        name="gdn_core",
    )(qkv, qkv, qkv, sz, ab, a_log.reshape(1, heads), dt_bias.reshape(1, heads), out_g.reshape(1, dh))


def _head_rms(x, bd, gain, dh):
    ss = jnp.dot((x * x).astype(BF16), bd, preferred_element_type=F32) * (1.0 / dh)
    return x * lax.rsqrt(ss + EPS) * gain


def _block_diag_ones(n, blk):
    r = jnp.arange(n, dtype=I32) // blk
    return (r[:, None] == r[None, :]).astype(BF16)


N_SPLIT = 3


def _bias_base(head, dh):
    return dh if head % 2 == 0 else 0


def _split3(x):
    hi = x.astype(BF16)
    r = x - hi.astype(F32)
    mid = r.astype(BF16)
    lo = (r - mid.astype(F32)).astype(BF16)
    return hi, mid, lo


def _key_bias_placement(heads, dh):
    place = np.zeros((N_SPLIT, heads, heads * 2 * dh), np.float32)
    for j in range(N_SPLIT):
        for h in range(heads):
            place[j, h, h * 2 * dh + _bias_base(h, dh) + j] = 1.0
    return jnp.asarray(place, BF16)


def _kv_kernel(x_ref, mod_ref, g_ref, wk_ref, wv_ref, wf_ref, fb_ref, kg_ref, bd_ref, place_ref,
               ka_ref, vt_ref, ft_ref, carry_ref, *, d, tm, heads, dh):
    i = pl.program_id(1)

    @pl.when(i == 0)
    def _():
        carry_ref[...] = jnp.zeros(carry_ref.shape, F32)

    h = _modulated(x_ref[...], g_ref[...], mod_ref[:, d:2 * d], mod_ref[:, 0:d]).astype(BF16)
    kraw = jnp.dot(h, wk_ref[...], preferred_element_type=F32)
    kn = _head_rms(kraw, bd_ref[...], kg_ref[...], dh)
    v = jnp.dot(h, wv_ref[...], preferred_element_type=F32).astype(BF16)
    f = jnp.dot(h, wf_ref[...], preferred_element_type=F32) + fb_ref[...]
    log_f = -_softplus(-f)
    tri = jnp.where(_iota2((tm, tm), 0) >= _iota2((tm, tm), 1), 1.0, 0.0).astype(F32)
    fcum = _hdot(tri, log_f) + carry_ref[...]
    carry_ref[...] = fcum[tm - 1:tm, :]
    eye_h = jnp.where(_iota2((heads, heads), 0) == _iota2((heads, heads), 1), 1.0, 0.0).astype(F32)
    fcum_t = _hdot(eye_h, fcum, NT_DIMS)
    for p in range(heads // 2):
        ft_ref[p] = fcum_t[2 * p:2 * p + 2, :]

    pieces = _split3(-fcum)
    bias = jnp.dot(pieces[0], place_ref[0], preferred_element_type=F32)
    for j in range(1, N_SPLIT):
        bias = bias + jnp.dot(pieces[j], place_ref[j], preferred_element_type=F32)
    lane = _iota2((tm, 2 * dh), 1)
    for hh in range(heads):
        p = hh // 2
        base = _bias_base(hh, dh)
        real = (lane < dh) if hh % 2 == 0 else (lane >= dh)
        ones = (lane >= base + N_SPLIT) & (lane < base + 2 * N_SPLIT)
        blk = jnp.where(real, kn[:, 2 * p * dh:(2 * p + 2) * dh],
                        jnp.where(ones, 1.0, bias[:, hh * 2 * dh:(hh + 1) * 2 * dh]))
        ka_ref[hh] = blk.astype(ka_ref.dtype)

    eye = jnp.where(_iota2((2 * dh, 2 * dh), 0) == _iota2((2 * dh, 2 * dh), 1), 1.0, 0.0).astype(BF16)
    rowi = _iota2((2 * dh, tm), 0)
    for p in range(heads // 2):
        vt = lax.dot_general(eye, v[:, 2 * p * dh:(2 * p + 2) * dh], NT_DIMS, preferred_element_type=F32)
        vt_ref[2 * p] = jnp.where(rowi < dh, vt, 1.0).astype(vt_ref.dtype)
        vt_ref[2 * p + 1] = jnp.where(rowi >= dh, vt, 1.0).astype(vt_ref.dtype)


def _shared_kv(x, mod, gain, wk, wv, wf, fb, kgain, bd, heads, tm):
    bsz, seq, d = x.shape
    dh = FOX_HEAD_DIM
    inner = heads * dh
    pairs = heads // 2
    place = _key_bias_placement(heads, dh)
    return pl.pallas_call(
        functools.partial(_kv_kernel, d=d, tm=tm, heads=heads, dh=dh),
        grid=(bsz, seq // tm),
        in_specs=[
            pl.BlockSpec((None, tm, d), lambda b, i: (b, i, 0)),
            pl.BlockSpec((None, 1, mod.shape[-1]), lambda b, i: (b, 0, 0)),
            pl.BlockSpec((1, d), lambda b, i: (0, 0)),
            pl.BlockSpec((d, inner), lambda b, i: (0, 0)),
            pl.BlockSpec((d, inner), lambda b, i: (0, 0)),
            pl.BlockSpec((d, heads), lambda b, i: (0, 0)),
            pl.BlockSpec((1, heads), lambda b, i: (0, 0)),
            pl.BlockSpec((1, inner), lambda b, i: (0, 0)),
            pl.BlockSpec((inner, inner), lambda b, i: (0, 0)),
            pl.BlockSpec(place.shape, lambda b, i: (0, 0, 0)),
        ],
        out_specs=[
            pl.BlockSpec((None, heads, tm, 2 * dh), lambda b, i: (b, 0, i, 0)),
            pl.BlockSpec((None, heads, 2 * dh, tm), lambda b, i: (b, 0, 0, i)),
            pl.BlockSpec((None, pairs, 2, tm), lambda b, i: (b, 0, 0, i)),
        ],
        out_shape=[
            jax.ShapeDtypeStruct((bsz, heads, seq, 2 * dh), BF16),
            jax.ShapeDtypeStruct((bsz, heads, 2 * dh, seq), BF16),
            jax.ShapeDtypeStruct((bsz, pairs, 2, seq), F32),
        ],
        scratch_shapes=[pltpu.VMEM((1, heads), F32)],
        compiler_params=_params("parallel", "arbitrary"),
        name="shared_kv",
    )(x, mod, gain, wk, wv, wf, fb, kgain, bd, place)


def _fox_qz_kernel(x_ref, mod_ref, g_ref, wq_ref, wz_ref, qg_ref, bd_ref, ft_ref, qt_ref, z_ref,
                   *, d, dh, heads, tm):
    h = _modulated(x_ref[...], g_ref[...], mod_ref[:, d:2 * d], mod_ref[:, 0:d]).astype(BF16)
    qraw = jnp.dot(h, wq_ref[...], preferred_element_type=F32)
    qn = (_head_rms(qraw, bd_ref[...], qg_ref[...], dh) * (dh ** -0.5)).astype(BF16)
    z_ref[...] = jnp.dot(h, wz_ref[...], preferred_element_type=F32).astype(z_ref.dtype)
    eye = jnp.where(_iota2((2 * dh, 2 * dh), 0) == _iota2((2 * dh, 2 * dh), 1), 1.0, 0.0).astype(BF16)
    rowi = _iota2((2 * dh, tm), 0)
    for p in range(heads // 2):
        qt = lax.dot_general(eye, qn[:, 2 * p * dh:(2 * p + 2) * dh], NT_DIMS, preferred_element_type=F32)
        for s in range(2):
            hh = 2 * p + s
            base = _bias_base(hh, dh)
            real = (rowi < dh) if s == 0 else (rowi >= dh)
            aug = jnp.where(real, qt, 0.0)
            aug = jnp.where((rowi >= base) & (rowi < base + N_SPLIT), 1.0, aug)
            pieces = _split3(ft_ref[p, s:s + 1, :])
            for j in range(N_SPLIT):
                aug = jnp.where(rowi == base + N_SPLIT + j, pieces[j].astype(F32), aug)
            qt_ref[hh] = aug.astype(qt_ref.dtype)


def _fox_qz_proj(x, mod, gain, wq, wz, qgain, bd, ft, heads, tm):
    bsz, seq, d = x.shape
    dh = FOX_HEAD_DIM
    inner = wq.shape[1]
    pairs = heads // 2
    return pl.pallas_call(
        functools.partial(_fox_qz_kernel, d=d, dh=dh, heads=heads, tm=tm),
        grid=(bsz, seq // tm),
        in_specs=[
            pl.BlockSpec((None, tm, d), lambda b, i: (b, i, 0)),
            pl.BlockSpec((None, 1, mod.shape[-1]), lambda b, i: (b, 0, 0)),
            pl.BlockSpec((1, d), lambda b, i: (0, 0)),
            pl.BlockSpec((d, inner), lambda b, i: (0, 0)),
            pl.BlockSpec((d, inner), lambda b, i: (0, 0)),
            pl.BlockSpec((1, inner), lambda b, i: (0, 0)),
            pl.BlockSpec((inner, inner), lambda b, i: (0, 0)),
            pl.BlockSpec((None, pairs, 2, tm), lambda b, i: (b, 0, 0, i)),
        ],
        out_specs=[
            pl.BlockSpec((None, heads, 2 * dh, tm), lambda b, i: (b, 0, 0, i)),
            pl.BlockSpec((None, tm, inner), lambda b, i: (b, i, 0)),
        ],
        out_shape=[
            jax.ShapeDtypeStruct((bsz, heads, 2 * dh, seq), BF16),
            jax.ShapeDtypeStruct((bsz, seq, inner), BF16),
        ],
        compiler_params=_params("parallel", "parallel"),
        name="fox_qz_proj",
    )(x, mod, gain, wq, wz, qgain, bd, ft)


def _fox_kernel(qt_ref, ka_ref, vt_ref, z_ref, o_ref, s_ref, p_ref, acc_ref, *, tq, dh):
    i = pl.program_id(2)
    acc_ref[...] = jnp.zeros(acc_ref.shape, F32)
    key_idx = _iota2((tq, LANES), 0)
    qry_idx = _iota2((tq, LANES), 1)

    wide = 2 * LANES
    units = [(h, c) for h in range(2) for c in range(tq // wide)]

    def scores(kb, slot, u):
        h, c = u
        start = pl.multiple_of(kb * tq, tq)
        s = jnp.dot(ka_ref[h, pl.ds(start, tq), :], qt_ref[h, :, c * wide:(c + 1) * wide],
                    preferred_element_type=F32)
        for t in range(wide // LANES):
            s_ref[slot, h, c * (wide // LANES) + t, 0:tq, :] = s[:, t * LANES:(t + 1) * LANES]

    def softmax_pv(kb, slot, u, ms, masked, m_out):
        h, c = u
        start = pl.multiple_of(kb * tq, tq)
        alphas = []
        for t in range(wide // LANES):
            q0 = c * wide + t * LANES
            cols = slice(q0, q0 + LANES)
            s = s_ref[slot, h, q0 // LANES, 0:tq, :]
            if masked:
                s = jnp.where(key_idx <= qry_idx + q0, s, -jnp.inf)
            m_old = ms[h][:, cols]
            parts = [s[r:r + 64, :] for r in range(0, tq, 64)]
            while len(parts) > 1:
                parts = [jnp.maximum(parts[n], parts[n + 1]) for n in range(0, len(parts), 2)]
            m_new = jnp.maximum(m_old, jnp.max(parts[0], axis=0, keepdims=True))
            p_ref[h, q0 // LANES, 0:tq, :] = jnp.exp((s - m_new).astype(BF16))
            alphas.append(jnp.exp(m_old - m_new))
            m_out[(h, q0)] = m_new
        cols = slice(c * wide, (c + 1) * wide)
        alpha = jnp.concatenate(alphas, axis=1)
        p = jnp.concatenate([p_ref[h, c * (wide // LANES) + t, 0:tq, :] for t in range(wide // LANES)], axis=1)
        acc_ref[h, :, cols] = alpha * acc_ref[h, :, cols] + jnp.dot(
            vt_ref[h, :, pl.ds(start, tq)], p, preferred_element_type=F32)

    def step(kb, slot, ms, masked, prefetch):
        m_out = {}
        if prefetch:
            for u in units:
                scores(kb + 1, 1 - slot, u)
        for u in units:
            softmax_pv(kb, slot, u, ms, masked, m_out)
        return tuple(jnp.concatenate([m_out[(h, q0)] for q0 in range(0, tq, LANES)], axis=1) for h in range(2))

    def by_parity(kb, ms, masked, prefetch):
        return lax.cond(kb % 2 == 0,
                        lambda ms: step(kb, 0, ms, masked, prefetch),
                        lambda ms: step(kb, 1, ms, masked, prefetch), ms)

    for u in units:
        scores(0, 0, u)
    neg = jnp.full((1, tq), -jnp.inf, F32)
    ms = lax.fori_loop(0, i, lambda kb, ms: by_parity(kb, ms, False, True), (neg, neg))
    by_parity(i, ms, True, False)
    a0 = acc_ref[0]
    a1 = acc_ref[1]
    num = jnp.concatenate([a0[0:dh, :], a1[dh:2 * dh, :]], axis=0)
    den = jnp.concatenate([a0[dh:2 * dh, :], a1[0:dh, :]], axis=0)
    o = jnp.transpose(num / den)
    o_ref[...] = (o * _sigmoid(z_ref[...].astype(F32))).astype(o_ref.dtype)


def _fox_attention(qt, ka, vt, z, tq):
    bsz, heads, feat, seq = qt.shape
    dh = FOX_HEAD_DIM
    pairs = heads // 2
    return pl.pallas_call(
        functools.partial(_fox_kernel, tq=tq, dh=dh),
        grid=(bsz, pairs, seq // tq),
        in_specs=[
            pl.BlockSpec((None, 2, feat, tq), lambda b, p, i: (b, p, 0, i)),
            pl.BlockSpec((None, 2, seq, feat), lambda b, p, i: (b, p, 0, 0)),
            pl.BlockSpec((None, 2, feat, seq), lambda b, p, i: (b, p, 0, 0)),
            pl.BlockSpec((None, tq, 2 * dh), lambda b, p, i: (b, i, p)),
        ],
        out_specs=pl.BlockSpec((None, tq, 2 * dh), lambda b, p, i: (b, i, p)),
        out_shape=jax.ShapeDtypeStruct((bsz, seq, heads * dh), BF16),
        scratch_shapes=[
            pltpu.VMEM((2, 2, tq // LANES, tq + SUBLANES, LANES), F32),
            pltpu.VMEM((2, tq // LANES, tq + 2 * SUBLANES, LANES), BF16),
            pltpu.VMEM((2, feat, tq), F32),
        ],
        compiler_params=_params("parallel", "parallel", "parallel"),
        name="fox_attention",
    )(qt, ka, vt, z)


def _router_kernel(o_ref, wo_ref, x_ref, mod_ref, g_ref, wr_ref, br_ref, xo_ref, h_ref, meta_ref, cnt_ref,
                   carry_ref, *, d, tm, groups, per_group):
    step = pl.program_id(0) * pl.num_programs(1) + pl.program_id(1)

    @pl.when(step == 0)
    def _():
        carry_ref[...] = jnp.zeros(carry_ref.shape, F32)

    x = x_ref[...] + mod_ref[:, 2 * d:3 * d] * jnp.dot(o_ref[...], wo_ref[...], preferred_element_type=F32)
    xo_ref[...] = x
    h = _modulated(x, g_ref[...], mod_ref[:, 4 * d:5 * d], mod_ref[:, 3 * d:4 * d])
    h_ref[...] = h.reshape(h_ref.shape)
    h_hi = h.astype(BF16)
    h_lo = (h - h_hi.astype(F32)).astype(BF16)
    logits = (jnp.dot(h_hi, wr_ref[0], preferred_element_type=F32)
              + (jnp.dot(h_lo, wr_ref[0], preferred_element_type=F32)
                 + jnp.dot(h_hi, wr_ref[1], preferred_element_type=F32))
              + br_ref[...])
    lane = _iota2((tm, LANES), 1)
    neg_inf = jnp.float32(-jnp.inf)
    big = jnp.int32(LANES)

    def first_argmax(vals, mask):
        mv = jnp.where(mask, vals, neg_inf)
        mx = jnp.max(mv, axis=-1, keepdims=True)
        idx = jnp.min(jnp.where(mask & (mv == mx), lane, big), axis=-1, keepdims=True)
        return mx, idx

    gmask = lane < groups
    gmax, gidx = first_argmax(logits, gmask)
    gsum = jnp.sum(jnp.where(gmask, jnp.exp(logits - gmax), 0.0), axis=-1, keepdims=True)
    group_gate = 1.0 / gsum
    lo = groups + gidx * per_group
    emask = (lane >= lo) & (lane < lo + per_group)
    e1max, e1lane = first_argmax(logits, emask)
    e2max, e2lane = first_argmax(logits, emask & (lane != e1lane))
    esum = jnp.sum(jnp.where(emask, jnp.exp(logits - e1max), 0.0), axis=-1, keepdims=True)
    p1 = 1.0 / esum
    p2 = jnp.exp(e2max - e1max) / esum
    psum = p1 + p2
    w1 = group_gate * (p1 / psum)
    w2 = group_gate * (p2 / psum)

    oh1 = (lane == e1lane).astype(F32)
    oh2 = (lane == e2lane).astype(F32)
    strict = jnp.where(_iota2((tm, tm), 0) > _iota2((tm, tm), 1), 1.0, 0.0).astype(BF16)
    c1 = jnp.dot(strict, oh1.astype(BF16), preferred_element_type=F32)
    c2 = jnp.dot(strict, oh2.astype(BF16), preferred_element_type=F32)
    tot1 = jnp.sum(oh1, axis=0, keepdims=True)
    tot2 = jnp.sum(oh2, axis=0, keepdims=True)
    carry = carry_ref[...]
    rank1 = jnp.sum(oh1 * (c1 + carry), axis=-1, keepdims=True)
    rank2 = jnp.sum(oh2 * (c2 + carry + tot1), axis=-1, keepdims=True)
    carry = carry + tot1 + tot2
    carry_ref[...] = carry
    cnt_ref[...] = carry

    e1 = (e1lane - groups).astype(F32)
    e2 = (e2lane - groups).astype(F32)
    mlane = _iota2((tm, 8), 1)
    meta = jnp.where(mlane == 0, e1, 0.0)
    meta = jnp.where(mlane == 1, e2, meta)
    meta = jnp.where(mlane == 2, rank1, meta)
    meta = jnp.where(mlane == 3, rank2, meta)
    meta = jnp.where(mlane == 4, w1, meta)
    meta = jnp.where(mlane == 5, w2, meta)
    meta_ref[...] = meta


def _router(o, w_out, x, mod, gain, w_router, b_router, groups, per_group, tm):
    bsz, seq, d = x.shape
    k = o.shape[-1]
    nb = seq // tm
    return pl.pallas_call(
        functools.partial(_router_kernel, d=d, tm=tm, groups=groups, per_group=per_group),
        grid=(bsz, seq // tm),
        in_specs=[
            pl.BlockSpec((None, tm, k), lambda b, i: (b, i, 0)),
            pl.BlockSpec((k, d), lambda b, i: (0, 0)),
            pl.BlockSpec((None, tm, d), lambda b, i: (b, i, 0)),
            pl.BlockSpec((None, 1, mod.shape[-1]), lambda b, i: (b, 0, 0)),
            pl.BlockSpec((1, d), lambda b, i: (0, 0)),
            pl.BlockSpec((2, d, LANES), lambda b, i: (0, 0, 0)),
            pl.BlockSpec((1, LANES), lambda b, i: (0, 0)),
        ],
        out_specs=[
            pl.BlockSpec((None, tm, d), lambda b, i: (b, i, 0)),
            pl.BlockSpec((tm, SUBLANES, d // SUBLANES), lambda b, i: (b * nb + i, 0, 0)),
            pl.BlockSpec((None, tm, 8), lambda b, i: (b, i, 0)),
            pl.BlockSpec((1, LANES), lambda b, i: (0, 0)),
        ],
        out_shape=[
            jax.ShapeDtypeStruct((bsz, seq, d), F32),
            jax.ShapeDtypeStruct((bsz * seq, SUBLANES, d // SUBLANES), F32),
            jax.ShapeDtypeStruct((bsz, seq, 8), F32),
            jax.ShapeDtypeStruct((1, LANES), F32),
        ],
        scratch_shapes=[pltpu.VMEM((1, LANES), F32)],
        compiler_params=_params("arbitrary", "arbitrary"),
        name="moe_router",
    )(o, w_out, x, mod, gain, w_router, b_router)


DMA_GROUP = 16


def _dispatch_kernel(pad_ref, dest_ref, h_ref, xs_ref, zero_ref, sem, zsem, *, tm, blk, n_exp):
    @pl.when(pl.program_id(0) == 0)
    def _():
        zero_ref[...] = jnp.zeros(zero_ref.shape, zero_ref.dtype)

        def zero_copy(e):
            return pltpu.make_async_copy(zero_ref, xs_ref.at[pl.ds(pad_ref[e] - blk, blk)], zsem)

        def start(e, carry):
            @pl.when(pad_ref[n_exp + e] > 0)
            def _():
                zero_copy(e).start()
            return carry

        def wait(e, carry):
            @pl.when(pad_ref[n_exp + e] > 0)
            def _():
                zero_copy(e).wait()
            return carry

        lax.fori_loop(0, n_exp, start, 0)
        lax.fori_loop(0, n_exp, wait, 0)

        def tail_copy(j):
            return pltpu.make_async_copy(zero_ref, xs_ref.at[pl.ds(j * blk, blk)], zsem)

        def tail_start(j, carry):
            tail_copy(j).start()
            return carry

        def tail_wait(j, carry):
            tail_copy(j).wait()
            return carry

        n_used = pad_ref[2 * n_exp]
        lax.fori_loop(n_used, xs_ref.shape[0] // blk, tail_start, 0)
        lax.fori_loop(n_used, xs_ref.shape[0] // blk, tail_wait, 0)

    def row_copy(r, k):
        return pltpu.make_async_copy(h_ref.at[r], xs_ref.at[dest_ref[r * TOP_K + k]], sem)

    def start(g, carry):
        for s in range(DMA_GROUP):
            for k in range(TOP_K):
                row_copy(g * DMA_GROUP + s, k).start(priority=k % 2)
        return carry

    def wait(g, carry):
        for s in range(DMA_GROUP):
            for k in range(TOP_K):
                row_copy(g * DMA_GROUP + s, k).wait()
        return carry

    lax.fori_loop(0, tm // DMA_GROUP, start, 0)
    lax.fori_loop(0, tm // DMA_GROUP, wait, 0)


def _dispatch(h_tiles, dest, pad_info, rows, tm, blk, n_exp):
    n_tok = h_tiles.shape[0]
    tile = h_tiles.shape[1:]
    return pl.pallas_call(
        functools.partial(_dispatch_kernel, tm=tm, blk=blk, n_exp=n_exp),
        grid_spec=pltpu.PrefetchScalarGridSpec(
            num_scalar_prefetch=1,
            grid=(n_tok // tm,),
            in_specs=[
                pl.BlockSpec((tm * TOP_K,), lambda i, pad: (i,), memory_space=pltpu.SMEM),
                pl.BlockSpec((tm,) + tile, lambda i, pad: (i, 0, 0)),
            ],
            out_specs=pl.BlockSpec(memory_space=pl.ANY),
            scratch_shapes=[
                pltpu.VMEM((blk,) + tile, F32),
                pltpu.SemaphoreType.DMA(()),
                pltpu.SemaphoreType.DMA(()),
            ],
        ),
        out_shape=jax.ShapeDtypeStruct((rows,) + tile, F32),
        compiler_params=_params("arbitrary"),
        name="moe_dispatch",
    )(pad_info, dest, h_tiles)


def _expert_kernel(be_ref, nu_ref, xs_ref, wg_ref, wu_ref, wd_ref, ys_ref, wgb, wub, wdb):
    i = pl.program_id(0)
    prev = be_ref[jnp.maximum(i - 1, 0)]
    fresh = (i == 0) | (be_ref[i] != prev)

    @pl.when(fresh)
    def _():
        wgb[...] = wg_ref[...].astype(BF16)
        wub[...] = wu_ref[...].astype(BF16)
        wdb[...] = wd_ref[...].astype(BF16)

    @pl.when(i < nu_ref[0])
    def _():
        blk = xs_ref.shape[0]
        x = xs_ref[...].reshape(blk, wgb.shape[0]).astype(BF16)
        g = jnp.dot(x, wgb[...], preferred_element_type=F32)
        u = jnp.dot(x, wub[...], preferred_element_type=F32)
        mid = (_silu(g) * u).astype(BF16)
        ys_ref[...] = jnp.dot(mid, wdb[...], preferred_element_type=F32).reshape(ys_ref.shape)

    @pl.when(i >= nu_ref[0])
    def _():
        ys_ref[...] = jnp.zeros(ys_ref.shape, ys_ref.dtype)


def _experts(xs, block_expert, n_used, w_gate, w_up, w_down, layer, blk):
    rows = xs.shape[0]
    tile = xs.shape[1:]
    d, de = w_gate.shape[-2:]
    n_blocks = rows // blk
    return pl.pallas_call(
        _expert_kernel,
        grid_spec=pltpu.PrefetchScalarGridSpec(
            num_scalar_prefetch=2,
            grid=(n_blocks,),
            in_specs=[
                pl.BlockSpec((blk,) + tile, lambda i, be, nu: (jnp.minimum(i, nu[0] - 1), 0, 0)),
                pl.BlockSpec((None, None, d, de), lambda i, be, nu: (layer, be[i], 0, 0)),
                pl.BlockSpec((None, None, d, de), lambda i, be, nu: (layer, be[i], 0, 0)),
                pl.BlockSpec((None, None, de, d), lambda i, be, nu: (layer, be[i], 0, 0)),
            ],
            out_specs=pl.BlockSpec((blk,) + tile, lambda i, be, nu: (i, 0, 0)),
            scratch_shapes=[
                pltpu.VMEM((d, de), BF16),
                pltpu.VMEM((d, de), BF16),
                pltpu.VMEM((de, d), BF16),
            ],
        ),
        out_shape=jax.ShapeDtypeStruct((rows,) + tile, F32),
        compiler_params=_params("arbitrary"),
        name="moe_experts",
    )(block_expert, n_used, xs, w_gate, w_up, w_down)


def _combine_kernel(dest_ref, next_ref, x_ref, mod_ref, meta_ref, ys_ref, o_ref, ybuf, sems, *, tm, d):
    step = pl.program_id(0) * pl.num_programs(1) + pl.program_id(1)
    n_steps = pl.num_programs(0) * pl.num_programs(1)
    slot = step % 2

    def row_copy(idx_ref, buf, r, k):
        return pltpu.make_async_copy(ys_ref.at[idx_ref[r * TOP_K + k]], ybuf.at[buf, k, r], sems.at[buf])

    def start_all(idx_ref, buf):
        def body(g, carry):
            for s in range(DMA_GROUP):
                for k in range(TOP_K):
                    row_copy(idx_ref, buf, g * DMA_GROUP + s, k).start(priority=k % 2)
            return carry
        lax.fori_loop(0, tm // DMA_GROUP, body, 0)

    def wait_all(idx_ref, buf):
        def body(g, carry):
            for s in range(DMA_GROUP):
                for k in range(TOP_K):
                    row_copy(idx_ref, buf, g * DMA_GROUP + s, k).wait()
            return carry
        lax.fori_loop(0, tm // DMA_GROUP, body, 0)

    @pl.when(step == 0)
    def _():
        start_all(dest_ref, 0)

    @pl.when(step + 1 < n_steps)
    def _():
        start_all(next_ref, 1 - slot)

    wait_all(dest_ref, slot)
    meta = meta_ref[...]
    y = meta[:, 4:5] * ybuf[slot, 0].reshape(tm, d) + meta[:, 5:6] * ybuf[slot, 1].reshape(tm, d)
    o_ref[...] = x_ref[...] + mod_ref[:, 5 * d:6 * d] * y


def _combine(dest, x, mod, meta, ys, tm):
    bsz, seq, d = x.shape
    nb = seq // tm
    last = bsz * nb - 1
    return pl.pallas_call(
        functools.partial(_combine_kernel, tm=tm, d=d),
        grid=(bsz, nb),
        in_specs=[
            pl.BlockSpec((tm * TOP_K,), lambda b, i: (b * nb + i,), memory_space=pltpu.SMEM),
            pl.BlockSpec((tm * TOP_K,), lambda b, i: (jnp.minimum(b * nb + i + 1, last),), memory_space=pltpu.SMEM),
            pl.BlockSpec((None, tm, d), lambda b, i: (b, i, 0)),
            pl.BlockSpec((None, 1, mod.shape[-1]), lambda b, i: (b, 0, 0)),
            pl.BlockSpec((None, tm, 8), lambda b, i: (b, i, 0)),
            pl.BlockSpec(memory_space=pl.ANY),
        ],
        out_specs=pl.BlockSpec((None, tm, d), lambda b, i: (b, i, 0)),
        scratch_shapes=[
            pltpu.VMEM((2, TOP_K, tm) + ys.shape[1:], F32),
            pltpu.SemaphoreType.DMA((2,)),
        ],
        out_shape=jax.ShapeDtypeStruct((bsz, seq, d), F32),
        compiler_params=_params("arbitrary", "arbitrary"),
        name="moe_combine",
    )(dest, dest, x, mod, meta, ys)


def _mixer_out_and_moe(o, w_out, x, mod, gain, w_group, b_group, w_expert, b_expert, w_gate, w_up, w_down, layer):
    bsz, seq, d = x.shape
    groups = w_group.shape[1]
    n_exp = w_expert.shape[1]
    per_group = n_exp // groups
    n_tok = bsz * seq
    m = n_tok * TOP_K
    blk = 512
    tm = 512

    pad = LANES - groups - n_exp
    w_router = jnp.concatenate([w_group, w_expert, jnp.zeros((d, pad), F32)], axis=1)
    w_router_hi = w_router.astype(BF16)
    w_router = jnp.stack([w_router_hi, (w_router - w_router_hi.astype(F32)).astype(BF16)])
    b_router = jnp.concatenate([b_group, b_expert, jnp.zeros((pad,), F32)]).reshape(1, LANES)
    x, h, meta, cnt = _router(o, w_out, x, mod, gain, w_router, b_router, groups, per_group, tm)

    counts = cnt[0, groups:groups + n_exp].astype(I32)
    padded = (counts + blk - 1) // blk * blk
    pad_end = jnp.cumsum(padded)
    pad_start = pad_end - padded
    n_blocks = -(-m // blk) + n_exp
    block_start = jnp.arange(n_blocks, dtype=I32) * blk
    block_expert = jnp.minimum(jnp.sum((pad_end[None, :] <= block_start[:, None]).astype(I32), axis=1), n_exp - 1)
    n_used = (pad_end[-1] // blk).astype(I32).reshape(1)
    meta_flat = meta.reshape(n_tok, 8)
    e_idx = meta_flat[:, 0:TOP_K].astype(I32)
    rank = meta_flat[:, TOP_K:2 * TOP_K].astype(I32)
    start_of = jnp.sum(jnp.where(e_idx[..., None] == jnp.arange(n_exp, dtype=I32), pad_start, 0), axis=-1)
    dest = (start_of + rank).reshape(m)

    pad_info = jnp.concatenate([pad_end, padded, n_used]).astype(I32)
    xs = _dispatch(h, dest, pad_info, n_blocks * blk, 2 * tm, blk, n_exp)
    ys = _experts(xs, block_expert, n_used, w_gate, w_up, w_down, layer, blk)
    return _combine(dest, x, mod, meta, ys, tm)


def kernel(x, c, mod_w, mod_b, norm_mix_g, norm_ffn_g, gdn_w_in, gdn_conv_w, gdn_a_log, gdn_dt_bias, gdn_out_norm_g, gdn_w_out, kv_mod_w, kv_mod_b, kv_norm_g, kv_w, kv_forget_b, k_norm_g, fox_w_qz, fox_q_norm_g, fox_w_out, moe_w_group, moe_b_group, moe_w_expert, moe_b_expert, moe_w_gate, moe_w_up, moe_w_down):
    bsz, seq, d = x.shape
    depth = mod_w.shape[0]
    n_a = gdn_w_in.shape[0]
    gdn_heads = gdn_a_log.shape[1]
    gdn_inner = gdn_heads * GDN_HEAD_DIM
    fox_heads = kv_forget_b.shape[0]
    fox_inner = fox_heads * FOX_HEAD_DIM
    tm = 512

    mod_all = _mod_vectors(c, mod_w, mod_b).reshape(depth, bsz, 1, 6 * d)
    kv_mod = _mod_vectors(c, kv_mod_w[None], kv_mod_b[None]).reshape(bsz, 1, 2 * d)
    bd = _block_diag_ones(fox_inner, FOX_HEAD_DIM)

    ka_sh = vt_sh = ft_sh = None
    for layer in range(depth):
        mod = mod_all[layer]
        gain_mix = norm_mix_g[layer].reshape(1, d)
        if layer < n_a:
            w_in = gdn_w_in[layer]
            qkv, sz, ab = _gdn_in_proj(x, mod, gain_mix, w_in[:, :4 * gdn_inner].astype(BF16),
                                       w_in[:, 4 * gdn_inner:].astype(BF16), gdn_conv_w[layer], gdn_heads, tm)
            o = _gdn_core(qkv, sz, ab, gdn_a_log[layer], gdn_dt_bias[layer],
                          gdn_out_norm_g[layer], gdn_heads, 4 * GDN_CHUNK)
            w_out = gdn_w_out[layer]
        else:
            j = layer - n_a
            wqz = fox_w_qz[j]
            qg = jnp.tile(fox_q_norm_g[j], fox_heads).reshape(1, fox_inner)
            qt, z = _fox_qz_proj(x, mod, gain_mix, wqz[:, :fox_inner].astype(BF16),
                                 wqz[:, fox_inner:].astype(BF16), qg, bd, ft_sh, fox_heads, tm)
            o = _fox_attention(qt, ka_sh, vt_sh, z, 512)
            w_out = fox_w_out[j]
        x = _mixer_out_and_moe(o, w_out.astype(BF16), x, mod, norm_ffn_g[layer].reshape(1, d),
                               moe_w_group[layer], moe_b_group[layer], moe_w_expert[layer], moe_b_expert[layer],
                               moe_w_gate, moe_w_up, moe_w_down, layer)
        if layer == n_a - 1:
            kg = jnp.tile(k_norm_g, fox_heads).reshape(1, fox_inner)
            ka_sh, vt_sh, ft_sh = _shared_kv(
                x, kv_mod, kv_norm_g.reshape(1, d), kv_w[:, :fox_inner].astype(BF16),
                kv_w[:, fox_inner:2 * fox_inner].astype(BF16), kv_w[:, 2 * fox_inner:].astype(BF16),
                kv_forget_b.reshape(1, fox_heads), kg, bd, fox_heads, tm)
    return x
```

```python
import functools

import jax
import jax.numpy as jnp
import numpy as np
from jax import lax
from jax.experimental import pallas as pl
from jax.experimental.pallas import tpu as pltpu

F32 = jnp.float32
BF16 = jnp.bfloat16
I32 = jnp.int32

EPS = 1e-6
GDN_CHUNK = 64
GDN_HEAD_DIM = 128
FOX_HEAD_DIM = 64
TOP_K = 2
LANES = 128
SUBLANES = 8
VMEM_LIMIT = 56 * 1024 * 1024
HIGHEST = lax.Precision.HIGHEST

TOKEN_TILE = 512
EXPERT_BLOCK = 512
DISPATCH_TILE = 2 * TOKEN_TILE
FOX_BLOCK = 512
GDN_TIME_BLOCK = 4 * GDN_CHUNK

NT_DIMS = (((1,), (1,)), ((), ()))
TN_DIMS = (((0,), (0,)), ((), ()))


def _params(*sem):
    return pltpu.CompilerParams(dimension_semantics=sem, vmem_limit_bytes=VMEM_LIMIT)


def _sigmoid(x):
    return 1.0 / (1.0 + jnp.exp(-x))


def _silu(x):
    return x * _sigmoid(x)


def _softplus(x):
    return jnp.maximum(x, 0.0) + jnp.log(1.0 + jnp.exp(-jnp.abs(x)))


def _modulated(x, gain, scale, shift):
    ms = jnp.mean(x * x, axis=-1, keepdims=True)
    y = x * lax.rsqrt(ms + EPS)
    return (y * gain) * (1.0 + scale) + shift


def _bdot(a, b, dims=None):
    a = a.astype(BF16)
    b = b.astype(BF16)
    if dims is None:
        return jnp.dot(a, b, preferred_element_type=F32)
    return lax.dot_general(a, b, dims, preferred_element_type=F32)


def _hdot(a, b, dims=None):
    if dims is None:
        return jnp.dot(a, b, preferred_element_type=F32, precision=HIGHEST)
    return lax.dot_general(a, b, dims, preferred_element_type=F32, precision=HIGHEST)


def _iota2(shape, dim):
    return lax.broadcasted_iota(I32, shape, dim)


def _mod_kernel(c_ref, w_ref, b_ref, o_ref):
    c = c_ref[...]
    o_ref[...] = _hdot(_silu(c), w_ref[...]) + b_ref[...]


def _mod_vectors(c, w, b):
    n_layers, d, n = w.shape
    bsz = c.shape[0]
    tn = 1536 if n % 1536 == 0 else n
    return pl.pallas_call(
        _mod_kernel,
        grid=(n_layers, n // tn),
        in_specs=[
            pl.BlockSpec((bsz, d), lambda l, j: (0, 0)),
            pl.BlockSpec((None, d, tn), lambda l, j: (l, 0, j)),
            pl.BlockSpec((None, 1, tn), lambda l, j: (l, 0, j)),
        ],
        out_specs=pl.BlockSpec((None, bsz, tn), lambda l, j: (l, 0, j)),
        out_shape=jax.ShapeDtypeStruct((n_layers, bsz, n), F32),
        compiler_params=_params("parallel", "parallel"),
        name="mod_vectors",
    )(c, w, b.reshape(n_layers, 1, n))


def _gdn_in_kernel(x_ref, mod_ref, g_ref, w_ref, wab_ref, cw_ref, qkv_ref, sz_ref, oab_ref, xpad_ref,
                   *, d, tm, heads, dh, width):
    inner = heads * dh
    halo = 8
    i = pl.program_id(1)

    @pl.when(i == 0)
    def _():
        xpad_ref[0:halo, :] = jnp.zeros((halo, 3 * inner), F32)

    h = _modulated(x_ref[...], g_ref[...], mod_ref[:, d:2 * d], mod_ref[:, 0:d]).astype(BF16)
    oab_ref[...] = jnp.dot(h, wab_ref[...], preferred_element_type=F32)
    z = jnp.dot(h, w_ref[:, 3 * inner:4 * inner], preferred_element_type=F32)
    sz_ref[...] = _silu(z).astype(sz_ref.dtype)
    xpad_ref[halo:halo + tm, :] = jnp.dot(h, w_ref[:, 0:3 * inner], preferred_element_type=F32)
    for j in range(3 * heads):
        cols = slice(j * dh, (j + 1) * dh)
        acc = xpad_ref[halo:halo + tm, cols] * cw_ref[width - 1:width, cols]
        for s in range(1, width):
            acc = acc + xpad_ref[halo - s:halo - s + tm, cols] * cw_ref[width - 1 - s:width - s, cols]
        y = _silu(acc)
        if j < 2 * heads:
            y = y * lax.rsqrt(jnp.sum(y * y, axis=-1, keepdims=True) + EPS)
        if j < heads:
            y = y * (dh ** -0.5)
        qkv_ref[:, cols] = y.astype(qkv_ref.dtype)
    xpad_ref[0:halo, :] = xpad_ref[tm:tm + halo, :]


def _gdn_in_proj(x, mod, gain, w_main, w_ab, conv_w, heads, tm):
    bsz, seq, d = x.shape
    dh = GDN_HEAD_DIM
    inner = heads * dh
    nab = w_ab.shape[1]
    width = conv_w.shape[0]
    return pl.pallas_call(
        functools.partial(_gdn_in_kernel, d=d, tm=tm, heads=heads, dh=dh, width=width),
        grid=(bsz, seq // tm),
        in_specs=[
            pl.BlockSpec((None, tm, d), lambda b, i: (b, i, 0)),
            pl.BlockSpec((None, 1, mod.shape[-1]), lambda b, i: (b, 0, 0)),
            pl.BlockSpec((1, d), lambda b, i: (0, 0)),
            pl.BlockSpec((d, 4 * inner), lambda b, i: (0, 0)),
            pl.BlockSpec((d, nab), lambda b, i: (0, 0)),
            pl.BlockSpec((width, 3 * inner), lambda b, i: (0, 0)),
        ],
        out_specs=[
            pl.BlockSpec((None, tm, 3 * inner), lambda b, i: (b, i, 0)),
            pl.BlockSpec((None, tm, inner), lambda b, i: (b, i, 0)),
            pl.BlockSpec((None, tm, nab), lambda b, i: (b, i, 0)),
        ],
        out_shape=[
            jax.ShapeDtypeStruct((bsz, seq, 3 * inner), BF16),
            jax.ShapeDtypeStruct((bsz, seq, inner), BF16),
            jax.ShapeDtypeStruct((bsz, seq, nab), F32),
        ],
        scratch_shapes=[pltpu.VMEM((8 + tm, 3 * inner), F32)],
        compiler_params=_params("parallel", "arbitrary"),
        name="gdn_in_proj",
    )(x, mod, gain, w_main, w_ab, conv_w)


def _gdn_kernel(q_ref, k_ref, v_ref, sz_ref, ab_ref, alog_ref, dtb_ref, og_ref, o_ref, state_ref,
                *, tb, heads, dh):
    c = GDN_CHUNK
    pairs = heads // 2
    i = pl.program_id(1)

    @pl.when(i == 0)
    def _():
        state_ref[...] = jnp.zeros(state_ref.shape, F32)

    ab = ab_ref[...]
    beta_all = _sigmoid(ab[:, heads:2 * heads])
    g_all = -jnp.exp(alog_ref[...]) * _softplus(ab[:, 0:heads] + dtb_ref[...])
    og = og_ref[...]

    row = _iota2((c, 2 * c), 0)
    lane = _iota2((c, 2 * c), 1)
    first = lane < c
    col = jnp.where(first, lane, lane - c)
    incl = row >= col
    strict = row > col
    eye_p = jnp.where(row == col, 1.0, 0.0).astype(F32)
    tri = jnp.where(_iota2((c, c), 0) >= _iota2((c, c), 1), 1.0, 0.0).astype(F32)
    bd_small = (_iota2((2 * c, 2 * c), 0) < c) == (_iota2((2 * c, 2 * c), 1) < c)
    bd_wide = (_iota2((2 * c, 2 * dh), 0) < c) == (_iota2((2 * c, 2 * dh), 1) < dh)
    first_h = _iota2((pairs, 2 * c), 1) < c
    sel0 = jnp.where(_iota2((pairs, heads), 1) == 2 * _iota2((pairs, heads), 0), 1.0, 0.0).astype(F32)
    sel1 = jnp.where(_iota2((pairs, heads), 1) == 2 * _iota2((pairs, heads), 0) + 1, 1.0, 0.0).astype(F32)

    def block_diag(x, mask):
        return jnp.where(mask, jnp.concatenate([x, x], axis=0), 0.0).astype(BF16)

    def pair_cols(x, p):
        return jnp.concatenate([jnp.broadcast_to(x[:, 2 * p:2 * p + 1], (c, dh)),
                                jnp.broadcast_to(x[:, 2 * p + 1:2 * p + 2], (c, dh))], axis=1)

    units = [(ci, p) for ci in range(tb // c) for p in range(pairs)]
    gcums, glasts, pk, tt, attn, w_u, qg, kdec = {}, {}, {}, {}, {}, {}, {}, {}

    for ci in range(tb // c):
        r0 = ci * c
        gcum = _hdot(tri, g_all[r0:r0 + c, :])
        gc2 = jnp.concatenate([gcum, gcum], axis=0)
        gt = jnp.where(first_h, _hdot(sel0, gc2, NT_DIMS), _hdot(sel1, gc2, NT_DIMS))
        glast = gcum[c - 1:c, :]
        gcums[ci] = gcum
        glasts[ci] = glast
        for p in range(pairs):
            u = (ci, p)
            cols = slice(2 * p * dh, (2 * p + 2) * dh)
            kp = k_ref[r0:r0 + c, cols].astype(F32)
            qp = q_ref[r0:r0 + c, cols].astype(F32)
            vp = v_ref[r0:r0 + c, cols].astype(F32)
            gcol = jnp.where(first, gcum[:, 2 * p:2 * p + 1], gcum[:, 2 * p + 1:2 * p + 2])
            decay = jnp.exp(jnp.where(incl, gcol - gt[p:p + 1, :], -jnp.inf))
            beta2 = pair_cols(beta_all[r0:r0 + c, :], p)
            gcum2 = pair_cols(gcum, p)
            egc2 = jnp.exp(gcum2)
            kb = kp * beta2
            y = block_diag(kp, bd_wide)
            kq = _bdot(jnp.concatenate([kb, qp], axis=0), y, NT_DIMS)
            a_mat = jnp.where(strict, kq[0:c, :] * decay, 0.0)
            attn[u] = kq[c:2 * c, :] * decay
            pk[u] = -a_mat
            tt[u] = eye_p - a_mat
            kbg = kb * egc2
            vb = vp * beta2
            w_u[u] = jnp.concatenate(
                [jnp.concatenate([kbg[:, 0:dh], vb[:, 0:dh]], axis=1),
                 jnp.concatenate([kbg[:, dh:2 * dh], vb[:, dh:2 * dh]], axis=1)], axis=0).astype(BF16)
            qg[u] = qp * egc2
            kdec[u] = kp * jnp.exp(pair_cols(jnp.broadcast_to(glast, (c, heads)), p) - gcum2)

    span = 2
    while span <= c:
        last = span == c
        for u in units:
            bd = block_diag(pk[u], bd_small)
            if span == 2:
                pk[u] = _bdot(pk[u], bd)
            elif last:
                tt[u] = tt[u] + _bdot(tt[u], bd)
            else:
                both = _bdot(jnp.concatenate([pk[u], tt[u]], axis=0), bd)
                pk[u] = both[0:c, :]
                tt[u] = tt[u] + both[c:2 * c, :]
        span *= 2
    for u in units:
        t = tt[u]
        lhs = jnp.concatenate([jnp.where(first, t, 0.0), jnp.where(first, 0.0, t)], axis=0)
        w_u[u] = _bdot(lhs, w_u[u])

    for ci in range(tb // c):
        r0 = ci * c
        for p in range(pairs):
            u = (ci, p)
            wu = w_u[u]
            st, wq, vn = [], [], []
            for s in range(2):
                h = 2 * p + s
                st.append(state_ref[h])
                wq.append(_bdot(jnp.concatenate([wu[s * c:(s + 1) * c, 0:dh], qg[u][:, s * dh:(s + 1) * dh]], axis=0),
                                st[s]))
                vn.append(wu[s * c:(s + 1) * c, dh:2 * dh] - wq[s][0:c, :])
            vn2 = jnp.concatenate(vn, axis=0).astype(BF16)
            for s in range(2):
                h = 2 * p + s
                am = jnp.where(first, attn[u], 0.0) if s == 0 else jnp.where(first, 0.0, attn[u])
                o = wq[s][c:2 * c, :] + _bdot(am, vn2)
                gl = glasts[ci][:, h:h + 1]
                state_ref[h] = st[s] * jnp.exp(gl) + _bdot(kdec[u][:, s * dh:(s + 1) * dh], vn[s], TN_DIMS)
                on = o * lax.rsqrt(jnp.mean(o * o, axis=-1, keepdims=True) + EPS) * og
                szh = sz_ref[r0:r0 + c, h * dh:(h + 1) * dh].astype(F32)
                o_ref[r0:r0 + c, h * dh:(h + 1) * dh] = (on * szh).astype(o_ref.dtype)


def _gdn_core(qkv, sz, ab, a_log, dt_bias, out_g, heads, tb):
    bsz, seq, _ = qkv.shape
    dh = GDN_HEAD_DIM
    inner = heads * dh
    return pl.pallas_call(
        functools.partial(_gdn_kernel, tb=tb, heads=heads, dh=dh),
        grid=(bsz, seq // tb),
        in_specs=[
            pl.BlockSpec((None, tb, inner), lambda b, i: (b, i, 0)),
            pl.BlockSpec((None, tb, inner), lambda b, i: (b, i, 1)),
            pl.BlockSpec((None, tb, inner), lambda b, i: (b, i, 2)),
            pl.BlockSpec((None, tb, inner), lambda b, i: (b, i, 0)),
            pl.BlockSpec((None, tb, 2 * heads), lambda b, i: (b, i, 0)),
            pl.BlockSpec((1, heads), lambda b, i: (0, 0)),
            pl.BlockSpec((1, heads), lambda b, i: (0, 0)),
            pl.BlockSpec((1, dh), lambda b, i: (0, 0)),
        ],
        out_specs=pl.BlockSpec((None, tb, inner), lambda b, i: (b, i, 0)),
        out_shape=jax.ShapeDtypeStruct((bsz, seq, inner), BF16),
        scratch_shapes=[pltpu.VMEM((heads, dh, dh), F32)],
        compiler_params=_params("parallel", "arbitrary"),
        name="gdn_core",
    )(qkv, qkv, qkv, sz, ab, a_log.reshape(1, heads), dt_bias.reshape(1, heads), out_g.reshape(1, dh))


def _head_rms(x, bd, gain, dh):
    w = bd.shape[0]
    sq = (x * x).astype(BF16)
    ss = jnp.concatenate([jnp.dot(sq[:, c:c + w], bd, preferred_element_type=F32)
                          for c in range(0, x.shape[1], w)], axis=1)
    return x * lax.rsqrt(ss * (1.0 / dh) + EPS) * gain


def _block_diag_ones(n, blk):
    r = jnp.arange(n, dtype=I32) // blk
    return (r[:, None] == r[None, :]).astype(BF16)


N_SPLIT = 3


def _bias_base(head, dh):
    return dh if head % 2 == 0 else 0


def _split3(x):
    hi = x.astype(BF16)
    r = x - hi.astype(F32)
    mid = r.astype(BF16)
    lo = (r - mid.astype(F32)).astype(BF16)
    return hi, mid, lo


def _key_bias_placement(heads, dh):
    place = np.zeros((N_SPLIT, heads, heads * 2 * dh), np.float32)
    for j in range(N_SPLIT):
        for h in range(heads):
            place[j, h, h * 2 * dh + _bias_base(h, dh) + j] = 1.0
    return jnp.asarray(place, BF16)


def _kv_kernel(x_ref, mod_ref, g_ref, wk_ref, wv_ref, wf_ref, fb_ref, kg_ref, bd_ref, place_ref,
               ka_ref, vt_ref, ft_ref, carry_ref, *, d, tm, heads, dh):
    i = pl.program_id(1)

    @pl.when(i == 0)
    def _():
        carry_ref[...] = jnp.zeros(carry_ref.shape, F32)

    h = _modulated(x_ref[...], g_ref[...], mod_ref[:, d:2 * d], mod_ref[:, 0:d]).astype(BF16)
    kraw = jnp.dot(h, wk_ref[...], preferred_element_type=F32)
    kn = _head_rms(kraw, bd_ref[...], kg_ref[...], dh)
    v = jnp.dot(h, wv_ref[...], preferred_element_type=F32).astype(BF16)
    f = jnp.dot(h, wf_ref[...], preferred_element_type=F32) + fb_ref[...]
    log_f = -_softplus(-f)
    tri = jnp.where(_iota2((tm, tm), 0) >= _iota2((tm, tm), 1), 1.0, 0.0).astype(F32)
    fcum = _hdot(tri, log_f) + carry_ref[...]
    carry_ref[...] = fcum[tm - 1:tm, :]
    eye_h = jnp.where(_iota2((heads, heads), 0) == _iota2((heads, heads), 1), 1.0, 0.0).astype(F32)
    fcum_t = _hdot(eye_h, fcum, NT_DIMS)
    for p in range(heads // 2):
        ft_ref[p] = fcum_t[2 * p:2 * p + 2, :]

    pieces = _split3(-fcum)
    bias = jnp.dot(pieces[0], place_ref[0], preferred_element_type=F32)
    for j in range(1, N_SPLIT):
        bias = bias + jnp.dot(pieces[j], place_ref[j], preferred_element_type=F32)
    lane = _iota2((tm, 2 * dh), 1)
    for hh in range(heads):
        p = hh // 2
        base = _bias_base(hh, dh)
        real = (lane < dh) if hh % 2 == 0 else (lane >= dh)
        ones = (lane >= base + N_SPLIT) & (lane < base + 2 * N_SPLIT)
        blk = jnp.where(real, kn[:, 2 * p * dh:(2 * p + 2) * dh],
                        jnp.where(ones, 1.0, bias[:, hh * 2 * dh:(hh + 1) * 2 * dh]))
        ka_ref[hh] = blk.astype(ka_ref.dtype)

    eye = jnp.where(_iota2((2 * dh, 2 * dh), 0) == _iota2((2 * dh, 2 * dh), 1), 1.0, 0.0).astype(BF16)
    rowi = _iota2((2 * dh, tm), 0)
    for p in range(heads // 2):
        vt = lax.dot_general(eye, v[:, 2 * p * dh:(2 * p + 2) * dh], NT_DIMS, preferred_element_type=F32)
        vt_ref[2 * p] = jnp.where(rowi < dh, vt, 1.0).astype(vt_ref.dtype)
        vt_ref[2 * p + 1] = jnp.where(rowi >= dh, vt, 1.0).astype(vt_ref.dtype)


def _shared_kv(x, mod, gain, wk, wv, wf, fb, kgain, bd, heads, tm):
    bsz, seq, d = x.shape
    dh = FOX_HEAD_DIM
    inner = heads * dh
    pairs = heads // 2
    place = _key_bias_placement(heads, dh)
    return pl.pallas_call(
        functools.partial(_kv_kernel, d=d, tm=tm, heads=heads, dh=dh),
        grid=(bsz, seq // tm),
        in_specs=[
            pl.BlockSpec((None, tm, d), lambda b, i: (b, i, 0)),
            pl.BlockSpec((None, 1, mod.shape[-1]), lambda b, i: (b, 0, 0)),
            pl.BlockSpec((1, d), lambda b, i: (0, 0)),
            pl.BlockSpec((d, inner), lambda b, i: (0, 0)),
            pl.BlockSpec((d, inner), lambda b, i: (0, 0)),
            pl.BlockSpec((d, heads), lambda b, i: (0, 0)),
            pl.BlockSpec((1, heads), lambda b, i: (0, 0)),
            pl.BlockSpec((1, inner), lambda b, i: (0, 0)),
            pl.BlockSpec(bd.shape, lambda b, i: (0, 0)),
            pl.BlockSpec(place.shape, lambda b, i: (0, 0, 0)),
        ],
        out_specs=[
            pl.BlockSpec((None, heads, tm, 2 * dh), lambda b, i: (b, 0, i, 0)),
            pl.BlockSpec((None, heads, 2 * dh, tm), lambda b, i: (b, 0, 0, i)),
            pl.BlockSpec((None, pairs, 2, tm), lambda b, i: (b, 0, 0, i)),
        ],
        out_shape=[
            jax.ShapeDtypeStruct((bsz, heads, seq, 2 * dh), BF16),
            jax.ShapeDtypeStruct((bsz, heads, 2 * dh, seq), BF16),
            jax.ShapeDtypeStruct((bsz, pairs, 2, seq), F32),
        ],
        scratch_shapes=[pltpu.VMEM((1, heads), F32)],
        compiler_params=_params("parallel", "arbitrary"),
        name="shared_kv",
    )(x, mod, gain, wk, wv, wf, fb, kgain, bd, place)


def _fox_qz_kernel(x_ref, mod_ref, g_ref, wq_ref, wz_ref, qg_ref, bd_ref, ft_ref, qt_ref, z_ref,
                   *, d, dh, heads, tm):
    h = _modulated(x_ref[...], g_ref[...], mod_ref[:, d:2 * d], mod_ref[:, 0:d]).astype(BF16)
    qraw = jnp.dot(h, wq_ref[...], preferred_element_type=F32)
    qn = (_head_rms(qraw, bd_ref[...], qg_ref[...], dh) * (dh ** -0.5)).astype(BF16)
    z_ref[...] = jnp.dot(h, wz_ref[...], preferred_element_type=F32).astype(z_ref.dtype)
    eye = jnp.where(_iota2((2 * dh, 2 * dh), 0) == _iota2((2 * dh, 2 * dh), 1), 1.0, 0.0).astype(BF16)
    rowi = _iota2((2 * dh, tm), 0)
    for p in range(heads // 2):
        qt = lax.dot_general(eye, qn[:, 2 * p * dh:(2 * p + 2) * dh], NT_DIMS, preferred_element_type=F32)
        for s in range(2):
            hh = 2 * p + s
            base = _bias_base(hh, dh)
            real = (rowi < dh) if s == 0 else (rowi >= dh)
            aug = jnp.where(real, qt, 0.0)
            aug = jnp.where((rowi >= base) & (rowi < base + N_SPLIT), 1.0, aug)
            pieces = _split3(ft_ref[p, s:s + 1, :])
            for j in range(N_SPLIT):
                aug = jnp.where(rowi == base + N_SPLIT + j, pieces[j].astype(F32), aug)
            qt_ref[hh] = aug.astype(qt_ref.dtype)


def _fox_qz_proj(x, mod, gain, wq, wz, qgain, bd, ft, heads, tm):
    bsz, seq, d = x.shape
    dh = FOX_HEAD_DIM
    inner = wq.shape[1]
    pairs = heads // 2
    return pl.pallas_call(
        functools.partial(_fox_qz_kernel, d=d, dh=dh, heads=heads, tm=tm),
        grid=(bsz, seq // tm),
        in_specs=[
            pl.BlockSpec((None, tm, d), lambda b, i: (b, i, 0)),
            pl.BlockSpec((None, 1, mod.shape[-1]), lambda b, i: (b, 0, 0)),
            pl.BlockSpec((1, d), lambda b, i: (0, 0)),
            pl.BlockSpec((d, inner), lambda b, i: (0, 0)),
            pl.BlockSpec((d, inner), lambda b, i: (0, 0)),
            pl.BlockSpec((1, inner), lambda b, i: (0, 0)),
            pl.BlockSpec(bd.shape, lambda b, i: (0, 0)),
            pl.BlockSpec((None, pairs, 2, tm), lambda b, i: (b, 0, 0, i)),
        ],
        out_specs=[
            pl.BlockSpec((None, heads, 2 * dh, tm), lambda b, i: (b, 0, 0, i)),
            pl.BlockSpec((None, tm, inner), lambda b, i: (b, i, 0)),
        ],
        out_shape=[
            jax.ShapeDtypeStruct((bsz, heads, 2 * dh, seq), BF16),
            jax.ShapeDtypeStruct((bsz, seq, inner), BF16),
        ],
        compiler_params=_params("parallel", "parallel"),
        name="fox_qz_proj",
    )(x, mod, gain, wq, wz, qgain, bd, ft)


def _fox_kernel(qt_ref, ka_ref, vt_ref, z_ref, o_ref, s_ref, p_ref, acc_ref, *, tq, dh):
    i = pl.program_id(2)
    acc_ref[...] = jnp.zeros(acc_ref.shape, F32)
    key_idx = _iota2((tq, LANES), 0)
    qry_idx = _iota2((tq, LANES), 1)

    wide = 2 * LANES
    units = [(h, c) for h in range(2) for c in range(tq // wide)]

    def scores(kb, slot, u):
        h, c = u
        start = pl.multiple_of(kb * tq, tq)
        s = jnp.dot(ka_ref[h, pl.ds(start, tq), :], qt_ref[h, :, c * wide:(c + 1) * wide],
                    preferred_element_type=F32)
        for t in range(wide // LANES):
            s_ref[slot, h, c * (wide // LANES) + t] = s[:, t * LANES:(t + 1) * LANES]

    def softmax_pv(kb, slot, u, ms, masked, m_out):
        h, c = u
        start = pl.multiple_of(kb * tq, tq)
        alphas = []
        for t in range(wide // LANES):
            q0 = c * wide + t * LANES
            cols = slice(q0, q0 + LANES)
            s = s_ref[slot, h, q0 // LANES]
            if masked:
                s = jnp.where(key_idx <= qry_idx + q0, s, -jnp.inf)
            m_old = ms[h][:, cols]
            parts = [s[r:r + 64, :] for r in range(0, tq, 64)]
            while len(parts) > 1:
                parts = [jnp.maximum(parts[n], parts[n + 1]) for n in range(0, len(parts), 2)]
            m_new = jnp.maximum(m_old, jnp.max(parts[0], axis=0, keepdims=True))
            p_ref[h, q0 // LANES] = jnp.exp((s - m_new).astype(BF16))
            alphas.append(jnp.exp(m_old - m_new))
            m_out[(h, q0)] = m_new
        cols = slice(c * wide, (c + 1) * wide)
        alpha = jnp.concatenate(alphas, axis=1)
        p = jnp.concatenate([p_ref[h, c * (wide // LANES) + t] for t in range(wide // LANES)], axis=1)
        acc_ref[h, :, cols] = alpha * acc_ref[h, :, cols] + jnp.dot(
            vt_ref[h, :, pl.ds(start, tq)], p, preferred_element_type=F32)

    def step(kb, slot, ms, masked, prefetch):
        m_out = {}
        if prefetch:
            for u in units:
                scores(kb + 1, 1 - slot, u)
        for u in units:
            softmax_pv(kb, slot, u, ms, masked, m_out)
        return tuple(jnp.concatenate([m_out[(h, q0)] for q0 in range(0, tq, LANES)], axis=1) for h in range(2))

    def by_parity(kb, ms, masked, prefetch):
        return lax.cond(kb % 2 == 0,
                        lambda ms: step(kb, 0, ms, masked, prefetch),
                        lambda ms: step(kb, 1, ms, masked, prefetch), ms)

    for u in units:
        scores(0, 0, u)
    neg = jnp.full((1, tq), -jnp.inf, F32)
    ms = lax.fori_loop(0, i, lambda kb, ms: by_parity(kb, ms, False, True), (neg, neg))
    by_parity(i, ms, True, False)
    a0 = acc_ref[0]
    a1 = acc_ref[1]
    num = jnp.concatenate([a0[0:dh, :], a1[dh:2 * dh, :]], axis=0)
    den = jnp.concatenate([a0[dh:2 * dh, :], a1[0:dh, :]], axis=0)
    o = jnp.transpose(num / den)
    o_ref[...] = (o * _sigmoid(z_ref[...].astype(F32))).astype(o_ref.dtype)


def _fox_attention(qt, ka, vt, z, tq):
    bsz, heads, feat, seq = qt.shape
    dh = FOX_HEAD_DIM
    pairs = heads // 2
    return pl.pallas_call(
        functools.partial(_fox_kernel, tq=tq, dh=dh),
        grid=(bsz, pairs, seq // tq),
        in_specs=[
            pl.BlockSpec((None, 2, feat, tq), lambda b, p, i: (b, p, 0, i)),
            pl.BlockSpec((None, 2, seq, feat), lambda b, p, i: (b, p, 0, 0)),
            pl.BlockSpec((None, 2, feat, seq), lambda b, p, i: (b, p, 0, 0)),
            pl.BlockSpec((None, tq, 2 * dh), lambda b, p, i: (b, i, p)),
        ],
        out_specs=pl.BlockSpec((None, tq, 2 * dh), lambda b, p, i: (b, i, p)),
        out_shape=jax.ShapeDtypeStruct((bsz, seq, heads * dh), BF16),
        scratch_shapes=[
            pltpu.VMEM((2, 2, tq // LANES, tq, LANES), F32),
            pltpu.VMEM((2, tq // LANES, tq, LANES), BF16),
            pltpu.VMEM((2, feat, tq), F32),
        ],
        compiler_params=_params("parallel", "parallel", "parallel"),
        name="fox_attention",
    )(qt, ka, vt, z)


def _router_kernel(o_ref, wo_ref, x_ref, mod_ref, g_ref, wr_ref, br_ref, xo_ref, h_ref, meta_ref, cnt_ref,
                   carry_ref, *, d, tm, groups, per_group):
    step = pl.program_id(0) * pl.num_programs(1) + pl.program_id(1)

    @pl.when(step == 0)
    def _():
        carry_ref[...] = jnp.zeros(carry_ref.shape, F32)

    x = x_ref[...] + mod_ref[:, 2 * d:3 * d] * jnp.dot(o_ref[...], wo_ref[...], preferred_element_type=F32)
    xo_ref[...] = x
    h = _modulated(x, g_ref[...], mod_ref[:, 4 * d:5 * d], mod_ref[:, 3 * d:4 * d])
    h_ref[...] = h.reshape(h_ref.shape)
    h_hi = h.astype(BF16)
    h_lo = (h - h_hi.astype(F32)).astype(BF16)
    logits = (jnp.dot(h_hi, wr_ref[0], preferred_element_type=F32)
              + (jnp.dot(h_lo, wr_ref[0], preferred_element_type=F32)
                 + jnp.dot(h_hi, wr_ref[1], preferred_element_type=F32))
              + br_ref[...])
    lane = _iota2((tm, LANES), 1)
    neg_inf = jnp.float32(-jnp.inf)
    big = jnp.int32(LANES)

    def first_argmax(vals, mask):
        mv = jnp.where(mask, vals, neg_inf)
        mx = jnp.max(mv, axis=-1, keepdims=True)
        idx = jnp.min(jnp.where(mask & (mv == mx), lane, big), axis=-1, keepdims=True)
        return mx, idx

    gmask = lane < groups
    gmax, gidx = first_argmax(logits, gmask)
    gsum = jnp.sum(jnp.where(gmask, jnp.exp(logits - gmax), 0.0), axis=-1, keepdims=True)
    group_gate = 1.0 / gsum
    lo = groups + gidx * per_group
    emask = (lane >= lo) & (lane < lo + per_group)
    e1max, e1lane = first_argmax(logits, emask)
    e2max, e2lane = first_argmax(logits, emask & (lane != e1lane))
    esum = jnp.sum(jnp.where(emask, jnp.exp(logits - e1max), 0.0), axis=-1, keepdims=True)
    p1 = 1.0 / esum
    p2 = jnp.exp(e2max - e1max) / esum
    psum = p1 + p2
    w1 = group_gate * (p1 / psum)
    w2 = group_gate * (p2 / psum)

    oh1 = (lane == e1lane).astype(F32)
    oh2 = (lane == e2lane).astype(F32)
    strict = jnp.where(_iota2((tm, tm), 0) > _iota2((tm, tm), 1), 1.0, 0.0).astype(BF16)
    c1 = jnp.dot(strict, oh1.astype(BF16), preferred_element_type=F32)
    c2 = jnp.dot(strict, oh2.astype(BF16), preferred_element_type=F32)
    tot1 = jnp.sum(oh1, axis=0, keepdims=True)
    tot2 = jnp.sum(oh2, axis=0, keepdims=True)
    carry = carry_ref[...]
    rank1 = jnp.sum(oh1 * (c1 + carry), axis=-1, keepdims=True)
    rank2 = jnp.sum(oh2 * (c2 + carry + tot1), axis=-1, keepdims=True)
    carry = carry + tot1 + tot2
    carry_ref[...] = carry
    cnt_ref[...] = carry

    e1 = (e1lane - groups).astype(F32)
    e2 = (e2lane - groups).astype(F32)
    mlane = _iota2((tm, 8), 1)
    meta = jnp.where(mlane == 0, e1, 0.0)
    meta = jnp.where(mlane == 1, e2, meta)
    meta = jnp.where(mlane == 2, rank1, meta)
    meta = jnp.where(mlane == 3, rank2, meta)
    meta = jnp.where(mlane == 4, w1, meta)
    meta = jnp.where(mlane == 5, w2, meta)
    meta_ref[...] = meta


def _router(o, w_out, x, mod, gain, w_router, b_router, groups, per_group, tm):
    bsz, seq, d = x.shape
    k = o.shape[-1]
    nb = seq // tm
    return pl.pallas_call(
        functools.partial(_router_kernel, d=d, tm=tm, groups=groups, per_group=per_group),
        grid=(bsz, seq // tm),
        in_specs=[
            pl.BlockSpec((None, tm, k), lambda b, i: (b, i, 0)),
            pl.BlockSpec((k, d), lambda b, i: (0, 0)),
            pl.BlockSpec((None, tm, d), lambda b, i: (b, i, 0)),
            pl.BlockSpec((None, 1, mod.shape[-1]), lambda b, i: (b, 0, 0)),
            pl.BlockSpec((1, d), lambda b, i: (0, 0)),
            pl.BlockSpec((2, d, LANES), lambda b, i: (0, 0, 0)),
            pl.BlockSpec((1, LANES), lambda b, i: (0, 0)),
        ],
        out_specs=[
            pl.BlockSpec((None, tm, d), lambda b, i: (b, i, 0)),
            pl.BlockSpec((tm, SUBLANES, d // SUBLANES), lambda b, i: (b * nb + i, 0, 0)),
            pl.BlockSpec((None, tm, 8), lambda b, i: (b, i, 0)),
            pl.BlockSpec((1, LANES), lambda b, i: (0, 0)),
        ],
        out_shape=[
            jax.ShapeDtypeStruct((bsz, seq, d), F32),
            jax.ShapeDtypeStruct((bsz * seq, SUBLANES, d // SUBLANES), F32),
            jax.ShapeDtypeStruct((bsz, seq, 8), F32),
            jax.ShapeDtypeStruct((1, LANES), F32),
        ],
        scratch_shapes=[pltpu.VMEM((1, LANES), F32)],
        compiler_params=_params("arbitrary", "arbitrary"),
        name="moe_router",
    )(o, w_out, x, mod, gain, w_router, b_router)


DMA_GROUP = 16


def _dispatch_kernel(pad_ref, dest_ref, h_ref, xs_ref, zero_ref, sem, zsem, *, tm, blk, n_exp):
    @pl.when(pl.program_id(0) == 0)
    def _():
        zero_ref[...] = jnp.zeros(zero_ref.shape, zero_ref.dtype)

        def zero_copy(e):
            return pltpu.make_async_copy(zero_ref, xs_ref.at[pl.ds(pad_ref[e] - blk, blk)], zsem)

        def start(e, carry):
            @pl.when(pad_ref[n_exp + e] > 0)
            def _():
                zero_copy(e).start()
            return carry

        def wait(e, carry):
            @pl.when(pad_ref[n_exp + e] > 0)
            def _():
                zero_copy(e).wait()
            return carry

        lax.fori_loop(0, n_exp, start, 0)
        lax.fori_loop(0, n_exp, wait, 0)

        def tail_copy(j):
            return pltpu.make_async_copy(zero_ref, xs_ref.at[pl.ds(j * blk, blk)], zsem)

        def tail_start(j, carry):
            tail_copy(j).start()
            return carry

        def tail_wait(j, carry):
            tail_copy(j).wait()
            return carry

        n_used = pad_ref[2 * n_exp]
        lax.fori_loop(n_used, xs_ref.shape[0] // blk, tail_start, 0)
        lax.fori_loop(n_used, xs_ref.shape[0] // blk, tail_wait, 0)

    def row_copy(r, k):
        return pltpu.make_async_copy(h_ref.at[r], xs_ref.at[dest_ref[r * TOP_K + k]], sem)

    def start(g, carry):
        for s in range(DMA_GROUP):
            for k in range(TOP_K):
                row_copy(g * DMA_GROUP + s, k).start(priority=k % 2)
        return carry

    def wait(g, carry):
        for s in range(DMA_GROUP):
            for k in range(TOP_K):
                row_copy(g * DMA_GROUP + s, k).wait()
        return carry

    lax.fori_loop(0, tm // DMA_GROUP, start, 0)
    lax.fori_loop(0, tm // DMA_GROUP, wait, 0)


def _dispatch(h_tiles, dest, pad_info, rows, tm, blk, n_exp):
    n_tok = h_tiles.shape[0]
    tile = h_tiles.shape[1:]
    return pl.pallas_call(
        functools.partial(_dispatch_kernel, tm=tm, blk=blk, n_exp=n_exp),
        grid_spec=pltpu.PrefetchScalarGridSpec(
            num_scalar_prefetch=1,
            grid=(n_tok // tm,),
            in_specs=[
                pl.BlockSpec((tm * TOP_K,), lambda i, pad: (i,), memory_space=pltpu.SMEM),
                pl.BlockSpec((tm,) + tile, lambda i, pad: (i, 0, 0)),
            ],
            out_specs=pl.BlockSpec(memory_space=pl.ANY),
            scratch_shapes=[
                pltpu.VMEM((blk,) + tile, F32),
                pltpu.SemaphoreType.DMA(()),
                pltpu.SemaphoreType.DMA(()),
            ],
        ),
        out_shape=jax.ShapeDtypeStruct((rows,) + tile, F32),
        compiler_params=_params("arbitrary"),
        name="moe_dispatch",
    )(pad_info, dest, h_tiles)


def _expert_kernel(be_ref, nu_ref, xs_ref, wg_ref, wu_ref, wd_ref, ys_ref, wgb, wub, wdb):
    i = pl.program_id(0)
    prev = be_ref[jnp.maximum(i - 1, 0)]
    fresh = (i == 0) | (be_ref[i] != prev)

    @pl.when(fresh)
    def _():
        wgb[...] = wg_ref[...].astype(BF16)
        wub[...] = wu_ref[...].astype(BF16)
        wdb[...] = wd_ref[...].astype(BF16)

    @pl.when(i < nu_ref[0])
    def _():
        blk = xs_ref.shape[0]
        x = xs_ref[...].reshape(blk, wgb.shape[0]).astype(BF16)
        g = jnp.dot(x, wgb[...], preferred_element_type=F32)
        u = jnp.dot(x, wub[...], preferred_element_type=F32)
        mid = (_silu(g) * u).astype(BF16)
        ys_ref[...] = jnp.dot(mid, wdb[...], preferred_element_type=F32).reshape(ys_ref.shape)

    @pl.when(i >= nu_ref[0])
    def _():
        ys_ref[...] = jnp.zeros(ys_ref.shape, ys_ref.dtype)


def _experts(xs, block_expert, n_used, w_gate, w_up, w_down, layer, blk):
    rows = xs.shape[0]
    tile = xs.shape[1:]
    d, de = w_gate.shape[-2:]
    n_blocks = rows // blk
    return pl.pallas_call(
        _expert_kernel,
        grid_spec=pltpu.PrefetchScalarGridSpec(
            num_scalar_prefetch=2,
            grid=(n_blocks,),
            in_specs=[
                pl.BlockSpec((blk,) + tile, lambda i, be, nu: (jnp.minimum(i, nu[0] - 1), 0, 0)),
                pl.BlockSpec((None, None, d, de), lambda i, be, nu: (layer, be[i], 0, 0)),
                pl.BlockSpec((None, None, d, de), lambda i, be, nu: (layer, be[i], 0, 0)),
                pl.BlockSpec((None, None, de, d), lambda i, be, nu: (layer, be[i], 0, 0)),
            ],
            out_specs=pl.BlockSpec((blk,) + tile, lambda i, be, nu: (i, 0, 0)),
            scratch_shapes=[
                pltpu.VMEM((d, de), BF16),
                pltpu.VMEM((d, de), BF16),
                pltpu.VMEM((de, d), BF16),
            ],
        ),
        out_shape=jax.ShapeDtypeStruct((rows,) + tile, F32),
        compiler_params=_params("arbitrary"),
        name="moe_experts",
    )(block_expert, n_used, xs, w_gate, w_up, w_down)


def _combine_kernel(dest_ref, next_ref, x_ref, mod_ref, meta_ref, ys_ref, o_ref, ybuf, sems, *, tm, d):
    step = pl.program_id(0) * pl.num_programs(1) + pl.program_id(1)
    n_steps = pl.num_programs(0) * pl.num_programs(1)
    slot = step % 2

    def row_copy(idx_ref, buf, r, k):
        return pltpu.make_async_copy(ys_ref.at[idx_ref[r * TOP_K + k]], ybuf.at[buf, k, r], sems.at[buf])

    def start_all(idx_ref, buf):
        def body(g, carry):
            for s in range(DMA_GROUP):
                for k in range(TOP_K):
                    row_copy(idx_ref, buf, g * DMA_GROUP + s, k).start(priority=k % 2)
            return carry
        lax.fori_loop(0, tm // DMA_GROUP, body, 0)

    def wait_all(idx_ref, buf):
        def body(g, carry):
            for s in range(DMA_GROUP):
                for k in range(TOP_K):
                    row_copy(idx_ref, buf, g * DMA_GROUP + s, k).wait()
            return carry
        lax.fori_loop(0, tm // DMA_GROUP, body, 0)

    @pl.when(step == 0)
    def _():
        start_all(dest_ref, 0)

    @pl.when(step + 1 < n_steps)
    def _():
        start_all(next_ref, 1 - slot)

    wait_all(dest_ref, slot)
    meta = meta_ref[...]
    y = meta[:, 4:5] * ybuf[slot, 0].reshape(tm, d) + meta[:, 5:6] * ybuf[slot, 1].reshape(tm, d)
    o_ref[...] = x_ref[...] + mod_ref[:, 5 * d:6 * d] * y


def _combine(dest, x, mod, meta, ys, tm):
    bsz, seq, d = x.shape
    nb = seq // tm
    last = bsz * nb - 1
    return pl.pallas_call(
        functools.partial(_combine_kernel, tm=tm, d=d),
        grid=(bsz, nb),
        in_specs=[
            pl.BlockSpec((tm * TOP_K,), lambda b, i: (b * nb + i,), memory_space=pltpu.SMEM),
            pl.BlockSpec((tm * TOP_K,), lambda b, i: (jnp.minimum(b * nb + i + 1, last),), memory_space=pltpu.SMEM),
            pl.BlockSpec((None, tm, d), lambda b, i: (b, i, 0)),
            pl.BlockSpec((None, 1, mod.shape[-1]), lambda b, i: (b, 0, 0)),
            pl.BlockSpec((None, tm, 8), lambda b, i: (b, i, 0)),
            pl.BlockSpec(memory_space=pl.ANY),
        ],
        out_specs=pl.BlockSpec((None, tm, d), lambda b, i: (b, i, 0)),
        scratch_shapes=[
            pltpu.VMEM((2, TOP_K, tm) + ys.shape[1:], F32),
            pltpu.SemaphoreType.DMA((2,)),
        ],
        out_shape=jax.ShapeDtypeStruct((bsz, seq, d), F32),
        compiler_params=_params("arbitrary", "arbitrary"),
        name="moe_combine",
    )(dest, dest, x, mod, meta, ys)


def _mixer_out_and_moe(o, w_out, x, mod, gain, w_group, b_group, w_expert, b_expert, w_gate, w_up, w_down, layer):
    bsz, seq, d = x.shape
    groups = w_group.shape[1]
    n_exp = w_expert.shape[1]
    per_group = n_exp // groups
    n_tok = bsz * seq
    m = n_tok * TOP_K
    blk = EXPERT_BLOCK
    tm = TOKEN_TILE

    pad = LANES - groups - n_exp
    w_router = jnp.concatenate([w_group, w_expert, jnp.zeros((d, pad), F32)], axis=1)
    w_router_hi = w_router.astype(BF16)
    w_router = jnp.stack([w_router_hi, (w_router - w_router_hi.astype(F32)).astype(BF16)])
    b_router = jnp.concatenate([b_group, b_expert, jnp.zeros((pad,), F32)]).reshape(1, LANES)
    x, h, meta, cnt = _router(o, w_out, x, mod, gain, w_router, b_router, groups, per_group, tm)

    counts = cnt[0, groups:groups + n_exp].astype(I32)
    padded = (counts + blk - 1) // blk * blk
    pad_end = jnp.cumsum(padded)
    pad_start = pad_end - padded
    n_blocks = -(-m // blk) + n_exp
    block_start = jnp.arange(n_blocks, dtype=I32) * blk
    block_expert = jnp.minimum(jnp.sum((pad_end[None, :] <= block_start[:, None]).astype(I32), axis=1), n_exp - 1)
    n_used = (pad_end[-1] // blk).astype(I32).reshape(1)
    meta_flat = meta.reshape(n_tok, 8)
    e_idx = meta_flat[:, 0:TOP_K].astype(I32)
    rank = meta_flat[:, TOP_K:2 * TOP_K].astype(I32)
    start_of = jnp.sum(jnp.where(e_idx[..., None] == jnp.arange(n_exp, dtype=I32), pad_start, 0), axis=-1)
    dest = (start_of + rank).reshape(m)

    pad_info = jnp.concatenate([pad_end, padded, n_used]).astype(I32)
    xs = _dispatch(h, dest, pad_info, n_blocks * blk, DISPATCH_TILE, blk, n_exp)
    ys = _experts(xs, block_expert, n_used, w_gate, w_up, w_down, layer, blk)
    return _combine(dest, x, mod, meta, ys, tm)


def kernel(x, c, mod_w, mod_b, norm_mix_g, norm_ffn_g, gdn_w_in, gdn_conv_w, gdn_a_log, gdn_dt_bias, gdn_out_norm_g, gdn_w_out, kv_mod_w, kv_mod_b, kv_norm_g, kv_w, kv_forget_b, k_norm_g, fox_w_qz, fox_q_norm_g, fox_w_out, moe_w_group, moe_b_group, moe_w_expert, moe_b_expert, moe_w_gate, moe_w_up, moe_w_down):
    bsz, seq, d = x.shape
    depth = mod_w.shape[0]
    n_a = gdn_w_in.shape[0]
    gdn_heads = gdn_a_log.shape[1]
    gdn_inner = gdn_heads * GDN_HEAD_DIM
    fox_heads = kv_forget_b.shape[0]
    fox_inner = fox_heads * FOX_HEAD_DIM
    tm = TOKEN_TILE

    mod_all = _mod_vectors(c, mod_w, mod_b).reshape(depth, bsz, 1, 6 * d)
    kv_mod = _mod_vectors(c, kv_mod_w[None], kv_mod_b[None]).reshape(bsz, 1, 2 * d)
    bd = _block_diag_ones(2 * FOX_HEAD_DIM, FOX_HEAD_DIM)

    ka_sh = vt_sh = ft_sh = None
    for layer in range(depth):
        mod = mod_all[layer]
        gain_mix = norm_mix_g[layer].reshape(1, d)
        if layer < n_a:
            w_in = gdn_w_in[layer]
            qkv, sz, ab = _gdn_in_proj(x, mod, gain_mix, w_in[:, :4 * gdn_inner].astype(BF16),
                                       w_in[:, 4 * gdn_inner:].astype(BF16), gdn_conv_w[layer], gdn_heads, tm)
            o = _gdn_core(qkv, sz, ab, gdn_a_log[layer], gdn_dt_bias[layer],
                          gdn_out_norm_g[layer], gdn_heads, GDN_TIME_BLOCK)
            w_out = gdn_w_out[layer]
        else:
            j = layer - n_a
            wqz = fox_w_qz[j]
            qg = jnp.tile(fox_q_norm_g[j], fox_heads).reshape(1, fox_inner)
            qt, z = _fox_qz_proj(x, mod, gain_mix, wqz[:, :fox_inner].astype(BF16),
                                 wqz[:, fox_inner:].astype(BF16), qg, bd, ft_sh, fox_heads, tm)
            o = _fox_attention(qt, ka_sh, vt_sh, z, FOX_BLOCK)
            w_out = fox_w_out[j]
        x = _mixer_out_and_moe(o, w_out.astype(BF16), x, mod, norm_ffn_g[layer].reshape(1, d),
                               moe_w_group[layer], moe_b_group[layer], moe_w_expert[layer], moe_b_expert[layer],
                               moe_w_gate, moe_w_up, moe_w_down, layer)
        if layer == n_a - 1:
            kg = jnp.tile(k_norm_g, fox_heads).reshape(1, fox_inner)
            ka_sh, vt_sh, ft_sh = _shared_kv(
                x, kv_mod, kv_norm_g.reshape(1, d), kv_w[:, :fox_inner].astype(BF16),
                kv_w[:, fox_inner:2 * fox_inner].astype(BF16), kv_w[:, 2 * fox_inner:].astype(BF16),
                kv_forget_b.reshape(1, fox_heads), kg, bd, fox_heads, tm)
    return x
```

```python
import functools

import jax
import jax.numpy as jnp
import numpy as np
from jax import lax
from jax.experimental import pallas as pl
from jax.experimental.pallas import tpu as pltpu

F32 = jnp.float32
BF16 = jnp.bfloat16
I32 = jnp.int32

EPS = 1e-6
GDN_CHUNK = 64
GDN_HEAD_DIM = 128
FOX_HEAD_DIM = 64
TOP_K = 2
LANES = 128
SUBLANES = 8
VMEM_LIMIT = 56 * 1024 * 1024
HIGHEST = lax.Precision.HIGHEST

TOKEN_TILE = 512
EXPERT_BLOCK = 512
DISPATCH_TILE = 2 * TOKEN_TILE
FOX_BLOCK = 512
GDN_TIME_BLOCK = 4 * GDN_CHUNK

NT_DIMS = (((1,), (1,)), ((), ()))
TN_DIMS = (((0,), (0,)), ((), ()))


def _params(*sem):
    return pltpu.CompilerParams(dimension_semantics=sem, vmem_limit_bytes=VMEM_LIMIT)


def _sigmoid(x):
    return 1.0 / (1.0 + jnp.exp(-x))


def _silu(x):
    return x * _sigmoid(x)


def _softplus(x):
    return jnp.maximum(x, 0.0) + jnp.log(1.0 + jnp.exp(-jnp.abs(x)))


def _modulated(x, gain, scale, shift):
    ms = jnp.mean(x * x, axis=-1, keepdims=True)
    y = x * lax.rsqrt(ms + EPS)
    return (y * gain) * (1.0 + scale) + shift


def _bdot(a, b, dims=None):
    a = a.astype(BF16)
    b = b.astype(BF16)
    if dims is None:
        return jnp.dot(a, b, preferred_element_type=F32)
    return lax.dot_general(a, b, dims, preferred_element_type=F32)


def _hdot(a, b, dims=None):
    if dims is None:
        return jnp.dot(a, b, preferred_element_type=F32, precision=HIGHEST)
    return lax.dot_general(a, b, dims, preferred_element_type=F32, precision=HIGHEST)


def _iota2(shape, dim):
    return lax.broadcasted_iota(I32, shape, dim)


def _mod_kernel(c_ref, w_ref, b_ref, o_ref):
    c = c_ref[...]
    o_ref[...] = _hdot(_silu(c), w_ref[...]) + b_ref[...]


def _mod_vectors(c, w, b):
    n_layers, d, n = w.shape
    bsz = c.shape[0]
    tn = 1536 if n % 1536 == 0 else n
    return pl.pallas_call(
        _mod_kernel,
        grid=(n_layers, n // tn),
        in_specs=[
            pl.BlockSpec((bsz, d), lambda l, j: (0, 0)),
            pl.BlockSpec((None, d, tn), lambda l, j: (l, 0, j)),
            pl.BlockSpec((None, 1, tn), lambda l, j: (l, 0, j)),
        ],
        out_specs=pl.BlockSpec((None, bsz, tn), lambda l, j: (l, 0, j)),
        out_shape=jax.ShapeDtypeStruct((n_layers, bsz, n), F32),
        compiler_params=_params("parallel", "parallel"),
        name="mod_vectors",
    )(c, w, b.reshape(n_layers, 1, n))


def _gdn_in_kernel(x_ref, mod_ref, g_ref, w_ref, wab_ref, cw_ref, qkv_ref, sz_ref, oab_ref, xpad_ref,
                   *, d, tm, heads, dh, width):
    inner = heads * dh
    halo = 8
    i = pl.program_id(1)

    @pl.when(i == 0)
    def _():
        xpad_ref[0:halo, :] = jnp.zeros((halo, 3 * inner), F32)

    h = _modulated(x_ref[...], g_ref[...], mod_ref[:, d:2 * d], mod_ref[:, 0:d]).astype(BF16)
    oab_ref[...] = jnp.dot(h, wab_ref[...], preferred_element_type=F32)
    z = jnp.dot(h, w_ref[:, 3 * inner:4 * inner], preferred_element_type=F32)
    sz_ref[...] = _silu(z).astype(sz_ref.dtype)
    xpad_ref[halo:halo + tm, :] = jnp.dot(h, w_ref[:, 0:3 * inner], preferred_element_type=F32)
    for j in range(3 * heads):
        cols = slice(j * dh, (j + 1) * dh)
        acc = xpad_ref[halo:halo + tm, cols] * cw_ref[width - 1:width, cols]
        for s in range(1, width):
            acc = acc + xpad_ref[halo - s:halo - s + tm, cols] * cw_ref[width - 1 - s:width - s, cols]
        y = _silu(acc)
        if j < 2 * heads:
            y = y * lax.rsqrt(jnp.sum(y * y, axis=-1, keepdims=True) + EPS)
        if j < heads:
            y = y * (dh ** -0.5)
        qkv_ref[:, cols] = y.astype(qkv_ref.dtype)
    xpad_ref[0:halo, :] = xpad_ref[tm:tm + halo, :]


def _gdn_in_proj(x, mod, gain, w_main, w_ab, conv_w, heads, tm):
    bsz, seq, d = x.shape
    dh = GDN_HEAD_DIM
    inner = heads * dh
    nab = w_ab.shape[1]
    width = conv_w.shape[0]
    return pl.pallas_call(
        functools.partial(_gdn_in_kernel, d=d, tm=tm, heads=heads, dh=dh, width=width),
        grid=(bsz, seq // tm),
        in_specs=[
            pl.BlockSpec((None, tm, d), lambda b, i: (b, i, 0)),
            pl.BlockSpec((None, 1, mod.shape[-1]), lambda b, i: (b, 0, 0)),
            pl.BlockSpec((1, d), lambda b, i: (0, 0)),
            pl.BlockSpec((d, 4 * inner), lambda b, i: (0, 0)),
            pl.BlockSpec((d, nab), lambda b, i: (0, 0)),
            pl.BlockSpec((width, 3 * inner), lambda b, i: (0, 0)),
        ],
        out_specs=[
            pl.BlockSpec((None, tm, 3 * inner), lambda b, i: (b, i, 0)),
            pl.BlockSpec((None, tm, inner), lambda b, i: (b, i, 0)),
            pl.BlockSpec((None, tm, nab), lambda b, i: (b, i, 0)),
        ],
        out_shape=[
            jax.ShapeDtypeStruct((bsz, seq, 3 * inner), BF16),
            jax.ShapeDtypeStruct((bsz, seq, inner), BF16),
            jax.ShapeDtypeStruct((bsz, seq, nab), F32),
        ],
        scratch_shapes=[pltpu.VMEM((8 + tm, 3 * inner), F32)],
        compiler_params=_params("parallel", "arbitrary"),
        name="gdn_in_proj",
    )(x, mod, gain, w_main, w_ab, conv_w)


def _gdn_kernel(q_ref, k_ref, v_ref, sz_ref, ab_ref, alog_ref, dtb_ref, og_ref, o_ref, state_ref,
                *, tb, heads, dh):
    c = GDN_CHUNK
    pairs = heads // 2
    i = pl.program_id(1)

    @pl.when(i == 0)
    def _():
        state_ref[...] = jnp.zeros(state_ref.shape, F32)

    ab = ab_ref[...]
    beta_all = _sigmoid(ab[:, heads:2 * heads])
    g_all = -jnp.exp(alog_ref[...]) * _softplus(ab[:, 0:heads] + dtb_ref[...])
    og = og_ref[...]

    row = _iota2((c, 2 * c), 0)
    lane = _iota2((c, 2 * c), 1)
    first = lane < c
    col = jnp.where(first, lane, lane - c)
    incl = row >= col
    strict = row > col
    eye_p = jnp.where(row == col, 1.0, 0.0).astype(F32)
    tri = jnp.where(_iota2((c, c), 0) >= _iota2((c, c), 1), 1.0, 0.0).astype(F32)
    bd_small = (_iota2((2 * c, 2 * c), 0) < c) == (_iota2((2 * c, 2 * c), 1) < c)
    bd_wide = (_iota2((2 * c, 2 * dh), 0) < c) == (_iota2((2 * c, 2 * dh), 1) < dh)
    first_h = _iota2((pairs, 2 * c), 1) < c
    sel0 = jnp.where(_iota2((pairs, heads), 1) == 2 * _iota2((pairs, heads), 0), 1.0, 0.0).astype(F32)
    sel1 = jnp.where(_iota2((pairs, heads), 1) == 2 * _iota2((pairs, heads), 0) + 1, 1.0, 0.0).astype(F32)

    def block_diag(x, mask):
        return jnp.where(mask, jnp.concatenate([x, x], axis=0), 0.0).astype(BF16)

    def pair_cols(x, p):
        return jnp.concatenate([jnp.broadcast_to(x[:, 2 * p:2 * p + 1], (c, dh)),
                                jnp.broadcast_to(x[:, 2 * p + 1:2 * p + 2], (c, dh))], axis=1)

    units = [(ci, p) for ci in range(tb // c) for p in range(pairs)]
    gcums, glasts, pk, tt, attn, w_u, qg, kdec = {}, {}, {}, {}, {}, {}, {}, {}

    for ci in range(tb // c):
        r0 = ci * c
        gcum = _hdot(tri, g_all[r0:r0 + c, :])
        gc2 = jnp.concatenate([gcum, gcum], axis=0)
        gt = jnp.where(first_h, _hdot(sel0, gc2, NT_DIMS), _hdot(sel1, gc2, NT_DIMS))
        glast = gcum[c - 1:c, :]
        gcums[ci] = gcum
        glasts[ci] = glast
        for p in range(pairs):
            u = (ci, p)
            cols = slice(2 * p * dh, (2 * p + 2) * dh)
            kp = k_ref[r0:r0 + c, cols].astype(F32)
            qp = q_ref[r0:r0 + c, cols].astype(F32)
            vp = v_ref[r0:r0 + c, cols].astype(F32)
            gcol = jnp.where(first, gcum[:, 2 * p:2 * p + 1], gcum[:, 2 * p + 1:2 * p + 2])
            decay = jnp.exp(jnp.where(incl, gcol - gt[p:p + 1, :], -jnp.inf))
            beta2 = pair_cols(beta_all[r0:r0 + c, :], p)
            gcum2 = pair_cols(gcum, p)
            egc2 = jnp.exp(gcum2)
            kb = kp * beta2
            y = block_diag(kp, bd_wide)
            kq = _bdot(jnp.concatenate([kb, qp], axis=0), y, NT_DIMS)
            a_mat = jnp.where(strict, kq[0:c, :] * decay, 0.0)
            attn[u] = kq[c:2 * c, :] * decay
            pk[u] = -a_mat
            tt[u] = eye_p - a_mat
            kbg = kb * egc2
            vb = vp * beta2
            w_u[u] = jnp.concatenate(
                [jnp.concatenate([kbg[:, 0:dh], vb[:, 0:dh]], axis=1),
                 jnp.concatenate([kbg[:, dh:2 * dh], vb[:, dh:2 * dh]], axis=1)], axis=0).astype(BF16)
            qg[u] = qp * egc2
            kdec[u] = kp * jnp.exp(pair_cols(jnp.broadcast_to(glast, (c, heads)), p) - gcum2)

    span = 2
    while span <= c:
        last = span == c
        for u in units:
            bd = block_diag(pk[u], bd_small)
            if span == 2:
                pk[u] = _bdot(pk[u], bd)
            elif last:
                tt[u] = tt[u] + _bdot(tt[u], bd)
            else:
                both = _bdot(jnp.concatenate([pk[u], tt[u]], axis=0), bd)
                pk[u] = both[0:c, :]
                tt[u] = tt[u] + both[c:2 * c, :]
        span *= 2
    for u in units:
        t = tt[u]
        lhs = jnp.concatenate([jnp.where(first, t, 0.0), jnp.where(first, 0.0, t)], axis=0)
        w_u[u] = _bdot(lhs, w_u[u])

    for ci in range(tb // c):
        r0 = ci * c
        for p in range(pairs):
            u = (ci, p)
            wu = w_u[u]
            st, wq, vn = [], [], []
            for s in range(2):
                h = 2 * p + s
                st.append(state_ref[h])
                wq.append(_bdot(jnp.concatenate([wu[s * c:(s + 1) * c, 0:dh], qg[u][:, s * dh:(s + 1) * dh]], axis=0),
                                st[s]))
                vn.append(wu[s * c:(s + 1) * c, dh:2 * dh] - wq[s][0:c, :])
            vn2 = jnp.concatenate(vn, axis=0).astype(BF16)
            for s in range(2):
                h = 2 * p + s
                am = jnp.where(first, attn[u], 0.0) if s == 0 else jnp.where(first, 0.0, attn[u])
                o = wq[s][c:2 * c, :] + _bdot(am, vn2)
                gl = glasts[ci][:, h:h + 1]
                state_ref[h] = st[s] * jnp.exp(gl) + _bdot(kdec[u][:, s * dh:(s + 1) * dh], vn[s], TN_DIMS)
                on = o * lax.rsqrt(jnp.mean(o * o, axis=-1, keepdims=True) + EPS) * og
                szh = sz_ref[r0:r0 + c, h * dh:(h + 1) * dh].astype(F32)
                o_ref[r0:r0 + c, h * dh:(h + 1) * dh] = (on * szh).astype(o_ref.dtype)


def _gdn_core(qkv, sz, ab, a_log, dt_bias, out_g, heads, tb):
    bsz, seq, _ = qkv.shape
    dh = GDN_HEAD_DIM
    inner = heads * dh
    return pl.pallas_call(
        functools.partial(_gdn_kernel, tb=tb, heads=heads, dh=dh),
        grid=(bsz, seq // tb),
        in_specs=[
            pl.BlockSpec((None, tb, inner), lambda b, i: (b, i, 0)),
            pl.BlockSpec((None, tb, inner), lambda b, i: (b, i, 1)),
            pl.BlockSpec((None, tb, inner), lambda b, i: (b, i, 2)),
            pl.BlockSpec((None, tb, inner), lambda b, i: (b, i, 0)),
            pl.BlockSpec((None, tb, 2 * heads), lambda b, i: (b, i, 0)),
            pl.BlockSpec((1, heads), lambda b, i: (0, 0)),
            pl.BlockSpec((1, heads), lambda b, i: (0, 0)),
            pl.BlockSpec((1, dh), lambda b, i: (0, 0)),
        ],
        out_specs=pl.BlockSpec((None, tb, inner), lambda b, i: (b, i, 0)),
        out_shape=jax.ShapeDtypeStruct((bsz, seq, inner), BF16),
        scratch_shapes=[pltpu.VMEM((heads, dh, dh), F32)],
        compiler_params=_params("parallel", "arbitrary"),
        name="gdn_core",
    )(qkv, qkv, qkv, sz, ab, a_log.reshape(1, heads), dt_bias.reshape(1, heads), out_g.reshape(1, dh))


def _head_rms(x, bd, gain, dh):
    w = bd.shape[0]
    sq = (x * x).astype(BF16)
    ss = jnp.concatenate([jnp.dot(sq[:, c:c + w], bd, preferred_element_type=F32)
                          for c in range(0, x.shape[1], w)], axis=1)
    return x * lax.rsqrt(ss * (1.0 / dh) + EPS) * gain


def _block_diag_ones(n, blk):
    r = jnp.arange(n, dtype=I32) // blk
    return (r[:, None] == r[None, :]).astype(BF16)


N_SPLIT = 3


def _bias_base(head, dh):
    return dh if head % 2 == 0 else 0


def _split3(x):
    hi = x.astype(BF16)
    r = x - hi.astype(F32)
    mid = r.astype(BF16)
    lo = (r - mid.astype(F32)).astype(BF16)
    return hi, mid, lo


def _key_bias_placement(heads, dh):
    place = np.zeros((N_SPLIT, heads, heads * 2 * dh), np.float32)
    for j in range(N_SPLIT):
        for h in range(heads):
            place[j, h, h * 2 * dh + _bias_base(h, dh) + j] = 1.0
    return jnp.asarray(place, BF16)


def _kv_kernel(x_ref, mod_ref, g_ref, wk_ref, wv_ref, wf_ref, fb_ref, kg_ref, bd_ref, place_ref,
               ka_ref, vt_ref, ft_ref, carry_ref, *, d, tm, heads, dh):
    i = pl.program_id(1)

    @pl.when(i == 0)
    def _():
        carry_ref[...] = jnp.zeros(carry_ref.shape, F32)

    h = _modulated(x_ref[...], g_ref[...], mod_ref[:, d:2 * d], mod_ref[:, 0:d]).astype(BF16)
    kraw = jnp.dot(h, wk_ref[...], preferred_element_type=F32)
    kn = _head_rms(kraw, bd_ref[...], kg_ref[...], dh)
    v = jnp.dot(h, wv_ref[...], preferred_element_type=F32).astype(BF16)
    f = jnp.dot(h, wf_ref[...], preferred_element_type=F32) + fb_ref[...]
    log_f = -_softplus(-f)
    tri = jnp.where(_iota2((tm, tm), 0) >= _iota2((tm, tm), 1), 1.0, 0.0).astype(F32)
    fcum = _hdot(tri, log_f) + carry_ref[...]
    carry_ref[...] = fcum[tm - 1:tm, :]
    eye_h = jnp.where(_iota2((heads, heads), 0) == _iota2((heads, heads), 1), 1.0, 0.0).astype(F32)
    fcum_t = _hdot(eye_h, fcum, NT_DIMS)
    for p in range(heads // 2):
        ft_ref[p] = fcum_t[2 * p:2 * p + 2, :]

    pieces = _split3(-fcum)
    bias = jnp.dot(pieces[0], place_ref[0], preferred_element_type=F32)
    for j in range(1, N_SPLIT):
        bias = bias + jnp.dot(pieces[j], place_ref[j], preferred_element_type=F32)
    lane = _iota2((tm, 2 * dh), 1)
    for hh in range(heads):
        p = hh // 2
        base = _bias_base(hh, dh)
        real = (lane < dh) if hh % 2 == 0 else (lane >= dh)
        ones = (lane >= base + N_SPLIT) & (lane < base + 2 * N_SPLIT)
        blk = jnp.where(real, kn[:, 2 * p * dh:(2 * p + 2) * dh],
                        jnp.where(ones, 1.0, bias[:, hh * 2 * dh:(hh + 1) * 2 * dh]))
        ka_ref[hh] = blk.astype(ka_ref.dtype)

    eye = jnp.where(_iota2((2 * dh, 2 * dh), 0) == _iota2((2 * dh, 2 * dh), 1), 1.0, 0.0).astype(BF16)
    rowi = _iota2((2 * dh, tm), 0)
    for p in range(heads // 2):
        vt = lax.dot_general(eye, v[:, 2 * p * dh:(2 * p + 2) * dh], NT_DIMS, preferred_element_type=F32)
        vt_ref[2 * p] = jnp.where(rowi < dh, vt, 1.0).astype(vt_ref.dtype)
        vt_ref[2 * p + 1] = jnp.where(rowi >= dh, vt, 1.0).astype(vt_ref.dtype)


def _shared_kv(x, mod, gain, wk, wv, wf, fb, kgain, bd, heads, tm):
    bsz, seq, d = x.shape
    dh = FOX_HEAD_DIM
    inner = heads * dh
    pairs = heads // 2
    place = _key_bias_placement(heads, dh)
    return pl.pallas_call(
        functools.partial(_kv_kernel, d=d, tm=tm, heads=heads, dh=dh),
        grid=(bsz, seq // tm),
        in_specs=[
            pl.BlockSpec((None, tm, d), lambda b, i: (b, i, 0)),
            pl.BlockSpec((None, 1, mod.shape[-1]), lambda b, i: (b, 0, 0)),
            pl.BlockSpec((1, d), lambda b, i: (0, 0)),
            pl.BlockSpec((d, inner), lambda b, i: (0, 0)),
            pl.BlockSpec((d, inner), lambda b, i: (0, 0)),
            pl.BlockSpec((d, heads), lambda b, i: (0, 0)),
            pl.BlockSpec((1, heads), lambda b, i: (0, 0)),
            pl.BlockSpec((1, inner), lambda b, i: (0, 0)),
            pl.BlockSpec(bd.shape, lambda b, i: (0, 0)),
            pl.BlockSpec(place.shape, lambda b, i: (0, 0, 0)),
        ],
        out_specs=[
            pl.BlockSpec((None, heads, tm, 2 * dh), lambda b, i: (b, 0, i, 0)),
            pl.BlockSpec((None, heads, 2 * dh, tm), lambda b, i: (b, 0, 0, i)),
            pl.BlockSpec((None, pairs, 2, tm), lambda b, i: (b, 0, 0, i)),
        ],
        out_shape=[
            jax.ShapeDtypeStruct((bsz, heads, seq, 2 * dh), BF16),
            jax.ShapeDtypeStruct((bsz, heads, 2 * dh, seq), BF16),
            jax.ShapeDtypeStruct((bsz, pairs, 2, seq), F32),
        ],
        scratch_shapes=[pltpu.VMEM((1, heads), F32)],
        compiler_params=_params("parallel", "arbitrary"),
        name="shared_kv",
    )(x, mod, gain, wk, wv, wf, fb, kgain, bd, place)


def _fox_qz_kernel(x_ref, mod_ref, g_ref, wq_ref, wz_ref, qg_ref, bd_ref, ft_ref, qt_ref, z_ref,
                   *, d, dh, heads, tm):
    h = _modulated(x_ref[...], g_ref[...], mod_ref[:, d:2 * d], mod_ref[:, 0:d]).astype(BF16)
    qraw = jnp.dot(h, wq_ref[...], preferred_element_type=F32)
    qn = (_head_rms(qraw, bd_ref[...], qg_ref[...], dh) * (dh ** -0.5)).astype(BF16)
    z_ref[...] = jnp.dot(h, wz_ref[...], preferred_element_type=F32).astype(z_ref.dtype)
    eye = jnp.where(_iota2((2 * dh, 2 * dh), 0) == _iota2((2 * dh, 2 * dh), 1), 1.0, 0.0).astype(BF16)
    rowi = _iota2((2 * dh, tm), 0)
    for p in range(heads // 2):
        qt = lax.dot_general(eye, qn[:, 2 * p * dh:(2 * p + 2) * dh], NT_DIMS, preferred_element_type=F32)
        for s in range(2):
            hh = 2 * p + s
            base = _bias_base(hh, dh)
            real = (rowi < dh) if s == 0 else (rowi >= dh)
            aug = jnp.where(real, qt, 0.0)
            aug = jnp.where((rowi >= base) & (rowi < base + N_SPLIT), 1.0, aug)
            pieces = _split3(ft_ref[p, s:s + 1, :])
            for j in range(N_SPLIT):
                aug = jnp.where(rowi == base + N_SPLIT + j, pieces[j].astype(F32), aug)
            qt_ref[hh] = aug.astype(qt_ref.dtype)


def _fox_qz_proj(x, mod, gain, wq, wz, qgain, bd, ft, heads, tm):
    bsz, seq, d = x.shape
    dh = FOX_HEAD_DIM
    inner = wq.shape[1]
    pairs = heads // 2
    return pl.pallas_call(
        functools.partial(_fox_qz_kernel, d=d, dh=dh, heads=heads, tm=tm),
        grid=(bsz, seq // tm),
        in_specs=[
            pl.BlockSpec((None, tm, d), lambda b, i: (b, i, 0)),
            pl.BlockSpec((None, 1, mod.shape[-1]), lambda b, i: (b, 0, 0)),
            pl.BlockSpec((1, d), lambda b, i: (0, 0)),
            pl.BlockSpec((d, inner), lambda b, i: (0, 0)),
            pl.BlockSpec((d, inner), lambda b, i: (0, 0)),
            pl.BlockSpec((1, inner), lambda b, i: (0, 0)),
            pl.BlockSpec(bd.shape, lambda b, i: (0, 0)),
            pl.BlockSpec((None, pairs, 2, tm), lambda b, i: (b, 0, 0, i)),
        ],
        out_specs=[
            pl.BlockSpec((None, heads, 2 * dh, tm), lambda b, i: (b, 0, 0, i)),
            pl.BlockSpec((None, tm, inner), lambda b, i: (b, i, 0)),
        ],
        out_shape=[
            jax.ShapeDtypeStruct((bsz, heads, 2 * dh, seq), BF16),
            jax.ShapeDtypeStruct((bsz, seq, inner), BF16),
        ],
        compiler_params=_params("parallel", "parallel"),
        name="fox_qz_proj",
    )(x, mod, gain, wq, wz, qgain, bd, ft)


def _fox_kernel(qt_ref, ka_ref, vt_ref, z_ref, o_ref, s_ref, p_ref, acc_ref, *, tq, dh):
    i = pl.program_id(2)
    acc_ref[...] = jnp.zeros(acc_ref.shape, F32)
    key_idx = _iota2((tq, LANES), 0)
    qry_idx = _iota2((tq, LANES), 1)

    wide = 2 * LANES
    units = [(h, c) for h in range(2) for c in range(tq // wide)]

    def scores(kb, slot, u):
        h, c = u
        start = pl.multiple_of(kb * tq, tq)
        s = jnp.dot(ka_ref[h, pl.ds(start, tq), :], qt_ref[h, :, c * wide:(c + 1) * wide],
                    preferred_element_type=F32)
        for t in range(wide // LANES):
            s_ref[slot, h, c * (wide // LANES) + t] = s[:, t * LANES:(t + 1) * LANES]

    def softmax_pv(kb, slot, u, ms, masked, m_out):
        h, c = u
        start = pl.multiple_of(kb * tq, tq)
        alphas = []
        for t in range(wide // LANES):
            q0 = c * wide + t * LANES
            cols = slice(q0, q0 + LANES)
            s = s_ref[slot, h, q0 // LANES]
            if masked:
                s = jnp.where(key_idx <= qry_idx + q0, s, -jnp.inf)
            m_old = ms[h][:, cols]
            parts = [s[r:r + 64, :] for r in range(0, tq, 64)]
            while len(parts) > 1:
                parts = [jnp.maximum(parts[n], parts[n + 1]) for n in range(0, len(parts), 2)]
            m_new = jnp.maximum(m_old, jnp.max(parts[0], axis=0, keepdims=True))
            p_ref[h, q0 // LANES] = jnp.exp(s - m_new).astype(BF16)
            alphas.append(jnp.exp(m_old - m_new))
            m_out[(h, q0)] = m_new
        cols = slice(c * wide, (c + 1) * wide)
        alpha = jnp.concatenate(alphas, axis=1)
        p = jnp.concatenate([p_ref[h, c * (wide // LANES) + t] for t in range(wide // LANES)], axis=1)
        acc_ref[h, :, cols] = alpha * acc_ref[h, :, cols] + jnp.dot(
            vt_ref[h, :, pl.ds(start, tq)], p, preferred_element_type=F32)

    def step(kb, slot, ms, masked, prefetch):
        m_out = {}
        if prefetch:
            for u in units:
                scores(kb + 1, 1 - slot, u)
        for u in units:
            softmax_pv(kb, slot, u, ms, masked, m_out)
        return tuple(jnp.concatenate([m_out[(h, q0)] for q0 in range(0, tq, LANES)], axis=1) for h in range(2))

    def by_parity(kb, ms, masked, prefetch):
        return lax.cond(kb % 2 == 0,
                        lambda ms: step(kb, 0, ms, masked, prefetch),
                        lambda ms: step(kb, 1, ms, masked, prefetch), ms)

    for u in units:
        scores(0, 0, u)
    neg = jnp.full((1, tq), -jnp.inf, F32)
    ms = lax.fori_loop(0, i, lambda kb, ms: by_parity(kb, ms, False, True), (neg, neg))
    by_parity(i, ms, True, False)
    a0 = acc_ref[0]
    a1 = acc_ref[1]
    num = jnp.concatenate([a0[0:dh, :], a1[dh:2 * dh, :]], axis=0)
    den = jnp.concatenate([a0[dh:2 * dh, :], a1[0:dh, :]], axis=0)
    o = jnp.transpose(num / den)
    o_ref[...] = (o * _sigmoid(z_ref[...].astype(F32))).astype(o_ref.dtype)


def _fox_attention(qt, ka, vt, z, tq):
    bsz, heads, feat, seq = qt.shape
    dh = FOX_HEAD_DIM
    pairs = heads // 2
    return pl.pallas_call(
        functools.partial(_fox_kernel, tq=tq, dh=dh),
        grid=(bsz, pairs, seq // tq),
        in_specs=[
            pl.BlockSpec((None, 2, feat, tq), lambda b, p, i: (b, p, 0, i)),
            pl.BlockSpec((None, 2, seq, feat), lambda b, p, i: (b, p, 0, 0)),
            pl.BlockSpec((None, 2, feat, seq), lambda b, p, i: (b, p, 0, 0)),
            pl.BlockSpec((None, tq, 2 * dh), lambda b, p, i: (b, i, p)),
        ],
        out_specs=pl.BlockSpec((None, tq, 2 * dh), lambda b, p, i: (b, i, p)),
        out_shape=jax.ShapeDtypeStruct((bsz, seq, heads * dh), BF16),
        scratch_shapes=[
            pltpu.VMEM((2, 2, tq // LANES, tq, LANES), F32),
            pltpu.VMEM((2, tq // LANES, tq, LANES), BF16),
            pltpu.VMEM((2, feat, tq), F32),
        ],
        compiler_params=_params("parallel", "parallel", "parallel"),
        name="fox_attention",
    )(qt, ka, vt, z)


def _router_kernel(o_ref, wo_ref, x_ref, mod_ref, g_ref, wr_ref, br_ref, xo_ref, h_ref, meta_ref, cnt_ref,
                   carry_ref, *, d, tm, groups, per_group):
    step = pl.program_id(0) * pl.num_programs(1) + pl.program_id(1)

    @pl.when(step == 0)
    def _():
        carry_ref[...] = jnp.zeros(carry_ref.shape, F32)

    x = x_ref[...] + mod_ref[:, 2 * d:3 * d] * jnp.dot(o_ref[...], wo_ref[...], preferred_element_type=F32)
    xo_ref[...] = x
    h = _modulated(x, g_ref[...], mod_ref[:, 4 * d:5 * d], mod_ref[:, 3 * d:4 * d])
    h_ref[...] = h.reshape(h_ref.shape)
    h_hi = h.astype(BF16)
    h_lo = (h - h_hi.astype(F32)).astype(BF16)
    logits = (jnp.dot(h_hi, wr_ref[0], preferred_element_type=F32)
              + (jnp.dot(h_lo, wr_ref[0], preferred_element_type=F32)
                 + jnp.dot(h_hi, wr_ref[1], preferred_element_type=F32))
              + br_ref[...])
    lane = _iota2((tm, LANES), 1)
    neg_inf = jnp.float32(-jnp.inf)
    big = jnp.int32(LANES)

    def first_argmax(vals, mask):
        mv = jnp.where(mask, vals, neg_inf)
        mx = jnp.max(mv, axis=-1, keepdims=True)
        idx = jnp.min(jnp.where(mask & (mv == mx), lane, big), axis=-1, keepdims=True)
        return mx, idx

    gmask = lane < groups
    gmax, gidx = first_argmax(logits, gmask)
    gsum = jnp.sum(jnp.where(gmask, jnp.exp(logits - gmax), 0.0), axis=-1, keepdims=True)
    group_gate = 1.0 / gsum
    lo = groups + gidx * per_group
    emask = (lane >= lo) & (lane < lo + per_group)
    e1max, e1lane = first_argmax(logits, emask)
    e2max, e2lane = first_argmax(logits, emask & (lane != e1lane))
    esum = jnp.sum(jnp.where(emask, jnp.exp(logits - e1max), 0.0), axis=-1, keepdims=True)
    p1 = 1.0 / esum
    p2 = jnp.exp(e2max - e1max) / esum
    psum = p1 + p2
    w1 = group_gate * (p1 / psum)
    w2 = group_gate * (p2 / psum)

    oh1 = (lane == e1lane).astype(F32)
    oh2 = (lane == e2lane).astype(F32)
    strict = jnp.where(_iota2((tm, tm), 0) > _iota2((tm, tm), 1), 1.0, 0.0).astype(BF16)
    c1 = jnp.dot(strict, oh1.astype(BF16), preferred_element_type=F32)
    c2 = jnp.dot(strict, oh2.astype(BF16), preferred_element_type=F32)
    tot1 = jnp.sum(oh1, axis=0, keepdims=True)
    tot2 = jnp.sum(oh2, axis=0, keepdims=True)
    carry = carry_ref[...]
    rank1 = jnp.sum(oh1 * (c1 + carry), axis=-1, keepdims=True)
    rank2 = jnp.sum(oh2 * (c2 + carry + tot1), axis=-1, keepdims=True)
    carry = carry + tot1 + tot2
    carry_ref[...] = carry
    cnt_ref[...] = carry

    e1 = (e1lane - groups).astype(F32)
    e2 = (e2lane - groups).astype(F32)
    mlane = _iota2((tm, 8), 1)
    meta = jnp.where(mlane == 0, e1, 0.0)
    meta = jnp.where(mlane == 1, e2, meta)
    meta = jnp.where(mlane == 2, rank1, meta)
    meta = jnp.where(mlane == 3, rank2, meta)
    meta = jnp.where(mlane == 4, w1, meta)
    meta = jnp.where(mlane == 5, w2, meta)
    meta_ref[...] = meta


def _router(o, w_out, x, mod, gain, w_router, b_router, groups, per_group, tm):
    bsz, seq, d = x.shape
    k = o.shape[-1]
    nb = seq // tm
    return pl.pallas_call(
        functools.partial(_router_kernel, d=d, tm=tm, groups=groups, per_group=per_group),
        grid=(bsz, seq // tm),
        in_specs=[
            pl.BlockSpec((None, tm, k), lambda b, i: (b, i, 0)),
            pl.BlockSpec((k, d), lambda b, i: (0, 0)),
            pl.BlockSpec((None, tm, d), lambda b, i: (b, i, 0)),
            pl.BlockSpec((None, 1, mod.shape[-1]), lambda b, i: (b, 0, 0)),
            pl.BlockSpec((1, d), lambda b, i: (0, 0)),
            pl.BlockSpec((2, d, LANES), lambda b, i: (0, 0, 0)),
            pl.BlockSpec((1, LANES), lambda b, i: (0, 0)),
        ],
        out_specs=[
            pl.BlockSpec((None, tm, d), lambda b, i: (b, i, 0)),
            pl.BlockSpec((tm, SUBLANES, d // SUBLANES), lambda b, i: (b * nb + i, 0, 0)),
            pl.BlockSpec((None, tm, 8), lambda b, i: (b, i, 0)),
            pl.BlockSpec((1, LANES), lambda b, i: (0, 0)),
        ],
        out_shape=[
            jax.ShapeDtypeStruct((bsz, seq, d), F32),
            jax.ShapeDtypeStruct((bsz * seq, SUBLANES, d // SUBLANES), F32),
            jax.ShapeDtypeStruct((bsz, seq, 8), F32),
            jax.ShapeDtypeStruct((1, LANES), F32),
        ],
        scratch_shapes=[pltpu.VMEM((1, LANES), F32)],
        compiler_params=_params("arbitrary", "arbitrary"),
        name="moe_router",
    )(o, w_out, x, mod, gain, w_router, b_router)


DMA_GROUP = 16


def _dispatch_kernel(pad_ref, dest_ref, h_ref, xs_ref, zero_ref, sem, zsem, *, tm, blk, n_exp):
    @pl.when(pl.program_id(0) == 0)
    def _():
        zero_ref[...] = jnp.zeros(zero_ref.shape, zero_ref.dtype)

        def zero_copy(e):
            return pltpu.make_async_copy(zero_ref, xs_ref.at[pl.ds(pad_ref[e] - blk, blk)], zsem)

        def start(e, carry):
            @pl.when(pad_ref[n_exp + e] > 0)
            def _():
                zero_copy(e).start()
            return carry

        def wait(e, carry):
            @pl.when(pad_ref[n_exp + e] > 0)
            def _():
                zero_copy(e).wait()
            return carry

        lax.fori_loop(0, n_exp, start, 0)
        lax.fori_loop(0, n_exp, wait, 0)

        def tail_copy(j):
            return pltpu.make_async_copy(zero_ref, xs_ref.at[pl.ds(j * blk, blk)], zsem)

        def tail_start(j, carry):
            tail_copy(j).start()
            return carry

        def tail_wait(j, carry):
            tail_copy(j).wait()
            return carry

        n_used = pad_ref[2 * n_exp]
        lax.fori_loop(n_used, xs_ref.shape[0] // blk, tail_start, 0)
        lax.fori_loop(n_used, xs_ref.shape[0] // blk, tail_wait, 0)

    def row_copy(r, k):
        return pltpu.make_async_copy(h_ref.at[r], xs_ref.at[dest_ref[r * TOP_K + k]], sem)

    def start(g, carry):
        for s in range(DMA_GROUP):
            for k in range(TOP_K):
                row_copy(g * DMA_GROUP + s, k).start(priority=k % 2)
        return carry

    def wait(g, carry):
        for s in range(DMA_GROUP):
            for k in range(TOP_K):
                row_copy(g * DMA_GROUP + s, k).wait()
        return carry

    lax.fori_loop(0, tm // DMA_GROUP, start, 0)
    lax.fori_loop(0, tm // DMA_GROUP, wait, 0)


def _dispatch(h_tiles, dest, pad_info, rows, tm, blk, n_exp):
    n_tok = h_tiles.shape[0]
    tile = h_tiles.shape[1:]
    return pl.pallas_call(
        functools.partial(_dispatch_kernel, tm=tm, blk=blk, n_exp=n_exp),
        grid_spec=pltpu.PrefetchScalarGridSpec(
            num_scalar_prefetch=1,
            grid=(n_tok // tm,),
            in_specs=[
                pl.BlockSpec((tm * TOP_K,), lambda i, pad: (i,), memory_space=pltpu.SMEM),
                pl.BlockSpec((tm,) + tile, lambda i, pad: (i, 0, 0)),
            ],
            out_specs=pl.BlockSpec(memory_space=pl.ANY),
            scratch_shapes=[
                pltpu.VMEM((blk,) + tile, F32),
                pltpu.SemaphoreType.DMA(()),
                pltpu.SemaphoreType.DMA(()),
            ],
        ),
        out_shape=jax.ShapeDtypeStruct((rows,) + tile, F32),
        compiler_params=_params("arbitrary"),
        name="moe_dispatch",
    )(pad_info, dest, h_tiles)


def _expert_kernel(be_ref, nu_ref, xs_ref, wg_ref, wu_ref, wd_ref, ys_ref, wgb, wub, wdb):
    i = pl.program_id(0)
    prev = be_ref[jnp.maximum(i - 1, 0)]
    fresh = (i == 0) | (be_ref[i] != prev)

    @pl.when(fresh)
    def _():
        wgb[...] = wg_ref[...].astype(BF16)
        wub[...] = wu_ref[...].astype(BF16)
        wdb[...] = wd_ref[...].astype(BF16)

    @pl.when(i < nu_ref[0])
    def _():
        blk = xs_ref.shape[0]
        x = xs_ref[...].reshape(blk, wgb.shape[0]).astype(BF16)
        g = jnp.dot(x, wgb[...], preferred_element_type=F32)
        u = jnp.dot(x, wub[...], preferred_element_type=F32)
        mid = (_silu(g) * u).astype(BF16)
        ys_ref[...] = jnp.dot(mid, wdb[...], preferred_element_type=F32).reshape(ys_ref.shape)

    @pl.when(i >= nu_ref[0])
    def _():
        ys_ref[...] = jnp.zeros(ys_ref.shape, ys_ref.dtype)


def _experts(xs, block_expert, n_used, w_gate, w_up, w_down, layer, blk):
    rows = xs.shape[0]
    tile = xs.shape[1:]
    d, de = w_gate.shape[-2:]
    n_blocks = rows // blk
    return pl.pallas_call(
        _expert_kernel,
        grid_spec=pltpu.PrefetchScalarGridSpec(
            num_scalar_prefetch=2,
            grid=(n_blocks,),
            in_specs=[
                pl.BlockSpec((blk,) + tile, lambda i, be, nu: (jnp.minimum(i, nu[0] - 1), 0, 0)),
                pl.BlockSpec((None, None, d, de), lambda i, be, nu: (layer, be[i], 0, 0)),
                pl.BlockSpec((None, None, d, de), lambda i, be, nu: (layer, be[i], 0, 0)),
                pl.BlockSpec((None, None, de, d), lambda i, be, nu: (layer, be[i], 0, 0)),
            ],
            out_specs=pl.BlockSpec((blk,) + tile, lambda i, be, nu: (i, 0, 0)),
            scratch_shapes=[
                pltpu.VMEM((d, de), BF16),
                pltpu.VMEM((d, de), BF16),
                pltpu.VMEM((de, d), BF16),
            ],
        ),
        out_shape=jax.ShapeDtypeStruct((rows,) + tile, F32),
        compiler_params=_params("arbitrary"),
        name="moe_experts",
    )(block_expert, n_used, xs, w_gate, w_up, w_down)


def _combine_kernel(dest_ref, next_ref, x_ref, mod_ref, meta_ref, ys_ref, o_ref, ybuf, sems, *, tm, d):
    step = pl.program_id(0) * pl.num_programs(1) + pl.program_id(1)
    n_steps = pl.num_programs(0) * pl.num_programs(1)
    slot = step % 2

    def row_copy(idx_ref, buf, r, k):
        return pltpu.make_async_copy(ys_ref.at[idx_ref[r * TOP_K + k]], ybuf.at[buf, k, r], sems.at[buf])

    def start_all(idx_ref, buf):
        def body(g, carry):
            for s in range(DMA_GROUP):
                for k in range(TOP_K):
                    row_copy(idx_ref, buf, g * DMA_GROUP + s, k).start(priority=k % 2)
            return carry
        lax.fori_loop(0, tm // DMA_GROUP, body, 0)

    def wait_all(idx_ref, buf):
        def body(g, carry):
            for s in range(DMA_GROUP):
                for k in range(TOP_K):
                    row_copy(idx_ref, buf, g * DMA_GROUP + s, k).wait()
            return carry
        lax.fori_loop(0, tm // DMA_GROUP, body, 0)

    @pl.when(step == 0)
    def _():
        start_all(dest_ref, 0)

    @pl.when(step + 1 < n_steps)
    def _():
        start_all(next_ref, 1 - slot)

    wait_all(dest_ref, slot)
    meta = meta_ref[...]
    y = meta[:, 4:5] * ybuf[slot, 0].reshape(tm, d) + meta[:, 5:6] * ybuf[slot, 1].reshape(tm, d)
    o_ref[...] = x_ref[...] + mod_ref[:, 5 * d:6 * d] * y


def _combine(dest, x, mod, meta, ys, tm):
    bsz, seq, d = x.shape
    nb = seq // tm
    last = bsz * nb - 1
    return pl.pallas_call(
        functools.partial(_combine_kernel, tm=tm, d=d),
        grid=(bsz, nb),
        in_specs=[
            pl.BlockSpec((tm * TOP_K,), lambda b, i: (b * nb + i,), memory_space=pltpu.SMEM),
            pl.BlockSpec((tm * TOP_K,), lambda b, i: (jnp.minimum(b * nb + i + 1, last),), memory_space=pltpu.SMEM),
            pl.BlockSpec((None, tm, d), lambda b, i: (b, i, 0)),
            pl.BlockSpec((None, 1, mod.shape[-1]), lambda b, i: (b, 0, 0)),
            pl.BlockSpec((None, tm, 8), lambda b, i: (b, i, 0)),
            pl.BlockSpec(memory_space=pl.ANY),
        ],
        out_specs=pl.BlockSpec((None, tm, d), lambda b, i: (b, i, 0)),
        scratch_shapes=[
            pltpu.VMEM((2, TOP_K, tm) + ys.shape[1:], F32),
            pltpu.SemaphoreType.DMA((2,)),
        ],
        out_shape=jax.ShapeDtypeStruct((bsz, seq, d), F32),
        compiler_params=_params("arbitrary", "arbitrary"),
        name="moe_combine",
    )(dest, dest, x, mod, meta, ys)


def _mixer_out_and_moe(o, w_out, x, mod, gain, w_group, b_group, w_expert, b_expert, w_gate, w_up, w_down, layer):
    bsz, seq, d = x.shape
    groups = w_group.shape[1]
    n_exp = w_expert.shape[1]
    per_group = n_exp // groups
    n_tok = bsz * seq
    m = n_tok * TOP_K
    blk = EXPERT_BLOCK
    tm = TOKEN_TILE

    pad = LANES - groups - n_exp
    w_router = jnp.concatenate([w_group, w_expert, jnp.zeros((d, pad), F32)], axis=1)
    w_router_hi = w_router.astype(BF16)
    w_router = jnp.stack([w_router_hi, (w_router - w_router_hi.astype(F32)).astype(BF16)])
    b_router = jnp.concatenate([b_group, b_expert, jnp.zeros((pad,), F32)]).reshape(1, LANES)
    x, h, meta, cnt = _router(o, w_out, x, mod, gain, w_router, b_router, groups, per_group, tm)

    counts = cnt[0, groups:groups + n_exp].astype(I32)
    padded = (counts + blk - 1) // blk * blk
    pad_end = jnp.cumsum(padded)
    pad_start = pad_end - padded
    n_blocks = -(-m // blk) + n_exp
    block_start = jnp.arange(n_blocks, dtype=I32) * blk
    block_expert = jnp.minimum(jnp.sum((pad_end[None, :] <= block_start[:, None]).astype(I32), axis=1), n_exp - 1)
    n_used = (pad_end[-1] // blk).astype(I32).reshape(1)
    meta_flat = meta.reshape(n_tok, 8)
    e_idx = meta_flat[:, 0:TOP_K].astype(I32)
    rank = meta_flat[:, TOP_K:2 * TOP_K].astype(I32)
    start_of = jnp.sum(jnp.where(e_idx[..., None] == jnp.arange(n_exp, dtype=I32), pad_start, 0), axis=-1)
    dest = (start_of + rank).reshape(m)

    pad_info = jnp.concatenate([pad_end, padded, n_used]).astype(I32)
    xs = _dispatch(h, dest, pad_info, n_blocks * blk, DISPATCH_TILE, blk, n_exp)
    ys = _experts(xs, block_expert, n_used, w_gate, w_up, w_down, layer, blk)
    return _combine(dest, x, mod, meta, ys, tm)


def kernel(x, c, mod_w, mod_b, norm_mix_g, norm_ffn_g, gdn_w_in, gdn_conv_w, gdn_a_log, gdn_dt_bias, gdn_out_norm_g, gdn_w_out, kv_mod_w, kv_mod_b, kv_norm_g, kv_w, kv_forget_b, k_norm_g, fox_w_qz, fox_q_norm_g, fox_w_out, moe_w_group, moe_b_group, moe_w_expert, moe_b_expert, moe_w_gate, moe_w_up, moe_w_down):
    bsz, seq, d = x.shape
    depth = mod_w.shape[0]
    n_a = gdn_w_in.shape[0]
    gdn_heads = gdn_a_log.shape[1]
    gdn_inner = gdn_heads * GDN_HEAD_DIM
    fox_heads = kv_forget_b.shape[0]
    fox_inner = fox_heads * FOX_HEAD_DIM
    tm = TOKEN_TILE

    mod_all = _mod_vectors(c, mod_w, mod_b).reshape(depth, bsz, 1, 6 * d)
    kv_mod = _mod_vectors(c, kv_mod_w[None], kv_mod_b[None]).reshape(bsz, 1, 2 * d)
    bd = _block_diag_ones(2 * FOX_HEAD_DIM, FOX_HEAD_DIM)

    ka_sh = vt_sh = ft_sh = None
    for layer in range(depth):
        mod = mod_all[layer]
        gain_mix = norm_mix_g[layer].reshape(1, d)
        if layer < n_a:
            w_in = gdn_w_in[layer]
            qkv, sz, ab = _gdn_in_proj(x, mod, gain_mix, w_in[:, :4 * gdn_inner].astype(BF16),
                                       w_in[:, 4 * gdn_inner:].astype(BF16), gdn_conv_w[layer], gdn_heads, tm)
            o = _gdn_core(qkv, sz, ab, gdn_a_log[layer], gdn_dt_bias[layer],
                          gdn_out_norm_g[layer], gdn_heads, GDN_TIME_BLOCK)
            w_out = gdn_w_out[layer]
        else:
            j = layer - n_a
            wqz = fox_w_qz[j]
            qg = jnp.tile(fox_q_norm_g[j], fox_heads).reshape(1, fox_inner)
            qt, z = _fox_qz_proj(x, mod, gain_mix, wqz[:, :fox_inner].astype(BF16),
                                 wqz[:, fox_inner:].astype(BF16), qg, bd, ft_sh, fox_heads, tm)
            o = _fox_attention(qt, ka_sh, vt_sh, z, FOX_BLOCK)
            w_out = fox_w_out[j]
        x = _mixer_out_and_moe(o, w_out.astype(BF16), x, mod, norm_ffn_g[layer].reshape(1, d),
                               moe_w_group[layer], moe_b_group[layer], moe_w_expert[layer], moe_b_expert[layer],
                               moe_w_gate, moe_w_up, moe_w_down, layer)
        if layer == n_a - 1:
            kg = jnp.tile(k_norm_g, fox_heads).reshape(1, fox_inner)
            ka_sh, vt_sh, ft_sh = _shared_kv(
                x, kv_mod, kv_norm_g.reshape(1, d), kv_w[:, :fox_inner].astype(BF16),
                kv_w[:, fox_inner:2 * fox_inner].astype(BF16), kv_w[:, 2 * fox_inner:].astype(BF16),
                kv_forget_b.reshape(1, fox_heads), kg, bd, fox_heads, tm)
    return x
```

```python
import functools

import jax
import jax.numpy as jnp
import numpy as np
from jax import lax
from jax.experimental import pallas as pl
from jax.experimental.pallas import tpu as pltpu

F32 = jnp.float32
BF16 = jnp.bfloat16
I32 = jnp.int32

EPS = 1e-6
GDN_CHUNK = 64
GDN_HEAD_DIM = 128
FOX_HEAD_DIM = 64
TOP_K = 2
LANES = 128
SUBLANES = 8
VMEM_LIMIT = 56 * 1024 * 1024
HIGHEST = lax.Precision.HIGHEST

TOKEN_TILE = 512
EXPERT_BLOCK = 512
DISPATCH_TILE = 2 * TOKEN_TILE
FOX_BLOCK = 512
GDN_TIME_BLOCK = 4 * GDN_CHUNK

NT_DIMS = (((1,), (1,)), ((), ()))
TN_DIMS = (((0,), (0,)), ((), ()))


def _params(*sem):
    return pltpu.CompilerParams(dimension_semantics=sem, vmem_limit_bytes=VMEM_LIMIT)


def _sigmoid(x):
    return 1.0 / (1.0 + jnp.exp(-x))


def _silu(x):
    return x * _sigmoid(x)


def _softplus(x):
    return jnp.maximum(x, 0.0) + jnp.log(1.0 + jnp.exp(-jnp.abs(x)))


def _modulated(x, gain, scale, shift):
    ms = jnp.mean(x * x, axis=-1, keepdims=True)
    y = x * lax.rsqrt(ms + EPS)
    return (y * gain) * (1.0 + scale) + shift


def _bdot(a, b, dims=None):
    a = a.astype(BF16)
    b = b.astype(BF16)
    if dims is None:
        return jnp.dot(a, b, preferred_element_type=F32)
    return lax.dot_general(a, b, dims, preferred_element_type=F32)


def _hdot(a, b, dims=None):
    if dims is None:
        return jnp.dot(a, b, preferred_element_type=F32, precision=HIGHEST)
    return lax.dot_general(a, b, dims, preferred_element_type=F32, precision=HIGHEST)


def _iota2(shape, dim):
    return lax.broadcasted_iota(I32, shape, dim)


def _mod_kernel(c_ref, w_ref, b_ref, o_ref):
    c = c_ref[...]
    o_ref[...] = _hdot(_silu(c), w_ref[...]) + b_ref[...]


def _mod_vectors(c, w, b):
    n_layers, d, n = w.shape
    bsz = c.shape[0]
    tn = 1536 if n % 1536 == 0 else n
    return pl.pallas_call(
        _mod_kernel,
        grid=(n_layers, n // tn),
        in_specs=[
            pl.BlockSpec((bsz, d), lambda l, j: (0, 0)),
            pl.BlockSpec((None, d, tn), lambda l, j: (l, 0, j)),
            pl.BlockSpec((None, 1, tn), lambda l, j: (l, 0, j)),
        ],
        out_specs=pl.BlockSpec((None, bsz, tn), lambda l, j: (l, 0, j)),
        out_shape=jax.ShapeDtypeStruct((n_layers, bsz, n), F32),
        compiler_params=_params("parallel", "parallel"),
        name="mod_vectors",
    )(c, w, b.reshape(n_layers, 1, n))


def _gdn_in_kernel(x_ref, mod_ref, g_ref, w_ref, wab_ref, cw_ref, qkv_ref, sz_ref, oab_ref, xpad_ref,
                   *, d, tm, heads, dh, width):
    inner = heads * dh
    halo = 8
    i = pl.program_id(1)

    @pl.when(i == 0)
    def _():
        xpad_ref[0:halo, :] = jnp.zeros((halo, 3 * inner), F32)

    h = _modulated(x_ref[...], g_ref[...], mod_ref[:, d:2 * d], mod_ref[:, 0:d]).astype(BF16)
    oab_ref[...] = jnp.dot(h, wab_ref[...], preferred_element_type=F32)
    z = jnp.dot(h, w_ref[:, 3 * inner:4 * inner], preferred_element_type=F32)
    sz_ref[...] = _silu(z).astype(sz_ref.dtype)
    xpad_ref[halo:halo + tm, :] = jnp.dot(h, w_ref[:, 0:3 * inner], preferred_element_type=F32)
    for j in range(3 * heads):
        cols = slice(j * dh, (j + 1) * dh)
        acc = xpad_ref[halo:halo + tm, cols] * cw_ref[width - 1:width, cols]
        for s in range(1, width):
            acc = acc + xpad_ref[halo - s:halo - s + tm, cols] * cw_ref[width - 1 - s:width - s, cols]
        y = _silu(acc)
        if j < 2 * heads:
            y = y * lax.rsqrt(jnp.sum(y * y, axis=-1, keepdims=True) + EPS)
        if j < heads:
            y = y * (dh ** -0.5)
        qkv_ref[:, cols] = y.astype(qkv_ref.dtype)
    xpad_ref[0:halo, :] = xpad_ref[tm:tm + halo, :]


def _gdn_in_proj(x, mod, gain, w_main, w_ab, conv_w, heads, tm):
    bsz, seq, d = x.shape
    dh = GDN_HEAD_DIM
    inner = heads * dh
    nab = w_ab.shape[1]
    width = conv_w.shape[0]
    return pl.pallas_call(
        functools.partial(_gdn_in_kernel, d=d, tm=tm, heads=heads, dh=dh, width=width),
        grid=(bsz, seq // tm),
        in_specs=[
            pl.BlockSpec((None, tm, d), lambda b, i: (b, i, 0)),
            pl.BlockSpec((None, 1, mod.shape[-1]), lambda b, i: (b, 0, 0)),
            pl.BlockSpec((1, d), lambda b, i: (0, 0)),
            pl.BlockSpec((d, 4 * inner), lambda b, i: (0, 0)),
            pl.BlockSpec((d, nab), lambda b, i: (0, 0)),
            pl.BlockSpec((width, 3 * inner), lambda b, i: (0, 0)),
        ],
        out_specs=[
            pl.BlockSpec((None, tm, 3 * inner), lambda b, i: (b, i, 0)),
            pl.BlockSpec((None, tm, inner), lambda b, i: (b, i, 0)),
            pl.BlockSpec((None, tm, nab), lambda b, i: (b, i, 0)),
        ],
        out_shape=[
            jax.ShapeDtypeStruct((bsz, seq, 3 * inner), BF16),
            jax.ShapeDtypeStruct((bsz, seq, inner), BF16),
            jax.ShapeDtypeStruct((bsz, seq, nab), F32),
        ],
        scratch_shapes=[pltpu.VMEM((8 + tm, 3 * inner), F32)],
        compiler_params=_params("parallel", "arbitrary"),
        name="gdn_in_proj",
    )(x, mod, gain, w_main, w_ab, conv_w)


def _gdn_kernel(q_ref, k_ref, v_ref, sz_ref, ab_ref, alog_ref, dtb_ref, og_ref, o_ref, state_ref,
                *, tb, heads, dh):
    c = GDN_CHUNK
    pairs = heads // 2
    i = pl.program_id(1)

    @pl.when(i == 0)
    def _():
        state_ref[...] = jnp.zeros(state_ref.shape, F32)

    ab = ab_ref[...]
    beta_all = _sigmoid(ab[:, heads:2 * heads])
    g_all = -jnp.exp(alog_ref[...]) * _softplus(ab[:, 0:heads] + dtb_ref[...])
    og = og_ref[...]

    row = _iota2((c, 2 * c), 0)
    lane = _iota2((c, 2 * c), 1)
    first = lane < c
    col = jnp.where(first, lane, lane - c)
    incl = row >= col
    strict = row > col
    eye_p = jnp.where(row == col, 1.0, 0.0).astype(F32)
    tri = jnp.where(_iota2((c, c), 0) >= _iota2((c, c), 1), 1.0, 0.0).astype(F32)
    bd_small = (_iota2((2 * c, 2 * c), 0) < c) == (_iota2((2 * c, 2 * c), 1) < c)
    bd_wide = (_iota2((2 * c, 2 * dh), 0) < c) == (_iota2((2 * c, 2 * dh), 1) < dh)
    first_h = _iota2((pairs, 2 * c), 1) < c
    sel0 = jnp.where(_iota2((pairs, heads), 1) == 2 * _iota2((pairs, heads), 0), 1.0, 0.0).astype(F32)
    sel1 = jnp.where(_iota2((pairs, heads), 1) == 2 * _iota2((pairs, heads), 0) + 1, 1.0, 0.0).astype(F32)

    def block_diag(x, mask):
        return jnp.where(mask, jnp.concatenate([x, x], axis=0), 0.0).astype(BF16)

    def pair_cols(x, p):
        return jnp.concatenate([jnp.broadcast_to(x[:, 2 * p:2 * p + 1], (c, dh)),
                                jnp.broadcast_to(x[:, 2 * p + 1:2 * p + 2], (c, dh))], axis=1)

    units = [(ci, p) for ci in range(tb // c) for p in range(pairs)]
    gcums, glasts, pk, tt, attn, w_u, qg, kdec = {}, {}, {}, {}, {}, {}, {}, {}

    for ci in range(tb // c):
        r0 = ci * c
        gcum = _hdot(tri, g_all[r0:r0 + c, :])
        gc2 = jnp.concatenate([gcum, gcum], axis=0)
        gt = jnp.where(first_h, _hdot(sel0, gc2, NT_DIMS), _hdot(sel1, gc2, NT_DIMS))
        glast = gcum[c - 1:c, :]
        gcums[ci] = gcum
        glasts[ci] = glast
        for p in range(pairs):
            u = (ci, p)
            cols = slice(2 * p * dh, (2 * p + 2) * dh)
            kp = k_ref[r0:r0 + c, cols].astype(F32)
            qp = q_ref[r0:r0 + c, cols].astype(F32)
            vp = v_ref[r0:r0 + c, cols].astype(F32)
            gcol = jnp.where(first, gcum[:, 2 * p:2 * p + 1], gcum[:, 2 * p + 1:2 * p + 2])
            decay = jnp.exp(jnp.where(incl, gcol - gt[p:p + 1, :], -jnp.inf))
            beta2 = pair_cols(beta_all[r0:r0 + c, :], p)
            gcum2 = pair_cols(gcum, p)
            egc2 = jnp.exp(gcum2)
            kb = kp * beta2
            y = block_diag(kp, bd_wide)
            kq = _bdot(jnp.concatenate([kb, qp], axis=0), y, NT_DIMS)
            a_mat = jnp.where(strict, kq[0:c, :] * decay, 0.0)
            attn[u] = kq[c:2 * c, :] * decay
            pk[u] = -a_mat
            tt[u] = eye_p - a_mat
            kbg = kb * egc2
            vb = vp * beta2
            w_u[u] = jnp.concatenate(
                [jnp.concatenate([kbg[:, 0:dh], vb[:, 0:dh]], axis=1),
                 jnp.concatenate([kbg[:, dh:2 * dh], vb[:, dh:2 * dh]], axis=1)], axis=0).astype(BF16)
            qg[u] = qp * egc2
            kdec[u] = kp * jnp.exp(pair_cols(jnp.broadcast_to(glast, (c, heads)), p) - gcum2)

    span = 2
    while span <= c:
        last = span == c
        for u in units:
            bd = block_diag(pk[u], bd_small)
            if span == 2:
                pk[u] = _bdot(pk[u], bd)
            elif last:
                tt[u] = tt[u] + _bdot(tt[u], bd)
            else:
                both = _bdot(jnp.concatenate([pk[u], tt[u]], axis=0), bd)
                pk[u] = both[0:c, :]
                tt[u] = tt[u] + both[c:2 * c, :]
        span *= 2
    for u in units:
        t = tt[u]
        lhs = jnp.concatenate([jnp.where(first, t, 0.0), jnp.where(first, 0.0, t)], axis=0)
        w_u[u] = _bdot(lhs, w_u[u])

    for ci in range(tb // c):
        r0 = ci * c
        for p in range(pairs):
            u = (ci, p)
            wu = w_u[u]
            st, wq, vn = [], [], []
            for s in range(2):
                h = 2 * p + s
                st.append(state_ref[h])
                wq.append(_bdot(jnp.concatenate([wu[s * c:(s + 1) * c, 0:dh], qg[u][:, s * dh:(s + 1) * dh]], axis=0),
                                st[s]))
                vn.append(wu[s * c:(s + 1) * c, dh:2 * dh] - wq[s][0:c, :])
            vn2 = jnp.concatenate(vn, axis=0).astype(BF16)
            for s in range(2):
                h = 2 * p + s
                am = jnp.where(first, attn[u], 0.0) if s == 0 else jnp.where(first, 0.0, attn[u])
                o = wq[s][c:2 * c, :] + _bdot(am, vn2)
                gl = glasts[ci][:, h:h + 1]
                state_ref[h] = st[s] * jnp.exp(gl) + _bdot(kdec[u][:, s * dh:(s + 1) * dh], vn[s], TN_DIMS)
                on = o * lax.rsqrt(jnp.mean(o * o, axis=-1, keepdims=True) + EPS) * og
                szh = sz_ref[r0:r0 + c, h * dh:(h + 1) * dh].astype(F32)
                o_ref[r0:r0 + c, h * dh:(h + 1) * dh] = (on * szh).astype(o_ref.dtype)


def _gdn_core(qkv, sz, ab, a_log, dt_bias, out_g, heads, tb):
    bsz, seq, _ = qkv.shape
    dh = GDN_HEAD_DIM
    inner = heads * dh
    return pl.pallas_call(
        functools.partial(_gdn_kernel, tb=tb, heads=heads, dh=dh),
        grid=(bsz, seq // tb),
        in_specs=[
            pl.BlockSpec((None, tb, inner), lambda b, i: (b, i, 0)),
            pl.BlockSpec((None, tb, inner), lambda b, i: (b, i, 1)),
            pl.BlockSpec((None, tb, inner), lambda b, i: (b, i, 2)),
            pl.BlockSpec((None, tb, inner), lambda b, i: (b, i, 0)),
            pl.BlockSpec((None, tb, 2 * heads), lambda b, i: (b, i, 0)),
            pl.BlockSpec((1, heads), lambda b, i: (0, 0)),
            pl.BlockSpec((1, heads), lambda b, i: (0, 0)),
            pl.BlockSpec((1, dh), lambda b, i: (0, 0)),
        ],
        out_specs=pl.BlockSpec((None, tb, inner), lambda b, i: (b, i, 0)),
        out_shape=jax.ShapeDtypeStruct((bsz, seq, inner), BF16),
        scratch_shapes=[pltpu.VMEM((heads, dh, dh), F32)],
        compiler_params=_params("parallel", "arbitrary"),
        name="gdn_core",
    )(qkv, qkv, qkv, sz, ab, a_log.reshape(1, heads), dt_bias.reshape(1, heads), out_g.reshape(1, dh))


def _head_rms(x, bd, gain, dh):
    w = bd.shape[0]
    sq = (x * x).astype(BF16)
    ss = jnp.concatenate([jnp.dot(sq[:, c:c + w], bd, preferred_element_type=F32)
                          for c in range(0, x.shape[1], w)], axis=1)
    return x * lax.rsqrt(ss * (1.0 / dh) + EPS) * gain


def _block_diag_ones(n, blk):
    r = jnp.arange(n, dtype=I32) // blk
    return (r[:, None] == r[None, :]).astype(BF16)


N_SPLIT = 3


def _bias_base(head, dh):
    return dh if head % 2 == 0 else 0


def _split3(x):
    hi = x.astype(BF16)
    r = x - hi.astype(F32)
    mid = r.astype(BF16)
    lo = (r - mid.astype(F32)).astype(BF16)
    return hi, mid, lo


def _key_bias_placement(heads, dh):
    place = np.zeros((N_SPLIT, heads, heads * 2 * dh), np.float32)
    for j in range(N_SPLIT):
        for h in range(heads):
            place[j, h, h * 2 * dh + _bias_base(h, dh) + j] = 1.0
    return jnp.asarray(place, BF16)


def _kv_kernel(x_ref, mod_ref, g_ref, wk_ref, wv_ref, wf_ref, fb_ref, kg_ref, bd_ref, place_ref,
               ka_ref, vt_ref, ft_ref, carry_ref, *, d, tm, heads, dh):
    i = pl.program_id(1)

    @pl.when(i == 0)
    def _():
        carry_ref[...] = jnp.zeros(carry_ref.shape, F32)

    h = _modulated(x_ref[...], g_ref[...], mod_ref[:, d:2 * d], mod_ref[:, 0:d]).astype(BF16)
    kraw = jnp.dot(h, wk_ref[...], preferred_element_type=F32)
    kn = _head_rms(kraw, bd_ref[...], kg_ref[...], dh)
    v = jnp.dot(h, wv_ref[...], preferred_element_type=F32).astype(BF16)
    f = jnp.dot(h, wf_ref[...], preferred_element_type=F32) + fb_ref[...]
    log_f = -_softplus(-f)
    tri = jnp.where(_iota2((tm, tm), 0) >= _iota2((tm, tm), 1), 1.0, 0.0).astype(F32)
    fcum = _hdot(tri, log_f) + carry_ref[...]
    carry_ref[...] = fcum[tm - 1:tm, :]
    eye_h = jnp.where(_iota2((heads, heads), 0) == _iota2((heads, heads), 1), 1.0, 0.0).astype(F32)
    fcum_t = _hdot(eye_h, fcum, NT_DIMS)
    for p in range(heads // 2):
        ft_ref[p] = fcum_t[2 * p:2 * p + 2, :]

    pieces = _split3(-fcum)
    lane = _iota2((tm, 2 * dh), 1)
    for hh in range(heads):
        p = hh // 2
        base = _bias_base(hh, dh)
        real = (lane < dh) if hh % 2 == 0 else (lane >= dh)
        ones = (lane >= base + N_SPLIT) & (lane < base + 2 * N_SPLIT)
        blk = jnp.where(real, kn[:, 2 * p * dh:(2 * p + 2) * dh], jnp.where(ones, 1.0, 0.0))
        for j in range(N_SPLIT):
            blk = jnp.where(lane == base + j, pieces[j][:, hh:hh + 1].astype(F32), blk)
        ka_ref[hh] = blk.astype(ka_ref.dtype)

    eye = jnp.where(_iota2((2 * dh, 2 * dh), 0) == _iota2((2 * dh, 2 * dh), 1), 1.0, 0.0).astype(BF16)
    rowi = _iota2((2 * dh, tm), 0)
    for p in range(heads // 2):
        vt = lax.dot_general(eye, v[:, 2 * p * dh:(2 * p + 2) * dh], NT_DIMS, preferred_element_type=F32)
        vt_ref[2 * p] = jnp.where(rowi < dh, vt, 1.0).astype(vt_ref.dtype)
        vt_ref[2 * p + 1] = jnp.where(rowi >= dh, vt, 1.0).astype(vt_ref.dtype)


def _shared_kv(x, mod, gain, wk, wv, wf, fb, kgain, bd, heads, tm):
    bsz, seq, d = x.shape
    dh = FOX_HEAD_DIM
    inner = heads * dh
    pairs = heads // 2
    place = _key_bias_placement(heads, dh)
    return pl.pallas_call(
        functools.partial(_kv_kernel, d=d, tm=tm, heads=heads, dh=dh),
        grid=(bsz, seq // tm),
        in_specs=[
            pl.BlockSpec((None, tm, d), lambda b, i: (b, i, 0)),
            pl.BlockSpec((None, 1, mod.shape[-1]), lambda b, i: (b, 0, 0)),
            pl.BlockSpec((1, d), lambda b, i: (0, 0)),
            pl.BlockSpec((d, inner), lambda b, i: (0, 0)),
            pl.BlockSpec((d, inner), lambda b, i: (0, 0)),
            pl.BlockSpec((d, heads), lambda b, i: (0, 0)),
            pl.BlockSpec((1, heads), lambda b, i: (0, 0)),
            pl.BlockSpec((1, inner), lambda b, i: (0, 0)),
            pl.BlockSpec(bd.shape, lambda b, i: (0, 0)),
            pl.BlockSpec(place.shape, lambda b, i: (0, 0, 0)),
        ],
        out_specs=[
            pl.BlockSpec((None, heads, tm, 2 * dh), lambda b, i: (b, 0, i, 0)),
            pl.BlockSpec((None, heads, 2 * dh, tm), lambda b, i: (b, 0, 0, i)),
            pl.BlockSpec((None, pairs, 2, tm), lambda b, i: (b, 0, 0, i)),
        ],
        out_shape=[
            jax.ShapeDtypeStruct((bsz, heads, seq, 2 * dh), BF16),
            jax.ShapeDtypeStruct((bsz, heads, 2 * dh, seq), BF16),
            jax.ShapeDtypeStruct((bsz, pairs, 2, seq), F32),
        ],
        scratch_shapes=[pltpu.VMEM((1, heads), F32)],
        compiler_params=_params("parallel", "arbitrary"),
        name="shared_kv",
    )(x, mod, gain, wk, wv, wf, fb, kgain, bd, place)


def _fox_qz_kernel(x_ref, mod_ref, g_ref, wq_ref, wz_ref, qg_ref, bd_ref, ft_ref, qt_ref, z_ref,
                   *, d, dh, heads, tm):
    h = _modulated(x_ref[...], g_ref[...], mod_ref[:, d:2 * d], mod_ref[:, 0:d]).astype(BF16)
    qraw = jnp.dot(h, wq_ref[...], preferred_element_type=F32)
    qn = (_head_rms(qraw, bd_ref[...], qg_ref[...], dh) * (dh ** -0.5)).astype(BF16)
    z_ref[...] = jnp.dot(h, wz_ref[...], preferred_element_type=F32).astype(z_ref.dtype)
    eye = jnp.where(_iota2((2 * dh, 2 * dh), 0) == _iota2((2 * dh, 2 * dh), 1), 1.0, 0.0).astype(BF16)
    rowi = _iota2((2 * dh, tm), 0)
    for p in range(heads // 2):
        qt = lax.dot_general(eye, qn[:, 2 * p * dh:(2 * p + 2) * dh], NT_DIMS, preferred_element_type=F32)
        for s in range(2):
            hh = 2 * p + s
            base = _bias_base(hh, dh)
            real = (rowi < dh) if s == 0 else (rowi >= dh)
            aug = jnp.where(real, qt, 0.0)
            aug = jnp.where((rowi >= base) & (rowi < base + N_SPLIT), 1.0, aug)
            pieces = _split3(ft_ref[p, s:s + 1, :])
            for j in range(N_SPLIT):
                aug = jnp.where(rowi == base + N_SPLIT + j, pieces[j].astype(F32), aug)
            qt_ref[hh] = aug.astype(qt_ref.dtype)


def _fox_qz_proj(x, mod, gain, wq, wz, qgain, bd, ft, heads, tm):
    bsz, seq, d = x.shape
    dh = FOX_HEAD_DIM
    inner = wq.shape[1]
    pairs = heads // 2
    return pl.pallas_call(
        functools.partial(_fox_qz_kernel, d=d, dh=dh, heads=heads, tm=tm),
        grid=(bsz, seq // tm),
        in_specs=[
            pl.BlockSpec((None, tm, d), lambda b, i: (b, i, 0)),
            pl.BlockSpec((None, 1, mod.shape[-1]), lambda b, i: (b, 0, 0)),
            pl.BlockSpec((1, d), lambda b, i: (0, 0)),
            pl.BlockSpec((d, inner), lambda b, i: (0, 0)),
            pl.BlockSpec((d, inner), lambda b, i: (0, 0)),
            pl.BlockSpec((1, inner), lambda b, i: (0, 0)),
            pl.BlockSpec(bd.shape, lambda b, i: (0, 0)),
            pl.BlockSpec((None, pairs, 2, tm), lambda b, i: (b, 0, 0, i)),
        ],
        out_specs=[
            pl.BlockSpec((None, heads, 2 * dh, tm), lambda b, i: (b, 0, 0, i)),
            pl.BlockSpec((None, tm, inner), lambda b, i: (b, i, 0)),
        ],
        out_shape=[
            jax.ShapeDtypeStruct((bsz, heads, 2 * dh, seq), BF16),
            jax.ShapeDtypeStruct((bsz, seq, inner), BF16),
        ],
        compiler_params=_params("parallel", "parallel"),
        name="fox_qz_proj",
    )(x, mod, gain, wq, wz, qgain, bd, ft)


def _fox_kernel(qt_ref, ka_ref, vt_ref, z_ref, o_ref, s_ref, p_ref, acc_ref, *, tq, dh):
    i = pl.program_id(2)
    acc_ref[...] = jnp.zeros(acc_ref.shape, F32)
    key_idx = _iota2((tq, LANES), 0)
    qry_idx = _iota2((tq, LANES), 1)

    wide = 2 * LANES
    units = [(h, c) for h in range(2) for c in range(tq // wide)]

    def scores(kb, slot, u):
        h, c = u
        start = pl.multiple_of(kb * tq, tq)
        s = jnp.dot(ka_ref[h, pl.ds(start, tq), :], qt_ref[h, :, c * wide:(c + 1) * wide],
                    preferred_element_type=F32)
        for t in range(wide // LANES):
            s_ref[slot, h, c * (wide // LANES) + t] = s[:, t * LANES:(t + 1) * LANES]

    def softmax_pv(kb, slot, u, ms, masked, m_out):
        h, c = u
        start = pl.multiple_of(kb * tq, tq)
        alphas = []
        for t in range(wide // LANES):
            q0 = c * wide + t * LANES
            cols = slice(q0, q0 + LANES)
            s = s_ref[slot, h, q0 // LANES]
            if masked:
                s = jnp.where(key_idx <= qry_idx + q0, s, -jnp.inf)
            m_old = ms[h][:, cols]
            parts = [s[r:r + 64, :] for r in range(0, tq, 64)]
            while len(parts) > 1:
                parts = [jnp.maximum(parts[n], parts[n + 1]) for n in range(0, len(parts), 2)]
            m_new = jnp.maximum(m_old, jnp.max(parts[0], axis=0, keepdims=True))
            p_ref[h, q0 // LANES] = jnp.exp(s - m_new).astype(BF16)
            alphas.append(jnp.exp(m_old - m_new))
            m_out[(h, q0)] = m_new
        cols = slice(c * wide, (c + 1) * wide)
        alpha = jnp.concatenate(alphas, axis=1)
        p = jnp.concatenate([p_ref[h, c * (wide // LANES) + t] for t in range(wide // LANES)], axis=1)
        acc_ref[h, :, cols] = alpha * acc_ref[h, :, cols] + jnp.dot(
            vt_ref[h, :, pl.ds(start, tq)], p, preferred_element_type=F32)

    def step(kb, slot, ms, masked, prefetch):
        m_out = {}
        if prefetch:
            for u in units:
                scores(kb + 1, 1 - slot, u)
        for u in units:
            softmax_pv(kb, slot, u, ms, masked, m_out)
        return tuple(jnp.concatenate([m_out[(h, q0)] for q0 in range(0, tq, LANES)], axis=1) for h in range(2))

    def by_parity(kb, ms, masked, prefetch):
        return lax.cond(kb % 2 == 0,
                        lambda ms: step(kb, 0, ms, masked, prefetch),
                        lambda ms: step(kb, 1, ms, masked, prefetch), ms)

    for u in units:
        scores(0, 0, u)
    neg = jnp.full((1, tq), -jnp.inf, F32)
    ms = lax.fori_loop(0, i, lambda kb, ms: by_parity(kb, ms, False, True), (neg, neg))
    by_parity(i, ms, True, False)
    a0 = acc_ref[0]
    a1 = acc_ref[1]
    num = jnp.concatenate([a0[0:dh, :], a1[dh:2 * dh, :]], axis=0)
    den = jnp.concatenate([a0[dh:2 * dh, :], a1[0:dh, :]], axis=0)
    o = jnp.transpose(num / den)
    o_ref[...] = (o * _sigmoid(z_ref[...].astype(F32))).astype(o_ref.dtype)


def _fox_attention(qt, ka, vt, z, tq):
    bsz, heads, feat, seq = qt.shape
    dh = FOX_HEAD_DIM
    pairs = heads // 2
    return pl.pallas_call(
        functools.partial(_fox_kernel, tq=tq, dh=dh),
        grid=(bsz, pairs, seq // tq),
        in_specs=[
            pl.BlockSpec((None, 2, feat, tq), lambda b, p, i: (b, p, 0, i)),
            pl.BlockSpec((None, 2, seq, feat), lambda b, p, i: (b, p, 0, 0)),
            pl.BlockSpec((None, 2, feat, seq), lambda b, p, i: (b, p, 0, 0)),
            pl.BlockSpec((None, tq, 2 * dh), lambda b, p, i: (b, i, p)),
        ],
        out_specs=pl.BlockSpec((None, tq, 2 * dh), lambda b, p, i: (b, i, p)),
        out_shape=jax.ShapeDtypeStruct((bsz, seq, heads * dh), BF16),
        scratch_shapes=[
            pltpu.VMEM((2, 2, tq // LANES, tq, LANES), F32),
            pltpu.VMEM((2, tq // LANES, tq, LANES), BF16),
            pltpu.VMEM((2, feat, tq), F32),
        ],
        compiler_params=_params("parallel", "parallel", "parallel"),
        name="fox_attention",
    )(qt, ka, vt, z)


def _router_kernel(o_ref, wo_ref, x_ref, mod_ref, g_ref, wr_ref, br_ref, xo_ref, h_ref, meta_ref, cnt_ref,
                   carry_ref, *, d, tm, groups, per_group):
    step = pl.program_id(0) * pl.num_programs(1) + pl.program_id(1)

    @pl.when(step == 0)
    def _():
        carry_ref[...] = jnp.zeros(carry_ref.shape, F32)

    x = x_ref[...] + mod_ref[:, 2 * d:3 * d] * jnp.dot(o_ref[...], wo_ref[...], preferred_element_type=F32)
    xo_ref[...] = x
    h = _modulated(x, g_ref[...], mod_ref[:, 4 * d:5 * d], mod_ref[:, 3 * d:4 * d])
    h_ref[...] = h.reshape(h_ref.shape)
    h_hi = h.astype(BF16)
    h_lo = (h - h_hi.astype(F32)).astype(BF16)
    logits = (jnp.dot(h_hi, wr_ref[0], preferred_element_type=F32)
              + (jnp.dot(h_lo, wr_ref[0], preferred_element_type=F32)
                 + jnp.dot(h_hi, wr_ref[1], preferred_element_type=F32))
              + br_ref[...])
    lane = _iota2((tm, LANES), 1)
    neg_inf = jnp.float32(-jnp.inf)
    big = jnp.int32(LANES)

    def first_argmax(vals, mask):
        mv = jnp.where(mask, vals, neg_inf)
        mx = jnp.max(mv, axis=-1, keepdims=True)
        idx = jnp.min(jnp.where(mask & (mv == mx), lane, big), axis=-1, keepdims=True)
        return mx, idx

    gmask = lane < groups
    gmax, gidx = first_argmax(logits, gmask)
    gsum = jnp.sum(jnp.where(gmask, jnp.exp(logits - gmax), 0.0), axis=-1, keepdims=True)
    group_gate = 1.0 / gsum
    lo = groups + gidx * per_group
    emask = (lane >= lo) & (lane < lo + per_group)
    e1max, e1lane = first_argmax(logits, emask)
    e2max, e2lane = first_argmax(logits, emask & (lane != e1lane))
    esum = jnp.sum(jnp.where(emask, jnp.exp(logits - e1max), 0.0), axis=-1, keepdims=True)
    p1 = 1.0 / esum
    p2 = jnp.exp(e2max - e1max) / esum
    psum = p1 + p2
    w1 = group_gate * (p1 / psum)
    w2 = group_gate * (p2 / psum)

    oh1 = (lane == e1lane).astype(F32)
    oh2 = (lane == e2lane).astype(F32)
    strict = jnp.where(_iota2((tm, tm), 0) > _iota2((tm, tm), 1), 1.0, 0.0).astype(BF16)
    c1 = jnp.dot(strict, oh1.astype(BF16), preferred_element_type=F32)
    c2 = jnp.dot(strict, oh2.astype(BF16), preferred_element_type=F32)
    tot1 = jnp.sum(oh1, axis=0, keepdims=True)
    tot2 = jnp.sum(oh2, axis=0, keepdims=True)
    carry = carry_ref[...]
    rank1 = jnp.sum(oh1 * (c1 + carry), axis=-1, keepdims=True)
    rank2 = jnp.sum(oh2 * (c2 + carry + tot1), axis=-1, keepdims=True)
    carry = carry + tot1 + tot2
    carry_ref[...] = carry
    cnt_ref[...] = carry

    e1 = (e1lane - groups).astype(F32)
    e2 = (e2lane - groups).astype(F32)
    mlane = _iota2((tm, 8), 1)
    meta = jnp.where(mlane == 0, e1, 0.0)
    meta = jnp.where(mlane == 1, e2, meta)
    meta = jnp.where(mlane == 2, rank1, meta)
    meta = jnp.where(mlane == 3, rank2, meta)
    meta = jnp.where(mlane == 4, w1, meta)
    meta = jnp.where(mlane == 5, w2, meta)
    meta_ref[...] = meta


def _router(o, w_out, x, mod, gain, w_router, b_router, groups, per_group, tm):
    bsz, seq, d = x.shape
    k = o.shape[-1]
    nb = seq // tm
    return pl.pallas_call(
        functools.partial(_router_kernel, d=d, tm=tm, groups=groups, per_group=per_group),
        grid=(bsz, seq // tm),
        in_specs=[
            pl.BlockSpec((None, tm, k), lambda b, i: (b, i, 0)),
            pl.BlockSpec((k, d), lambda b, i: (0, 0)),
            pl.BlockSpec((None, tm, d), lambda b, i: (b, i, 0)),
            pl.BlockSpec((None, 1, mod.shape[-1]), lambda b, i: (b, 0, 0)),
            pl.BlockSpec((1, d), lambda b, i: (0, 0)),
            pl.BlockSpec((2, d, LANES), lambda b, i: (0, 0, 0)),
            pl.BlockSpec((1, LANES), lambda b, i: (0, 0)),
        ],
        out_specs=[
            pl.BlockSpec((None, tm, d), lambda b, i: (b, i, 0)),
            pl.BlockSpec((tm, SUBLANES, d // SUBLANES), lambda b, i: (b * nb + i, 0, 0)),
            pl.BlockSpec((None, tm, 8), lambda b, i: (b, i, 0)),
            pl.BlockSpec((1, LANES), lambda b, i: (0, 0)),
        ],
        out_shape=[
            jax.ShapeDtypeStruct((bsz, seq, d), F32),
            jax.ShapeDtypeStruct((bsz * seq, SUBLANES, d // SUBLANES), F32),
            jax.ShapeDtypeStruct((bsz, seq, 8), F32),
            jax.ShapeDtypeStruct((1, LANES), F32),
        ],
        scratch_shapes=[pltpu.VMEM((1, LANES), F32)],
        compiler_params=_params("arbitrary", "arbitrary"),
        name="moe_router",
    )(o, w_out, x, mod, gain, w_router, b_router)


DMA_GROUP = 16


def _dispatch_kernel(pad_ref, dest_ref, h_ref, xs_ref, zero_ref, sem, zsem, *, tm, blk, n_exp):
    @pl.when(pl.program_id(0) == 0)
    def _():
        zero_ref[...] = jnp.zeros(zero_ref.shape, zero_ref.dtype)

        def zero_copy(e):
            return pltpu.make_async_copy(zero_ref, xs_ref.at[pl.ds(pad_ref[e] - blk, blk)], zsem)

        def start(e, carry):
            @pl.when(pad_ref[n_exp + e] > 0)
            def _():
                zero_copy(e).start()
            return carry

        def wait(e, carry):
            @pl.when(pad_ref[n_exp + e] > 0)
            def _():
                zero_copy(e).wait()
            return carry

        lax.fori_loop(0, n_exp, start, 0)
        lax.fori_loop(0, n_exp, wait, 0)

        def tail_copy(j):
            return pltpu.make_async_copy(zero_ref, xs_ref.at[pl.ds(j * blk, blk)], zsem)

        def tail_start(j, carry):
            tail_copy(j).start()
            return carry

        def tail_wait(j, carry):
            tail_copy(j).wait()
            return carry

        n_used = pad_ref[2 * n_exp]
        lax.fori_loop(n_used, xs_ref.shape[0] // blk, tail_start, 0)
        lax.fori_loop(n_used, xs_ref.shape[0] // blk, tail_wait, 0)

    def row_copy(r, k):
        return pltpu.make_async_copy(h_ref.at[r], xs_ref.at[dest_ref[r * TOP_K + k]], sem)

    def start(g, carry):
        for s in range(DMA_GROUP):
            for k in range(TOP_K):
                row_copy(g * DMA_GROUP + s, k).start(priority=k % 2)
        return carry

    def wait(g, carry):
        for s in range(DMA_GROUP):
            for k in range(TOP_K):
                row_copy(g * DMA_GROUP + s, k).wait()
        return carry

    lax.fori_loop(0, tm // DMA_GROUP, start, 0)
    lax.fori_loop(0, tm // DMA_GROUP, wait, 0)


def _dispatch(h_tiles, dest, pad_info, rows, tm, blk, n_exp):
    n_tok = h_tiles.shape[0]
    tile = h_tiles.shape[1:]
    return pl.pallas_call(
        functools.partial(_dispatch_kernel, tm=tm, blk=blk, n_exp=n_exp),
        grid_spec=pltpu.PrefetchScalarGridSpec(
            num_scalar_prefetch=1,
            grid=(n_tok // tm,),
            in_specs=[
                pl.BlockSpec((tm * TOP_K,), lambda i, pad: (i,), memory_space=pltpu.SMEM),
                pl.BlockSpec((tm,) + tile, lambda i, pad: (i, 0, 0)),
            ],
            out_specs=pl.BlockSpec(memory_space=pl.ANY),
            scratch_shapes=[
                pltpu.VMEM((blk,) + tile, F32),
                pltpu.SemaphoreType.DMA(()),
                pltpu.SemaphoreType.DMA(()),
            ],
        ),
        out_shape=jax.ShapeDtypeStruct((rows,) + tile, F32),
        compiler_params=_params("arbitrary"),
        name="moe_dispatch",
    )(pad_info, dest, h_tiles)


def _expert_kernel(be_ref, nu_ref, xs_ref, wg_ref, wu_ref, wd_ref, ys_ref, wgb, wub, wdb):
    i = pl.program_id(0)
    prev = be_ref[jnp.maximum(i - 1, 0)]
    fresh = (i == 0) | (be_ref[i] != prev)

    @pl.when(fresh)
    def _():
        wgb[...] = wg_ref[...].astype(BF16)
        wub[...] = wu_ref[...].astype(BF16)
        wdb[...] = wd_ref[...].astype(BF16)

    @pl.when(i < nu_ref[0])
    def _():
        blk = xs_ref.shape[0]
        x = xs_ref[...].reshape(blk, wgb.shape[0]).astype(BF16)
        g = jnp.dot(x, wgb[...], preferred_element_type=F32)
        u = jnp.dot(x, wub[...], preferred_element_type=F32)
        mid = (_silu(g) * u).astype(BF16)
        ys_ref[...] = jnp.dot(mid, wdb[...], preferred_element_type=F32).reshape(ys_ref.shape)

    @pl.when(i >= nu_ref[0])
    def _():
        ys_ref[...] = jnp.zeros(ys_ref.shape, ys_ref.dtype)


def _experts(xs, block_expert, n_used, w_gate, w_up, w_down, layer, blk):
    rows = xs.shape[0]
    tile = xs.shape[1:]
    d, de = w_gate.shape[-2:]
    n_blocks = rows // blk
    return pl.pallas_call(
        _expert_kernel,
        grid_spec=pltpu.PrefetchScalarGridSpec(
            num_scalar_prefetch=2,
            grid=(n_blocks,),
            in_specs=[
                pl.BlockSpec((blk,) + tile, lambda i, be, nu: (jnp.minimum(i, nu[0] - 1), 0, 0)),
                pl.BlockSpec((None, None, d, de), lambda i, be, nu: (layer, be[i], 0, 0)),
                pl.BlockSpec((None, None, d, de), lambda i, be, nu: (layer, be[i], 0, 0)),
                pl.BlockSpec((None, None, de, d), lambda i, be, nu: (layer, be[i], 0, 0)),
            ],
            out_specs=pl.BlockSpec((blk,) + tile, lambda i, be, nu: (i, 0, 0)),
            scratch_shapes=[
                pltpu.VMEM((d, de), BF16),
                pltpu.VMEM((d, de), BF16),
                pltpu.VMEM((de, d), BF16),
            ],
        ),
        out_shape=jax.ShapeDtypeStruct((rows,) + tile, F32),
        compiler_params=_params("arbitrary"),
        name="moe_experts",
    )(block_expert, n_used, xs, w_gate, w_up, w_down)


def _combine_kernel(dest_ref, next_ref, x_ref, mod_ref, meta_ref, ys_ref, o_ref, ybuf, sems, *, tm, d):
    step = pl.program_id(0) * pl.num_programs(1) + pl.program_id(1)
    n_steps = pl.num_programs(0) * pl.num_programs(1)
    slot = step % 2

    def row_copy(idx_ref, buf, r, k):
        return pltpu.make_async_copy(ys_ref.at[idx_ref[r * TOP_K + k]], ybuf.at[buf, k, r], sems.at[buf])

    def start_all(idx_ref, buf):
        def body(g, carry):
            for s in range(DMA_GROUP):
                for k in range(TOP_K):
                    row_copy(idx_ref, buf, g * DMA_GROUP + s, k).start(priority=k % 2)
            return carry
        lax.fori_loop(0, tm // DMA_GROUP, body, 0)

    def wait_all(idx_ref, buf):
        def body(g, carry):
            for s in range(DMA_GROUP):
                for k in range(TOP_K):
                    row_copy(idx_ref, buf, g * DMA_GROUP + s, k).wait()
            return carry
        lax.fori_loop(0, tm // DMA_GROUP, body, 0)

    @pl.when(step == 0)
    def _():
        start_all(dest_ref, 0)

    @pl.when(step + 1 < n_steps)
    def _():
        start_all(next_ref, 1 - slot)

    wait_all(dest_ref, slot)
    meta = meta_ref[...]
    y = meta[:, 4:5] * ybuf[slot, 0].reshape(tm, d) + meta[:, 5:6] * ybuf[slot, 1].reshape(tm, d)
    o_ref[...] = x_ref[...] + mod_ref[:, 5 * d:6 * d] * y


def _combine(dest, x, mod, meta, ys, tm):
    bsz, seq, d = x.shape
    nb = seq // tm
    last = bsz * nb - 1
    return pl.pallas_call(
        functools.partial(_combine_kernel, tm=tm, d=d),
        grid=(bsz, nb),
        in_specs=[
            pl.BlockSpec((tm * TOP_K,), lambda b, i: (b * nb + i,), memory_space=pltpu.SMEM),
            pl.BlockSpec((tm * TOP_K,), lambda b, i: (jnp.minimum(b * nb + i + 1, last),), memory_space=pltpu.SMEM),
            pl.BlockSpec((None, tm, d), lambda b, i: (b, i, 0)),
            pl.BlockSpec((None, 1, mod.shape[-1]), lambda b, i: (b, 0, 0)),
            pl.BlockSpec((None, tm, 8), lambda b, i: (b, i, 0)),
            pl.BlockSpec(memory_space=pl.ANY),
        ],
        out_specs=pl.BlockSpec((None, tm, d), lambda b, i: (b, i, 0)),
        scratch_shapes=[
            pltpu.VMEM((2, TOP_K, tm) + ys.shape[1:], F32),
            pltpu.SemaphoreType.DMA((2,)),
        ],
        out_shape=jax.ShapeDtypeStruct((bsz, seq, d), F32),
        compiler_params=_params("arbitrary", "arbitrary"),
        name="moe_combine",
    )(dest, dest, x, mod, meta, ys)


def _mixer_out_and_moe(o, w_out, x, mod, gain, w_group, b_group, w_expert, b_expert, w_gate, w_up, w_down, layer):
    bsz, seq, d = x.shape
    groups = w_group.shape[1]
    n_exp = w_expert.shape[1]
    per_group = n_exp // groups
    n_tok = bsz * seq
    m = n_tok * TOP_K
    blk = EXPERT_BLOCK
    tm = TOKEN_TILE

    pad = LANES - groups - n_exp
    w_router = jnp.concatenate([w_group, w_expert, jnp.zeros((d, pad), F32)], axis=1)
    w_router_hi = w_router.astype(BF16)
    w_router = jnp.stack([w_router_hi, (w_router - w_router_hi.astype(F32)).astype(BF16)])
    b_router = jnp.concatenate([b_group, b_expert, jnp.zeros((pad,), F32)]).reshape(1, LANES)
    x, h, meta, cnt = _router(o, w_out, x, mod, gain, w_router, b_router, groups, per_group, tm)

    counts = cnt[0, groups:groups + n_exp].astype(I32)
    padded = (counts + blk - 1) // blk * blk
    pad_end = jnp.cumsum(padded)
    pad_start = pad_end - padded
    n_blocks = -(-m // blk) + n_exp
    block_start = jnp.arange(n_blocks, dtype=I32) * blk
    block_expert = jnp.minimum(jnp.sum((pad_end[None, :] <= block_start[:, None]).astype(I32), axis=1), n_exp - 1)
    n_used = (pad_end[-1] // blk).astype(I32).reshape(1)
    meta_flat = meta.reshape(n_tok, 8)
    e_idx = meta_flat[:, 0:TOP_K].astype(I32)
    rank = meta_flat[:, TOP_K:2 * TOP_K].astype(I32)
    start_of = jnp.sum(jnp.where(e_idx[..., None] == jnp.arange(n_exp, dtype=I32), pad_start, 0), axis=-1)
    dest = (start_of + rank).reshape(m)

    pad_info = jnp.concatenate([pad_end, padded, n_used]).astype(I32)
    xs = _dispatch(h, dest, pad_info, n_blocks * blk, DISPATCH_TILE, blk, n_exp)
    ys = _experts(xs, block_expert, n_used, w_gate, w_up, w_down, layer, blk)
    return _combine(dest, x, mod, meta, ys, tm)


def kernel(x, c, mod_w, mod_b, norm_mix_g, norm_ffn_g, gdn_w_in, gdn_conv_w, gdn_a_log, gdn_dt_bias, gdn_out_norm_g, gdn_w_out, kv_mod_w, kv_mod_b, kv_norm_g, kv_w, kv_forget_b, k_norm_g, fox_w_qz, fox_q_norm_g, fox_w_out, moe_w_group, moe_b_group, moe_w_expert, moe_b_expert, moe_w_gate, moe_w_up, moe_w_down):
    bsz, seq, d = x.shape
    depth = mod_w.shape[0]
    n_a = gdn_w_in.shape[0]
    gdn_heads = gdn_a_log.shape[1]
    gdn_inner = gdn_heads * GDN_HEAD_DIM
    fox_heads = kv_forget_b.shape[0]
    fox_inner = fox_heads * FOX_HEAD_DIM
    tm = TOKEN_TILE

    mod_all = _mod_vectors(c, mod_w, mod_b).reshape(depth, bsz, 1, 6 * d)
    kv_mod = _mod_vectors(c, kv_mod_w[None], kv_mod_b[None]).reshape(bsz, 1, 2 * d)
    bd = _block_diag_ones(2 * FOX_HEAD_DIM, FOX_HEAD_DIM)

    ka_sh = vt_sh = ft_sh = None
    for layer in range(depth):
        mod = mod_all[layer]
        gain_mix = norm_mix_g[layer].reshape(1, d)
        if layer < n_a:
            w_in = gdn_w_in[layer]
            qkv, sz, ab = _gdn_in_proj(x, mod, gain_mix, w_in[:, :4 * gdn_inner].astype(BF16),
                                       w_in[:, 4 * gdn_inner:].astype(BF16), gdn_conv_w[layer], gdn_heads, tm)
            o = _gdn_core(qkv, sz, ab, gdn_a_log[layer], gdn_dt_bias[layer],
                          gdn_out_norm_g[layer], gdn_heads, GDN_TIME_BLOCK)
            w_out = gdn_w_out[layer]
        else:
            j = layer - n_a
            wqz = fox_w_qz[j]
            qg = jnp.tile(fox_q_norm_g[j], fox_heads).reshape(1, fox_inner)
            qt, z = _fox_qz_proj(x, mod, gain_mix, wqz[:, :fox_inner].astype(BF16),
                                 wqz[:, fox_inner:].astype(BF16), qg, bd, ft_sh, fox_heads, tm)
            o = _fox_attention(qt, ka_sh, vt_sh, z, FOX_BLOCK)
            w_out = fox_w_out[j]
        x = _mixer_out_and_moe(o, w_out.astype(BF16), x, mod, norm_ffn_g[layer].reshape(1, d),
                               moe_w_group[layer], moe_b_group[layer], moe_w_expert[layer], moe_b_expert[layer],
                               moe_w_gate, moe_w_up, moe_w_down, layer)
        if layer == n_a - 1:
            kg = jnp.tile(k_norm_g, fox_heads).reshape(1, fox_inner)
            ka_sh, vt_sh, ft_sh = _shared_kv(
                x, kv_mod, kv_norm_g.reshape(1, d), kv_w[:, :fox_inner].astype(BF16),
                kv_w[:, fox_inner:2 * fox_inner].astype(BF16), kv_w[:, 2 * fox_inner:].astype(BF16),
                kv_forget_b.reshape(1, fox_heads), kg, bd, fox_heads, tm)
    return x
```
